```python
import math
import jax, jax.numpy as jnp
from jax import lax
import numpy as np

D_MODEL = 2048
BATCH = 8
SEQ = 2048
DEPTH = 1
DEC_BATCH = 4
DEC_SEQ = 4096
PAST_LEN = 128

HEAD_DIM = 128
A_HEADS = 8
A_KV_HEADS = 2
ROPE_THETA = 10000.0
GRID_W = 64
Q_BLOCK = 128
B_PAIRS = ((128, 1), (512, 4), (2048, 16))
B_GROUPS = len(B_PAIRS)
B_HEADS_PER_GROUP = 4
B_HEADS = B_GROUPS * B_HEADS_PER_GROUP
REL_BUCKETS = 32
REL_MAX_DIST = 1024
A_Q = A_HEADS * HEAD_DIM
A_KV = A_KV_HEADS * HEAD_DIM
B_W = B_HEADS * HEAD_DIM
B_OUT = B_HEADS_PER_GROUP * HEAD_DIM
IN_COLS = A_Q + 2 * A_KV + 3 * B_W + 2 * D_MODEL
N_GROUPS = 8
EXPERTS_PER_GROUP = 8
N_EXPERTS = N_GROUPS * EXPERTS_PER_GROUP
TOP_K = 2
EXPERT_FF = 1024
MOE_BLOCK = 128
EPS = 1e-6
NEG = -1e30

kernel_name = 'hybrid_gated_encoder_gqa_dilated_hmoe'


def rmsnorm(x, g):
    x32 = x.astype(jnp.float32)
    y = x32 * lax.rsqrt(jnp.mean(x32 * x32, axis=-1, keepdims=True) + EPS)
    return (y * g.astype(jnp.float32)).astype(x.dtype)


def rope_axis(x, pos):
    half = x.shape[-1] // 2
    inv = jnp.power(ROPE_THETA, -jnp.arange(half, dtype=jnp.float32) / half)
    ang = pos[:, None] * inv[None, :]
    cos = jnp.cos(ang)[None, :, None, :]
    sin = jnp.sin(ang)[None, :, None, :]
    x32 = x.astype(jnp.float32)
    x1, x2 = x32[..., :half], x32[..., half:]
    return jnp.concatenate([x1 * cos - x2 * sin, x2 * cos + x1 * sin], axis=-1).astype(x.dtype)


def axial_rope(x):
    s = x.shape[1]
    rows = s // GRID_W
    row = jnp.repeat(jnp.arange(rows, dtype=jnp.float32), GRID_W)
    col = jnp.tile(jnp.arange(GRID_W, dtype=jnp.float32), rows)
    xr, xc = jnp.split(x, 2, axis=-1)
    return jnp.concatenate([rope_axis(xr, row), rope_axis(xc, col)], axis=-1)


def grid_attention(q, k, v):
    b, s, h, dh = q.shape
    hkv = k.shape[2]
    g = h // hkv
    nq = s // Q_BLOCK
    qb = q.reshape(b, nq, Q_BLOCK, hkv, g, dh).transpose(1, 0, 2, 3, 4, 5)
    scale = dh ** -0.5

    def one_block(qi):
        sc = jnp.einsum('bqkgd,bskd->bkgqs', qi, k, preferred_element_type=jnp.float32) * scale
        p = jax.nn.softmax(sc, axis=-1).astype(v.dtype)
        return jnp.einsum('bkgqs,bskd->bqkgd', p, v)

    o = lax.map(one_block, qb)
    return o.transpose(1, 0, 2, 3, 4, 5).reshape(b, s, h * dh)


def t5_bucket(rel):
    nb = REL_BUCKETS // 2
    ret = (rel > 0).astype(np.int32) * nb
    n = np.abs(rel)
    max_exact = nb // 2
    large = max_exact + (np.log(np.maximum(n, 1) / max_exact) / np.log(REL_MAX_DIST / max_exact)
                         * (nb - max_exact)).astype(np.int32)
    large = np.minimum(large, nb - 1)
    return (ret + np.where(n < max_exact, n, large)).astype(np.int32)


def dilated_group(q, k, v, table_g, window, dilation):
    b, s, h, dh = q.shape
    n = (window // 2) // dilation
    length = s // dilation
    nb = -(-length // n)
    lp = nb * n

    def to_res(t):
        return t.reshape(b, length, dilation, h, dh).transpose(0, 2, 3, 1, 4)

    qr = jnp.pad(to_res(q), ((0, 0), (0, 0), (0, 0), (0, lp - length), (0, 0))).reshape(b, dilation, h, nb, n, dh)

    def windows(t):
        tp = jnp.pad(to_res(t), ((0, 0), (0, 0), (0, 0), (n, lp - length + n), (0, 0)))
        tp = tp.reshape(b, dilation, h, nb + 2, n, dh)
        return jnp.concatenate([tp[:, :, :, :-2], tp[:, :, :, 1:-1], tp[:, :, :, 2:]], axis=4)

    kw = windows(k)
    vw = windows(v)
    qi = np.arange(n)[:, None]
    kj = np.arange(3 * n)[None, :]
    m = kj - n - qi
    band = np.abs(m) <= n
    bucket = t5_bucket(m * dilation)
    bias = jnp.take(table_g, jnp.asarray(bucket), axis=0).astype(jnp.float32)
    bias = bias.transpose(2, 0, 1)
    kpos = (np.arange(nb)[:, None, None] - 1) * n + kj[None]
    valid = jnp.asarray(band[None] & (kpos >= 0) & (kpos < length))
    sc = jnp.einsum('bdhcqe,bdhcke->bdhcqk', qr, kw, preferred_element_type=jnp.float32) * (dh ** -0.5)
    sc = jnp.where(valid[None, None, None], sc + bias[None, None, :, None], NEG)
    mx = jnp.max(sc, axis=-1, keepdims=True)
    p = jnp.exp(sc - mx)
    den = jnp.sum(p, axis=-1, keepdims=True)
    o = jnp.einsum('bdhcqk,bdhcke->bdhcqe', (p / den).astype(v.dtype), vw)
    lse = (mx + jnp.log(den))[..., 0]
    o = o.reshape(b, dilation, h, lp, dh)[:, :, :, :length].transpose(0, 3, 1, 2, 4).reshape(b, s, h, dh)
    lse = lse.reshape(b, dilation, h, lp)[..., :length].transpose(0, 3, 1, 2).reshape(b, s, h)
    return o, lse


def dilated_mixture(q, k, v, rel_bias):
    b, s, _ = q.shape
    q = q.reshape(b, s, B_HEADS, HEAD_DIM)
    k = k.reshape(b, s, B_HEADS, HEAD_DIM)
    v = v.reshape(b, s, B_HEADS, HEAD_DIM)
    outs, lses = [], []
    for gi, (window, dilation) in enumerate(B_PAIRS):
        sl = slice(gi * B_HEADS_PER_GROUP, (gi + 1) * B_HEADS_PER_GROUP)
        o, lse = dilated_group(q[:, :, sl], k[:, :, sl], v[:, :, sl], rel_bias[:, sl], window, dilation)
        outs.append(o)
        lses.append(lse)
    wts = jax.nn.softmax(jnp.stack(lses, axis=0), axis=0)[..., None]
    out = jnp.sum(wts * jnp.stack(outs, axis=0).astype(jnp.float32), axis=0).astype(q.dtype)
    return out.reshape(b, s, B_OUT)


def token_mixer(h, w_in, q_norm, k_norm, w_branch_a, w_branch_b, w_out, rel_bias):
    b, s, _ = h.shape
    proj = h @ w_in
    cuts = np.cumsum([A_Q, A_KV, A_KV, B_W, B_W, B_W, D_MODEL]).tolist()
    qa, ka, va, qb, kb, vb, ga, gb = jnp.split(proj, cuts, axis=-1)
    qa = axial_rope(rmsnorm(qa.reshape(b, s, A_HEADS, HEAD_DIM), q_norm))
    ka = axial_rope(rmsnorm(ka.reshape(b, s, A_KV_HEADS, HEAD_DIM), k_norm))
    va = va.reshape(b, s, A_KV_HEADS, HEAD_DIM)
    ya = grid_attention(qa, ka, va)
    yb = dilated_mixture(qb, kb, vb, rel_bias)
    merged = jax.nn.sigmoid(ga) * (ya @ w_branch_a) + jax.nn.sigmoid(gb) * (yb @ w_branch_b)
    return merged @ w_out


def expert_dispatch(x, experts, weights, w_gate, w_up, w_down):
    n_tok, d = x.shape
    a = n_tok * TOP_K
    flat_e = experts.reshape(a).astype(jnp.int32)
    flat_tok = jnp.repeat(jnp.arange(n_tok, dtype=jnp.int32), TOP_K)
    flat_w = weights.reshape(a)
    order = jnp.argsort(flat_e)
    se = flat_e[order]
    counts = jnp.bincount(flat_e, length=N_EXPERTS).astype(jnp.int32)
    padded = (counts + MOE_BLOCK - 1) // MOE_BLOCK * MOE_BLOCK
    starts = jnp.cumsum(counts) - counts
    pend = jnp.cumsum(padded)
    pstarts = pend - padded
    dest = pstarts[se] + jnp.arange(a, dtype=jnp.int32) - starts[se]
    n_blocks = -(-a // MOE_BLOCK) + N_EXPERTS
    n_slots = n_blocks * MOE_BLOCK
    slot_tok = jnp.full((n_slots,), n_tok, jnp.int32).at[dest].set(flat_tok[order])
    slot_w = jnp.zeros((n_slots,), x.dtype).at[dest].set(flat_w[order])
    block_e = jnp.minimum(jnp.searchsorted(pend, jnp.arange(n_blocks, dtype=jnp.int32) * MOE_BLOCK, side='right'),
                          N_EXPERTS - 1)
    xp = jnp.concatenate([x, jnp.zeros((1, d), x.dtype)], axis=0)

    def run_block(args):
        tok, e = args
        xb = xp[tok]
        hb = jax.nn.silu(xb @ w_gate[e]) * (xb @ w_up[e])
        return hb @ w_down[e]

    out = lax.map(run_block, (slot_tok.reshape(n_blocks, MOE_BLOCK), block_e))
    out = out.reshape(n_slots, d) * slot_w[:, None]
    return jnp.zeros((n_tok + 1, d), x.dtype).at[slot_tok].add(out)[:n_tok]


def hier_moe(h, w_rg, b_rg, w_re, b_re, w_gate, w_up, w_down):
    b, s, d = h.shape
    x = h.reshape(b * s, d)
    n_tok = x.shape[0]
    glog = (x @ w_rg).astype(jnp.float32) + b_rg.astype(jnp.float32)
    pg = jax.nn.softmax(glog, axis=-1)
    grp = jnp.argmax(glog, axis=-1)
    gate_g = jnp.take_along_axis(pg, grp[:, None], axis=1)
    elog = ((x @ w_re).astype(jnp.float32) + b_re.astype(jnp.float32)).reshape(n_tok, N_GROUPS, EXPERTS_PER_GROUP)
    within = jnp.take_along_axis(elog, grp[:, None, None], axis=1)[:, 0]
    top_v, top_i = lax.top_k(within, TOP_K)
    weights = (gate_g * jax.nn.softmax(top_v, axis=-1)).astype(x.dtype)
    experts = grp[:, None] * EXPERTS_PER_GROUP + top_i
    y = expert_dispatch(x, experts, weights, w_gate, w_up, w_down)
    return y.reshape(b, s, d)


def encoder_trunk(x, c, w_ada, b_ada, norm1, w_in, q_norm, k_norm, w_branch_a, w_branch_b, w_out, rel_bias,
                  norm2, w_route_group, b_route_group, w_route_expert, b_route_expert,
                  w_gate_exp, w_up_exp, w_down_exp, norm_final):
    for l in range(DEPTH):
        mod = jax.nn.silu(c) @ w_ada[l] + b_ada[l]
        sh1, sc1, g1, sh2, sc2, g2 = jnp.split(mod[:, None, :], 6, axis=-1)
        h = rmsnorm(x, norm1[l]) * (1 + sc1) + sh1
        x = x + g1 * token_mixer(h, w_in[l], q_norm[l], k_norm[l], w_branch_a[l], w_branch_b[l], w_out[l], rel_bias)
        h = rmsnorm(x, norm2[l]) * (1 + sc2) + sh2
        x = x + g2 * hier_moe(h, w_route_group[l], b_route_group[l], w_route_expert[l], b_route_expert[l],
                              w_gate_exp[l], w_up_exp[l], w_down_exp[l])
    return rmsnorm(x, norm_final)


def setup_inputs(seed: int = 0) -> dict:
    key = jax.random.key(seed)
    ks = jax.random.split(key, 24)

    def nrm(k, shape, scale):
        return jax.random.normal(k, shape, jnp.float32) * scale

    d = D_MODEL
    return {
        'x_prompt': nrm(ks[0], (BATCH, SEQ, d), 1.0),
        'x_sample': nrm(ks[1], (DEC_BATCH, DEC_SEQ, d), 1.0),
        'c_prompt': nrm(ks[2], (BATCH, d), 1.0),
        'c_sample': nrm(ks[3], (DEC_BATCH, d), 1.0),
        'w_ada': nrm(ks[4], (DEPTH, d, 6 * d), d ** -0.5),
        'b_ada': nrm(ks[5], (DEPTH, 6 * d), 0.02),
        'norm1': 1.0 + nrm(ks[6], (DEPTH, d), 0.02),
        'w_in': nrm(ks[7], (DEPTH, d, IN_COLS), d ** -0.5),
        'q_norm': 1.0 + nrm(ks[8], (DEPTH, HEAD_DIM), 0.02),
        'k_norm': 1.0 + nrm(ks[9], (DEPTH, HEAD_DIM), 0.02),
        'w_branch_a': nrm(ks[10], (DEPTH, A_Q, d), A_Q ** -0.5),
        'w_branch_b': nrm(ks[11], (DEPTH, B_OUT, d), B_OUT ** -0.5),
        'w_out': nrm(ks[12], (DEPTH, d, d), d ** -0.5),
        'rel_bias': nrm(ks[13], (REL_BUCKETS, B_HEADS), 0.5),
        'norm2': 1.0 + nrm(ks[14], (DEPTH, d), 0.02),
        'w_route_group': nrm(ks[15], (DEPTH, d, N_GROUPS), d ** -0.5),
        'b_route_group': nrm(ks[16], (DEPTH, N_GROUPS), 0.01),
        'w_route_expert': nrm(ks[17], (DEPTH, d, N_EXPERTS), d ** -0.5),
        'b_route_expert': nrm(ks[18], (DEPTH, N_EXPERTS), 0.01),
        'w_gate_exp': nrm(ks[19], (DEPTH, N_EXPERTS, d, EXPERT_FF), d ** -0.5),
        'w_up_exp': nrm(ks[20], (DEPTH, N_EXPERTS, d, EXPERT_FF), d ** -0.5),
        'w_down_exp': nrm(ks[21], (DEPTH, N_EXPERTS, EXPERT_FF, d), EXPERT_FF ** -0.5),
        'norm_final': 1.0 + nrm(ks[22], (d,), 0.02),
    }


def reference(x_prompt, x_sample, c_prompt, c_sample, w_ada, b_ada, norm1, w_in, q_norm, k_norm,
              w_branch_a, w_branch_b, w_out, rel_bias, norm2, w_route_group, b_route_group,
              w_route_expert, b_route_expert, w_gate_exp, w_up_exp, w_down_exp, norm_final):
    y_prompt = encoder_trunk(x_prompt, c_prompt, w_ada, b_ada, norm1, w_in, q_norm, k_norm, w_branch_a, w_branch_b,
                             w_out, rel_bias, norm2, w_route_group, b_route_group, w_route_expert, b_route_expert,
                             w_gate_exp, w_up_exp, w_down_exp, norm_final)
    y_sample = encoder_trunk(x_sample, c_sample, w_ada, b_ada, norm1, w_in, q_norm, k_norm, w_branch_a, w_branch_b,
                             w_out, rel_bias, norm2, w_route_group, b_route_group, w_route_expert, b_route_expert,
                             w_gate_exp, w_up_exp, w_down_exp, norm_final)
    return (y_prompt, y_sample)
```

```python
import functools
import math

import numpy as np
import jax
import jax.numpy as jnp
from jax import lax
from jax.experimental import pallas as pl
from jax.experimental.pallas import tpu as pltpu

F32 = jnp.float32
MXU_DTYPE = jnp.bfloat16

HEAD_DIM = 128
A_HEADS = 8
A_KV_HEADS = 2
A_GROUP = A_HEADS // A_KV_HEADS
ROPE_THETA = 10000.0
GRID_W = 64
B_PAIRS = ((128, 1), (512, 4), (2048, 16))
B_GROUPS = len(B_PAIRS)
B_HEADS_PER_GROUP = 4
B_HEADS = B_GROUPS * B_HEADS_PER_GROUP
REL_BUCKETS = 32
REL_MAX_DIST = 1024
A_Q = A_HEADS * HEAD_DIM
A_KV = A_KV_HEADS * HEAD_DIM
B_W = B_HEADS * HEAD_DIM
B_OUT = B_HEADS_PER_GROUP * HEAD_DIM
N_GROUPS = 8
EXPERTS_PER_GROUP = 8
N_EXPERTS = N_GROUPS * EXPERTS_PER_GROUP
TOP_K = 2
EPS = 1e-6
NEG = -1e30
LANES = 128

VMEM_LIMIT_BYTES = 56 * 1024 * 1024

PROJ_TM = 512
PROJ_TN = 512
ATT_TQ = 256
DIL_TQ = 128
DIL_SIDE = 64
POST_TM = 256
MOE_TM = 256
DISP_TM = 256
COMB_TM = 128
RANK_CHUNK = 512


def _params(sem):
    return pltpu.CompilerParams(dimension_semantics=sem, vmem_limit_bytes=VMEM_LIMIT_BYTES)


def _ada_kernel(c_ref, w_ref, b_ref, o_ref):
    c = c_ref[...]
    a = (c * jax.nn.sigmoid(c)).astype(MXU_DTYPE)
    o_ref[...] = jnp.dot(a, w_ref[...].astype(MXU_DTYPE), preferred_element_type=F32) + b_ref[...]


def _adaln(c_all, w, b):
    r, d = c_all.shape
    n = w.shape[1]
    tn = min(n, 1024)
    return pl.pallas_call(
        _ada_kernel,
        grid=(n // tn,),
        in_specs=[pl.BlockSpec((r, d), lambda j: (0, 0)),
                  pl.BlockSpec((d, tn), lambda j: (0, j)),
                  pl.BlockSpec((1, tn), lambda j: (0, j))],
        out_specs=pl.BlockSpec((r, tn), lambda j: (0, j)),
        out_shape=jax.ShapeDtypeStruct((r, n), F32),
        compiler_params=_params(("arbitrary",)),
        name="adaln",
    )(c_all, w, b.reshape(1, n))


def _rms(x):
    return x * lax.rsqrt(jnp.mean(x * x, axis=-1, keepdims=True) + EPS)


def _norm_rope(x, g, cos, sin):
    y = _rms(x) * g
    lane = lax.broadcasted_iota(jnp.int32, y.shape, 1)
    r32 = pltpu.roll(y, 32, 1)
    r96 = pltpu.roll(y, 96, 1)
    partner = jnp.where(pltpu.roll(lane, 32, 1) == (lane ^ 32), r32, r96)
    return y * cos + partner * sin


def _inproj_kernel(x_ref, mod_ref, n1_ref, w_ref, cos_ref, sin_ref, qn_ref, kn_ref,
                   qa_ref, kva_ref, qkvb_ref, g_ref, h_scr, *, j_kv, j_b, j_bq_end, j_g):
    j = pl.program_id(1)

    @pl.when(j == 0)
    def _():
        h = _rms(x_ref[...]) * n1_ref[...]
        h = h * (1.0 + mod_ref[1:2, :]) + mod_ref[0:1, :]
        h_scr[...] = h.astype(h_scr.dtype)

    acc = jnp.dot(h_scr[...], w_ref[...], preferred_element_type=F32)
    heads = PROJ_TN // HEAD_DIM

    def head(a, hh):
        return a[:, hh * HEAD_DIM:(hh + 1) * HEAD_DIM]

    @pl.when(j < j_kv)
    def _():
        for hh in range(heads):
            qa_ref[:, hh * HEAD_DIM:(hh + 1) * HEAD_DIM] = _norm_rope(
                head(acc, hh), qn_ref[...], cos_ref[...], sin_ref[...]).astype(qa_ref.dtype)

    @pl.when(j == j_kv)
    def _():
        for hh in range(heads):
            if hh < A_KV_HEADS:
                val = _norm_rope(head(acc, hh), kn_ref[...], cos_ref[...], sin_ref[...])
            else:
                val = head(acc, hh)
            kva_ref[:, hh * HEAD_DIM:(hh + 1) * HEAD_DIM] = val.astype(kva_ref.dtype)

    @pl.when(jnp.logical_and(j >= j_b, j < j_g))
    def _():
        scale = jnp.where(j < j_bq_end, HEAD_DIM ** -0.5, 1.0).astype(F32)
        qkvb_ref[...] = (acc * scale).astype(qkvb_ref.dtype)

    @pl.when(j >= j_g)
    def _():
        g_ref[...] = jax.nn.sigmoid(acc).astype(g_ref.dtype)


def _rope_tables(s):
    half = HEAD_DIM // 4
    inv = np.power(ROPE_THETA, -np.arange(half, dtype=np.float64) / half)
    t = np.arange(s)
    row = (t // GRID_W).astype(np.float64)
    col = (t % GRID_W).astype(np.float64)
    ang_r = row[:, None] * inv[None, :]
    ang_c = col[:, None] * inv[None, :]
    cos = np.concatenate([np.cos(ang_r), np.cos(ang_r), np.cos(ang_c), np.cos(ang_c)], axis=1)
    sin = np.concatenate([-np.sin(ang_r), np.sin(ang_r), -np.sin(ang_c), np.sin(ang_c)], axis=1)
    return jnp.asarray(cos, F32), jnp.asarray(sin, F32)


def _inproj(x2, mod3, boff, seq, n1, w_in, qn, kn):
    t, d = x2.shape
    tm, tn = PROJ_TM, PROJ_TN
    assert t % tm == 0 and seq % tm == 0
    assert A_Q % tn == 0 and 2 * A_KV == tn and (3 * B_W) % tn == 0 and B_W % tn == 0 and d % tn == 0
    j_kv = A_Q // tn
    j_b = j_kv + 1
    j_bq_end = j_b + B_W // tn
    j_g = j_b + 3 * B_W // tn
    n_j = j_g + 2 * d // tn
    assert w_in.shape == (d, n_j * tn)
    cos, sin = _rope_tables(seq)
    spb = seq // tm
    kern = functools.partial(_inproj_kernel, j_kv=j_kv, j_b=j_b, j_bq_end=j_bq_end, j_g=j_g)
    return pl.pallas_call(
        kern,
        grid=(t // tm, n_j),
        in_specs=[
            pl.BlockSpec((tm, d), lambda i, j: (i, 0)),
            pl.BlockSpec((None, 6, d), lambda i, j: (boff + i // spb, 0, 0)),
            pl.BlockSpec((1, d), lambda i, j: (0, 0)),
            pl.BlockSpec((d, tn), lambda i, j: (0, j)),
            pl.BlockSpec((tm, HEAD_DIM), lambda i, j: (i % spb, 0)),
            pl.BlockSpec((tm, HEAD_DIM), lambda i, j: (i % spb, 0)),
            pl.BlockSpec((1, HEAD_DIM), lambda i, j: (0, 0)),
            pl.BlockSpec((1, HEAD_DIM), lambda i, j: (0, 0)),
        ],
        out_specs=[
            pl.BlockSpec((tm, tn), lambda i, j: (i, jnp.clip(j, 0, j_kv - 1))),
            pl.BlockSpec((tm, tn), lambda i, j: (i, 0)),
            pl.BlockSpec((tm, tn), lambda i, j: (i, jnp.clip(j - j_b, 0, j_g - j_b - 1))),
            pl.BlockSpec((tm, tn), lambda i, j: (i, jnp.clip(j - j_g, 0, n_j - j_g - 1))),
        ],
        out_shape=[
            jax.ShapeDtypeStruct((t, A_Q), MXU_DTYPE),
            jax.ShapeDtypeStruct((t, 2 * A_KV), MXU_DTYPE),
            jax.ShapeDtypeStruct((t, 3 * B_W), MXU_DTYPE),
            jax.ShapeDtypeStruct((t, 2 * d), MXU_DTYPE),
        ],
        scratch_shapes=[pltpu.VMEM((tm, d), MXU_DTYPE)],
        compiler_params=_params(("arbitrary", "arbitrary")),
        name="inproj",
    )(x2, mod3, n1.reshape(1, d), w_in, cos, sin, qn.reshape(1, HEAD_DIM), kn.reshape(1, HEAD_DIM))


def _attn_a_kernel(q_ref, k_ref, v_ref, o_ref):
    k = k_ref[...]
    v = v_ref[...]
    for hh in range(A_GROUP):
        sl = slice(hh * HEAD_DIM, (hh + 1) * HEAD_DIM)
        s = lax.dot_general(q_ref[:, sl], k, (((1,), (1,)), ((), ())), preferred_element_type=F32)
        m = jnp.max(s, axis=-1, keepdims=True)
        p = jnp.exp(s - m)
        l = jnp.sum(p, axis=-1, keepdims=True)
        o = jnp.dot(p.astype(v.dtype), v, preferred_element_type=F32)
        o_ref[:, sl] = (o / l).astype(o_ref.dtype)


def _attn_a(qa, kva, batch, seq):
    t = qa.shape[0]
    tq = min(ATT_TQ, seq)
    nq = seq // tq
    gw = A_GROUP * HEAD_DIM
    return pl.pallas_call(
        _attn_a_kernel,
        grid=(batch, A_KV_HEADS, nq),
        in_specs=[
            pl.BlockSpec((tq, gw), lambda b, kh, qi: (b * nq + qi, kh)),
            pl.BlockSpec((seq, HEAD_DIM), lambda b, kh, qi: (b, kh)),
            pl.BlockSpec((seq, HEAD_DIM), lambda b, kh, qi: (b, A_KV_HEADS + kh)),
        ],
        out_specs=pl.BlockSpec((tq, gw), lambda b, kh, qi: (b * nq + qi, kh)),
        out_shape=jax.ShapeDtypeStruct((t, A_Q), MXU_DTYPE),
        compiler_params=_params(("arbitrary", "arbitrary", "arbitrary")),
        name="attn_dense",
    )(qa, kva, kva)


def _t5_bucket(rel):
    nb = REL_BUCKETS // 2
    ret = (rel > 0).astype(np.int32) * nb
    n = np.abs(rel)
    max_exact = nb // 2
    large = max_exact + (np.log(np.maximum(n, 1) / max_exact) / np.log(REL_MAX_DIST / max_exact)
                         * (nb - max_exact)).astype(np.int32)
    large = np.minimum(large, nb - 1)
    return (ret + np.where(n < max_exact, n, large)).astype(np.int32)


def _band_bias(table_g, dilation, tq, w):
    qi = np.arange(tq)[:, None]
    kj = np.arange(w)[None, :]
    out = []
    for var in range(3):
        m = kj - qi - DIL_SIDE * var
        band = np.abs(m) <= DIL_SIDE
        bucket = _t5_bucket(m * dilation)
        bias = jnp.take(table_g, jnp.asarray(bucket), axis=0).astype(F32)
        bias = jnp.where(jnp.asarray(band)[:, :, None], bias, NEG)
        out.append(bias.transpose(2, 0, 1))
    return jnp.stack(out, axis=0)


def _dil_kernel(q_ref, k_ref, v_ref, bias_ref, o_ref, lse_ref, *, length, tq, w):
    nblk = length // tq
    lane = lax.broadcasted_iota(jnp.int32, (tq, LANES), 1)

    def body(c, carry):
        q0 = pl.multiple_of(c * tq, tq)
        ws = pl.multiple_of(jnp.clip(q0 - DIL_SIDE, 0, length - w), DIL_SIDE)
        var = (q0 - ws) // DIL_SIDE
        lse_blk = jnp.zeros((tq, LANES), F32)
        for hh in range(B_HEADS_PER_GROUP):
            sl = slice(hh * HEAD_DIM, (hh + 1) * HEAD_DIM)
            q = q_ref[pl.ds(q0, tq), sl]
            kw = k_ref[pl.ds(ws, w), sl]
            vw = v_ref[pl.ds(ws, w), sl]
            s = lax.dot_general(q, kw, (((1,), (1,)), ((), ())), preferred_element_type=F32)
            s = s + bias_ref[var, hh]
            m = jnp.max(s, axis=-1, keepdims=True)
            p = jnp.exp(s - m)
            l = jnp.sum(p, axis=-1, keepdims=True)
            o = jnp.dot(p.astype(vw.dtype), vw, preferred_element_type=F32)
            o_ref[pl.ds(q0, tq), sl] = (o / l).astype(o_ref.dtype)
            lse_blk = jnp.where(lane == hh, m + jnp.log(l), lse_blk)
        lse_ref[pl.ds(q0, tq), :] = lse_blk
        return carry

    lax.fori_loop(0, nblk, body, 0)


def _dilated_group(qkvb, table_g, gi, window, dilation, batch, seq):
    assert (window // 2) // dilation == DIL_SIDE
    t = qkvb.shape[0]
    length = seq // dilation
    assert seq % dilation == 0 and length % DIL_SIDE == 0
    tq = min(DIL_TQ, length)
    w = min(length, tq + 2 * DIL_SIDE)
    assert length % tq == 0
    gw = B_HEADS_PER_GROUP * HEAD_DIM
    cols = 3 * B_W // gw
    qo, ko, vo = gi, B_HEADS // B_HEADS_PER_GROUP + gi, 2 * (B_HEADS // B_HEADS_PER_GROUP) + gi
    x3 = qkvb.reshape(batch, length, dilation * 3 * B_W)
    bias = _band_bias(table_g, dilation, tq, w)
    kern = functools.partial(_dil_kernel, length=length, tq=tq, w=w)
    o, lse = pl.pallas_call(
        kern,
        grid=(batch, dilation),
        in_specs=[
            pl.BlockSpec((None, length, gw), lambda b, r: (b, 0, r * cols + qo)),
            pl.BlockSpec((None, length, gw), lambda b, r: (b, 0, r * cols + ko)),
            pl.BlockSpec((None, length, gw), lambda b, r: (b, 0, r * cols + vo)),
            pl.BlockSpec((3, B_HEADS_PER_GROUP, tq, w), lambda b, r: (0, 0, 0, 0)),
        ],
        out_specs=[
            pl.BlockSpec((None, length, gw), lambda b, r: (b, 0, r)),
            pl.BlockSpec((None, length, LANES), lambda b, r: (b, 0, r)),
        ],
        out_shape=[
            jax.ShapeDtypeStruct((batch, length, dilation * gw), MXU_DTYPE),
            jax.ShapeDtypeStruct((batch, length, dilation * LANES), F32),
        ],
        compiler_params=_params(("arbitrary", "arbitrary")),
        name=f"attn_dilated_{dilation}",
    )(x3, x3, x3, bias)
    return o.reshape(t, gw), lse.reshape(t, LANES)


def _post_kernel(x_ref, ya_ref, o1_ref, o2_ref, o3_ref, l1_ref, l2_ref, l3_ref, ga_ref, gb_ref,
                 mod_ref, n2_ref, wa_ref, wb_ref, wo_ref, wr_ref, br_ref,
                 x1_ref, h2_ref, rt_ref):
    lses = [l1_ref[...], l2_ref[...], l3_ref[...]]
    outs = [o1_ref, o2_ref, o3_ref]
    parts = []
    for jh in range(B_HEADS_PER_GROUP):
        sl = slice(jh * HEAD_DIM, (jh + 1) * HEAD_DIM)
        lj = [lg[:, jh:jh + 1] for lg in lses]
        mx = jnp.maximum(jnp.maximum(lj[0], lj[1]), lj[2])
        ex = [jnp.exp(v - mx) for v in lj]
        inv = 1.0 / (ex[0] + ex[1] + ex[2])
        acc = (ex[0] * inv) * outs[0][:, sl].astype(F32)
        acc = acc + (ex[1] * inv) * outs[1][:, sl].astype(F32)
        acc = acc + (ex[2] * inv) * outs[2][:, sl].astype(F32)
        parts.append(acc.astype(MXU_DTYPE))
    yb = jnp.concatenate(parts, axis=1)
    a = jnp.dot(ya_ref[...], wa_ref[...], preferred_element_type=F32)
    bb = jnp.dot(yb, wb_ref[...], preferred_element_type=F32)
    merged = ga_ref[...].astype(F32) * a + gb_ref[...].astype(F32) * bb
    out = jnp.dot(merged.astype(MXU_DTYPE), wo_ref[...], preferred_element_type=F32)
    x1 = x_ref[...] + mod_ref[2:3, :] * out
    x1_ref[...] = x1
    h2 = _rms(x1) * n2_ref[...]
    h2 = h2 * (1.0 + mod_ref[4:5, :]) + mod_ref[3:4, :]
    h2_ref[...] = h2

    r = jnp.dot(h2, wr_ref[...], preferred_element_type=F32, precision=lax.Precision.HIGHEST) + br_ref[...]
    lane = lax.broadcasted_iota(jnp.int32, r.shape, 1)
    lanef = lane.astype(F32)
    ninf = -jnp.inf
    gmask = lane < N_GROUPS
    gl = jnp.where(gmask, r, ninf)
    gmax = jnp.max(gl, axis=-1, keepdims=True)
    grp = jnp.min(jnp.where(gl == gmax, lanef, float(LANES)), axis=-1, keepdims=True)
    gsum = jnp.sum(jnp.where(gmask, jnp.exp(jnp.where(gmask, r, gmax) - gmax), 0.0), axis=-1, keepdims=True)
    gate = 1.0 / gsum
    lo = N_GROUPS + grp * EXPERTS_PER_GROUP
    emask = jnp.logical_and(lanef >= lo, lanef < lo + EXPERTS_PER_GROUP)
    el = jnp.where(emask, r, ninf)
    v1 = jnp.max(el, axis=-1, keepdims=True)
    i1 = jnp.min(jnp.where(el == v1, lanef, float(LANES)), axis=-1, keepdims=True)
    el2 = jnp.where(lanef == i1, ninf, el)
    v2 = jnp.max(el2, axis=-1, keepdims=True)
    i2 = jnp.min(jnp.where(el2 == v2, lanef, float(LANES)), axis=-1, keepdims=True)
    tt = jnp.exp(v2 - v1)
    w1 = 1.0 / (1.0 + tt)
    w2 = tt * w1
    rt = jnp.where(lane == 0, i1 - N_GROUPS,
                   jnp.where(lane == 1, i2 - N_GROUPS,
                             jnp.where(lane == 2, gate * w1,
                                       jnp.where(lane == 3, gate * w2, 0.0))))
    rt_ref[...] = rt


def _post(x2, ya, obs, lses, gates, mod3, boff, seq, n2, wa, wb, wo, wr, br):
    t, d = x2.shape
    tm = POST_TM
    assert t % tm == 0 and seq % tm == 0
    spb = seq // tm
    row = lambda i: (i, 0)
    const = lambda i: (0, 0)
    return pl.pallas_call(
        _post_kernel,
        grid=(t // tm,),
        in_specs=[
            pl.BlockSpec((tm, d), row),
            pl.BlockSpec((tm, A_Q), row),
            pl.BlockSpec((tm, B_OUT), row), pl.BlockSpec((tm, B_OUT), row), pl.BlockSpec((tm, B_OUT), row),
            pl.BlockSpec((tm, LANES), row), pl.BlockSpec((tm, LANES), row), pl.BlockSpec((tm, LANES), row),
            pl.BlockSpec((tm, d), lambda i: (i, 0)),
            pl.BlockSpec((tm, d), lambda i: (i, 1)),
            pl.BlockSpec((None, 6, d), lambda i: (boff + i // spb, 0, 0)),
            pl.BlockSpec((1, d), const),
            pl.BlockSpec((A_Q, d), const, pipeline_mode=pl.Buffered(1)),
            pl.BlockSpec((B_OUT, d), const, pipeline_mode=pl.Buffered(1)),
            pl.BlockSpec((d, d), const, pipeline_mode=pl.Buffered(1)),
            pl.BlockSpec((d, LANES), const, pipeline_mode=pl.Buffered(1)),
            pl.BlockSpec((1, LANES), const),
        ],
        out_specs=[pl.BlockSpec((tm, d), row), pl.BlockSpec((tm, d), row), pl.BlockSpec((tm, LANES), row)],
        out_shape=[jax.ShapeDtypeStruct((t, d), F32), jax.ShapeDtypeStruct((t, d), F32),
                   jax.ShapeDtypeStruct((t, LANES), F32)],
        compiler_params=_params(("arbitrary",)),
        name="post_attn",
    )(x2, ya, obs[0], obs[1], obs[2], lses[0], lses[1], lses[2], gates, gates, mod3, n2.reshape(1, d),
      wa, wb, wo, wr, br)


def _slot_plan(experts):
    a = experts.shape[0]
    c = RANK_CHUNK
    assert a % c == 0 and a % MOE_TM == 0
    oh = (experts[:, None] == jnp.arange(N_EXPERTS, dtype=jnp.int32)[None, :])
    oh3 = oh.astype(jnp.bfloat16).reshape(a // c, c, N_EXPERTS)
    tri = (np.arange(c)[:, None] > np.arange(c)[None, :])
    within = jnp.einsum('ij,cjk->cik', jnp.asarray(tri, jnp.bfloat16), oh3, preferred_element_type=F32)
    tot = jnp.sum(oh3.astype(F32), axis=1)
    off = jnp.cumsum(tot, axis=0) - tot
    counts = jnp.sum(tot, axis=0).astype(jnp.int32)
    padded = (counts + MOE_TM - 1) // MOE_TM * MOE_TM
    pend = jnp.cumsum(padded)
    pstart = pend - padded
    slot = within + off[:, None, :] + pstart.astype(F32)[None, None, :]
    dest = jnp.sum(jnp.where(oh.reshape(a // c, c, N_EXPERTS), slot, 0.0), axis=-1).reshape(a).astype(jnp.int32)
    n_blocks = a // MOE_TM + N_EXPERTS
    block_e = jnp.minimum(
        jnp.searchsorted(pend, jnp.arange(n_blocks, dtype=jnp.int32) * MOE_TM, side='right'),
        N_EXPERTS - 1).astype(jnp.int32)
    nb_used = (pend[-1] // MOE_TM).astype(jnp.int32).reshape(1)
    return dest, block_e, nb_used, n_blocks


def _dispatch_kernel(dest_ref, h_ref, xs_in_ref, xs_ref, sem):
    del xs_in_ref
    tm = h_ref.shape[0]

    def row_copy(t, k):
        d = dest_ref[0, TOP_K * t + k]
        return pltpu.make_async_copy(h_ref.at[pl.ds(t, 1)], xs_ref.at[pl.ds(d, 1)], sem)

    def start(t, carry):
        for k in range(TOP_K):
            row_copy(t, k).start()
        return carry

    def wait(t, carry):
        for k in range(TOP_K):
            row_copy(t, k).wait()
        return carry

    lax.fori_loop(0, tm, start, 0)
    lax.fori_loop(0, tm, wait, 0)


def _dispatch(dest_blocks, h2, xs):
    t, d = h2.shape
    tm = DISP_TM
    return pl.pallas_call(
        _dispatch_kernel,
        grid=(t // tm,),
        in_specs=[
            pl.BlockSpec((None, 1, TOP_K * tm), lambda i: (i, 0, 0), memory_space=pltpu.SMEM),
            pl.BlockSpec((tm, d), lambda i: (i, 0)),
            pl.BlockSpec(memory_space=pl.ANY),
        ],
        out_specs=pl.BlockSpec(memory_space=pl.ANY),
        out_shape=jax.ShapeDtypeStruct(xs.shape, xs.dtype),
        scratch_shapes=[pltpu.SemaphoreType.DMA(())],
        input_output_aliases={2: 0},
        compiler_params=_params(("arbitrary",)),
        name="moe_dispatch",
    )(dest_blocks, h2, xs)


def _expert_kernel(be_ref, nb_ref, x_ref, wg_ref, wu_ref, wd_ref, y_ref):
    del be_ref
    used = pl.program_id(0) < nb_ref[0]

    @pl.when(used)
    def _():
        x = x_ref[...].astype(MXU_DTYPE)
        g = jnp.dot(x, wg_ref[...], preferred_element_type=F32)
        u = jnp.dot(x, wu_ref[...], preferred_element_type=F32)
        h = (g * jax.nn.sigmoid(g)) * u
        y_ref[...] = jnp.dot(h.astype(MXU_DTYPE), wd_ref[...], preferred_element_type=F32)

    @pl.when(jnp.logical_not(used))
    def _():
        y_ref[...] = jnp.zeros_like(y_ref)


def _experts(xs, block_e, nb_used, n_blocks, wg, wu, wd):
    n_slots, d = xs.shape
    ff = wg.shape[-1]
    tm = MOE_TM
    rows = lambda i, be, nb: (jnp.minimum(i, nb[0] - 1), 0)
    grid_spec = pltpu.PrefetchScalarGridSpec(
        num_scalar_prefetch=2,
        grid=(n_blocks,),
        in_specs=[
            pl.BlockSpec((tm, d), rows),
            pl.BlockSpec((None, d, ff), lambda i, be, nb: (be[i], 0, 0)),
            pl.BlockSpec((None, d, ff), lambda i, be, nb: (be[i], 0, 0)),
            pl.BlockSpec((None, ff, d), lambda i, be, nb: (be[i], 0, 0)),
        ],
        out_specs=pl.BlockSpec((tm, d), lambda i, be, nb: (i, 0)),
    )
    return pl.pallas_call(
        _expert_kernel,
        grid_spec=grid_spec,
        out_shape=jax.ShapeDtypeStruct((n_slots, d), F32),
        compiler_params=_params(("arbitrary",)),
        name="moe_experts",
    )(block_e, nb_used, xs, wg, wu, wd)


def _combine_kernel(dest_ref, x1_ref, rt_ref, mod_ref, nf_ref, ys_ref, o_ref, ybuf, sem, *, final):
    tm = x1_ref.shape[0]

    def row_copy(t, k):
        d = dest_ref[0, TOP_K * t + k]
        return pltpu.make_async_copy(ys_ref.at[pl.ds(d, 1)], ybuf.at[pl.ds(k * tm + t, 1)], sem)

    def start(t, carry):
        for k in range(TOP_K):
            row_copy(t, k).start()
        return carry

    def wait(t, carry):
        for k in range(TOP_K):
            row_copy(t, k).wait()
        return carry

    lax.fori_loop(0, tm, start, 0)
    lax.fori_loop(0, tm, wait, 0)
    rt = rt_ref[...]
    moe = ybuf[0:tm, :] * rt[:, 2:3] + ybuf[tm:2 * tm, :] * rt[:, 3:4]
    x2 = x1_ref[...] + mod_ref[5:6, :] * moe
    if final:
        x2 = _rms(x2) * nf_ref[...]
    o_ref[...] = x2


def _combine(dest_blocks, x1, rt, mod3, boff, seq, nf, ys, final):
    t, d = x1.shape
    tm = COMB_TM
    spb = seq // tm
    kern = functools.partial(_combine_kernel, final=final)
    return pl.pallas_call(
        kern,
        grid=(t // tm,),
        in_specs=[
            pl.BlockSpec((None, 1, TOP_K * tm), lambda i: (i, 0, 0), memory_space=pltpu.SMEM),
            pl.BlockSpec((tm, d), lambda i: (i, 0)),
            pl.BlockSpec((tm, LANES), lambda i: (i, 0)),
            pl.BlockSpec((None, 6, d), lambda i: (boff + i // spb, 0, 0)),
            pl.BlockSpec((1, d), lambda i: (0, 0)),
            pl.BlockSpec(memory_space=pl.ANY),
        ],
        out_specs=pl.BlockSpec((tm, d), lambda i: (i, 0)),
        out_shape=jax.ShapeDtypeStruct((t, d), F32),
        scratch_shapes=[pltpu.VMEM((TOP_K * tm, d), F32), pltpu.SemaphoreType.DMA(())],
        compiler_params=_params(("arbitrary",)),
        name="moe_combine",
    )(dest_blocks, x1, rt, mod3, nf.reshape(1, d), ys)


def kernel(x_prompt, x_sample, c_prompt, c_sample, w_ada, b_ada, norm1, w_in, q_norm, k_norm, w_branch_a, w_branch_b, w_out, rel_bias, norm2, w_route_group, b_route_group, w_route_expert, b_route_expert, w_gate_exp, w_up_exp, w_down_exp, norm_final):
    depth = w_ada.shape[0]
    d = x_prompt.shape[-1]
    trunks = []
    boff = 0
    for x, c in ((x_prompt, c_prompt), (x_sample, c_sample)):
        b, s, _ = x.shape
        trunks.append(dict(x=x.reshape(b * s, d), batch=b, seq=s, boff=boff))
        boff += b
    n_cond = boff
    rows = -(-n_cond // 8) * 8
    c_all = jnp.concatenate([c_prompt, c_sample, jnp.zeros((rows - n_cond, d), F32)], axis=0)
    scale = HEAD_DIM ** -0.5

    for l in range(depth):
        mod3 = _adaln(c_all, w_ada[l], b_ada[l]).reshape(rows, 6, d)
        w_in_l = w_in[l].astype(MXU_DTYPE)
        wa = w_branch_a[l].astype(MXU_DTYPE)
        wb = w_branch_b[l].astype(MXU_DTYPE)
        wo = w_out[l].astype(MXU_DTYPE)
        wr = jnp.concatenate([w_route_group[l], w_route_expert[l],
                              jnp.zeros((d, LANES - N_GROUPS - N_EXPERTS), F32)], axis=1)
        br = jnp.concatenate([b_route_group[l], b_route_expert[l],
                              jnp.zeros((LANES - N_GROUPS - N_EXPERTS,), F32)]).reshape(1, LANES)
        wg = w_gate_exp[l].astype(MXU_DTYPE)
        wu = w_up_exp[l].astype(MXU_DTYPE)
        wd = w_down_exp[l].astype(MXU_DTYPE)

        for tr in trunks:
            qa, kva, qkvb, gates = _inproj(tr['x'], mod3, tr['boff'], tr['seq'], norm1[l], w_in_l,
                                           q_norm[l] * scale, k_norm[l])
            ya = _attn_a(qa, kva, tr['batch'], tr['seq'])
            obs, lses = [], []
            for gi, (window, dilation) in enumerate(B_PAIRS):
                hs = slice(gi * B_HEADS_PER_GROUP, (gi + 1) * B_HEADS_PER_GROUP)
                o, lse = _dilated_group(qkvb, rel_bias[:, hs], gi, window, dilation, tr['batch'], tr['seq'])
                obs.append(o)
                lses.append(lse)
            tr['x1'], tr['h2'], tr['rt'] = _post(tr['x'], ya, obs, lses, gates, mod3, tr['boff'], tr['seq'],
                                                 norm2[l], wa, wb, wo, wr, br)

        experts = jnp.concatenate([tr['rt'][:, :TOP_K] for tr in trunks], axis=0).astype(jnp.int32).reshape(-1)
        dest, block_e, nb_used, n_blocks = _slot_plan(experts)
        xs = jnp.zeros((n_blocks * MOE_TM, d), F32)
        a0 = 0
        for tr in trunks:
            n_tok = tr['x'].shape[0]
            tr['dest'] = dest[a0:a0 + TOP_K * n_tok]
            a0 += TOP_K * n_tok
            xs = _dispatch(tr['dest'].reshape(n_tok // DISP_TM, 1, TOP_K * DISP_TM), tr['h2'], xs)
        ys = _experts(xs, block_e, nb_used, n_blocks, wg, wu, wd)
        for tr in trunks:
            n_tok = tr['x'].shape[0]
            tr['x'] = _combine(tr['dest'].reshape(n_tok // COMB_TM, 1, TOP_K * COMB_TM), tr['x1'], tr['rt'],
                               mod3, tr['boff'], tr['seq'], norm_final, ys, final=(l == depth - 1))

    return tuple(tr['x'].reshape(tr['batch'], tr['seq'], d) for tr in trunks)
```

```python
import functools
import math

import numpy as np
import jax
import jax.numpy as jnp
from jax import lax
from jax.experimental import pallas as pl
from jax.experimental.pallas import tpu as pltpu

F32 = jnp.float32
MXU_DTYPE = jnp.bfloat16

HEAD_DIM = 128
A_HEADS = 8
A_KV_HEADS = 2
A_GROUP = A_HEADS // A_KV_HEADS
ROPE_THETA = 10000.0
GRID_W = 64
B_PAIRS = ((128, 1), (512, 4), (2048, 16))
B_GROUPS = len(B_PAIRS)
B_HEADS_PER_GROUP = 4
B_HEADS = B_GROUPS * B_HEADS_PER_GROUP
REL_BUCKETS = 32
REL_MAX_DIST = 1024
A_Q = A_HEADS * HEAD_DIM
A_KV = A_KV_HEADS * HEAD_DIM
B_W = B_HEADS * HEAD_DIM
B_OUT = B_HEADS_PER_GROUP * HEAD_DIM
N_GROUPS = 8
EXPERTS_PER_GROUP = 8
N_EXPERTS = N_GROUPS * EXPERTS_PER_GROUP
TOP_K = 2
EPS = 1e-6
NEG = -1e30
LANES = 128

VMEM_LIMIT_BYTES = 56 * 1024 * 1024

PROJ_TM = 512
PROJ_TN = 512
ATT_TQ = 256
DIL_TQ = 128
DIL_SIDE = 64
POST_TM = 256
MOE_TM = 512
MOE_SUB = 256
DISP_TM = 256
COMB_TM = 128
RANK_CHUNK = 512


def _params(sem):
    return pltpu.CompilerParams(dimension_semantics=sem, vmem_limit_bytes=VMEM_LIMIT_BYTES)


def _ada_kernel(c_ref, w_ref, b_ref, o_ref):
    c = c_ref[...]
    a = (c * jax.nn.sigmoid(c)).astype(MXU_DTYPE)
    o_ref[...] = jnp.dot(a, w_ref[...].astype(MXU_DTYPE), preferred_element_type=F32) + b_ref[...]


def _adaln(c_all, w, b):
    r, d = c_all.shape
    n = w.shape[1]
    tn = min(n, 1024)
    return pl.pallas_call(
        _ada_kernel,
        grid=(n // tn,),
        in_specs=[pl.BlockSpec((r, d), lambda j: (0, 0)),
                  pl.BlockSpec((d, tn), lambda j: (0, j)),
                  pl.BlockSpec((1, tn), lambda j: (0, j))],
        out_specs=pl.BlockSpec((r, tn), lambda j: (0, j)),
        out_shape=jax.ShapeDtypeStruct((r, n), F32),
        compiler_params=_params(("arbitrary",)),
        name="adaln",
    )(c_all, w, b.reshape(1, n))


def _rms(x):
    return x * lax.rsqrt(jnp.mean(x * x, axis=-1, keepdims=True) + EPS)


def _norm_rope(x, g, cos, sin):
    y = _rms(x) * g
    lane = lax.broadcasted_iota(jnp.int32, y.shape, 1)
    r32 = pltpu.roll(y, 32, 1)
    r96 = pltpu.roll(y, 96, 1)
    partner = jnp.where(pltpu.roll(lane, 32, 1) == (lane ^ 32), r32, r96)
    return y * cos + partner * sin


def _inproj_kernel(x_ref, mod_ref, n1_ref, w_ref, cos_ref, sin_ref, qn_ref, kn_ref,
                   qa_ref, kva_ref, qkv1_ref, qkv2_ref, qkv3_ref, g_ref, h_scr, de_scr, *, j_kv, j_b, j_g):
    j = pl.program_id(1)
    tm = x_ref.shape[0]

    @pl.when(j == 0)
    def _():
        h = _rms(x_ref[...]) * n1_ref[...]
        h = h * (1.0 + mod_ref[1:2, :]) + mod_ref[0:1, :]
        h_scr[...] = h.astype(h_scr.dtype)

    acc = jnp.dot(h_scr[...], w_ref[...], preferred_element_type=F32)
    heads = PROJ_TN // HEAD_DIM

    def head(a, hh):
        return a[:, hh * HEAD_DIM:(hh + 1) * HEAD_DIM]

    @pl.when(j < j_kv)
    def _():
        for hh in range(heads):
            qa_ref[:, hh * HEAD_DIM:(hh + 1) * HEAD_DIM] = _norm_rope(
                head(acc, hh), qn_ref[...], cos_ref[...], sin_ref[...]).astype(qa_ref.dtype)

    @pl.when(j == j_kv)
    def _():
        for hh in range(heads):
            if hh < A_KV_HEADS:
                val = _norm_rope(head(acc, hh), kn_ref[...], cos_ref[...], sin_ref[...])
            else:
                val = head(acc, hh)
            kva_ref[:, hh * HEAD_DIM:(hh + 1) * HEAD_DIM] = val.astype(kva_ref.dtype)

    for gi, ((_, dil), ref) in enumerate(zip(B_PAIRS, (qkv1_ref, qkv2_ref, qkv3_ref))):
        j0 = j_b + 3 * gi

        @pl.when(jnp.logical_and(j >= j0, j < j0 + 3))
        def _(dil=dil, ref=ref, j0=j0):
            val = acc * jnp.where(j == j0, HEAD_DIM ** -0.5, 1.0).astype(F32)
            if dil == 1:
                ref[0] = val.astype(ref.dtype)
            else:
                for hh in range(heads):
                    de_scr[hh] = head(val, hh)
                for r in range(dil):
                    for hh in range(heads):
                        ref[r, :, hh * HEAD_DIM:(hh + 1) * HEAD_DIM] = de_scr[
                            hh, pl.ds(r, tm // dil, stride=dil), :].astype(ref.dtype)

    @pl.when(j >= j_g)
    def _():
        g_ref[...] = jax.nn.sigmoid(acc).astype(g_ref.dtype)


def _rope_tables(s):
    half = HEAD_DIM // 4
    inv = np.power(ROPE_THETA, -np.arange(half, dtype=np.float64) / half)
    t = np.arange(s)
    row = (t // GRID_W).astype(np.float64)
    col = (t % GRID_W).astype(np.float64)
    ang_r = row[:, None] * inv[None, :]
    ang_c = col[:, None] * inv[None, :]
    cos = np.concatenate([np.cos(ang_r), np.cos(ang_r), np.cos(ang_c), np.cos(ang_c)], axis=1)
    sin = np.concatenate([-np.sin(ang_r), np.sin(ang_r), -np.sin(ang_c), np.sin(ang_c)], axis=1)
    return jnp.asarray(cos, F32), jnp.asarray(sin, F32)


def _inproj_weight(w_in_l):
    b0 = A_Q + 2 * A_KV
    parts = [w_in_l[:, :b0]]
    for gi in range(B_GROUPS):
        for part in range(3):
            c0 = b0 + part * B_W + gi * B_OUT
            parts.append(w_in_l[:, c0:c0 + B_OUT])
    parts.append(w_in_l[:, b0 + 3 * B_W:])
    return jnp.concatenate(parts, axis=1)


def _inproj(x2, mod3, boff, batch, seq, n1, w_in, qn, kn):
    t, d = x2.shape
    tm, tn = PROJ_TM, PROJ_TN
    assert t % tm == 0 and seq % tm == 0
    assert A_Q % tn == 0 and 2 * A_KV == tn and B_OUT == tn and d % tn == 0
    j_kv = A_Q // tn
    j_b = j_kv + 1
    j_g = j_b + 3 * B_GROUPS
    n_j = j_g + 2 * d // tn
    assert w_in.shape == (d, n_j * tn)
    cos, sin = _rope_tables(seq)
    spb = seq // tm
    kern = functools.partial(_inproj_kernel, j_kv=j_kv, j_b=j_b, j_g=j_g)
    grp_specs, grp_shapes = [], []
    for gi, (_, dil) in enumerate(B_PAIRS):
        assert tm % dil == 0 and (tm // dil) % 16 == 0
        j0 = j_b + 3 * gi
        grp_specs.append(pl.BlockSpec((None, dil, tm // dil, tn),
                                      lambda i, j, j0=j0: (i // spb, 0, i % spb, jnp.clip(j - j0, 0, 2))))
        grp_shapes.append(jax.ShapeDtypeStruct((batch, dil, seq // dil, 3 * tn), MXU_DTYPE))
    return pl.pallas_call(
        kern,
        grid=(t // tm, n_j),
        in_specs=[
            pl.BlockSpec((tm, d), lambda i, j: (i, 0)),
            pl.BlockSpec((None, 6, d), lambda i, j: (boff + i // spb, 0, 0)),
            pl.BlockSpec((1, d), lambda i, j: (0, 0)),
            pl.BlockSpec((d, tn), lambda i, j: (0, j)),
            pl.BlockSpec((tm, HEAD_DIM), lambda i, j: (i % spb, 0)),
            pl.BlockSpec((tm, HEAD_DIM), lambda i, j: (i % spb, 0)),
            pl.BlockSpec((1, HEAD_DIM), lambda i, j: (0, 0)),
            pl.BlockSpec((1, HEAD_DIM), lambda i, j: (0, 0)),
        ],
        out_specs=[
            pl.BlockSpec((tm, tn), lambda i, j: (i, jnp.clip(j, 0, j_kv - 1))),
            pl.BlockSpec((tm, tn), lambda i, j: (i, 0)),
            *grp_specs,
            pl.BlockSpec((tm, tn), lambda i, j: (i, jnp.clip(j - j_g, 0, n_j - j_g - 1))),
        ],
        out_shape=[
            jax.ShapeDtypeStruct((t, A_Q), MXU_DTYPE),
            jax.ShapeDtypeStruct((t, 2 * A_KV), MXU_DTYPE),
            *grp_shapes,
            jax.ShapeDtypeStruct((t, 2 * d), MXU_DTYPE),
        ],
        scratch_shapes=[pltpu.VMEM((tm, d), MXU_DTYPE), pltpu.VMEM((tn // HEAD_DIM, tm, HEAD_DIM), F32)],
        compiler_params=_params(("arbitrary", "arbitrary")),
        name="inproj",
    )(x2, mod3, n1.reshape(1, d), w_in, cos, sin, qn.reshape(1, HEAD_DIM), kn.reshape(1, HEAD_DIM))


def _attn_a_kernel(q_ref, k_ref, v_ref, o_ref):
    k = k_ref[...]
    v = v_ref[...]
    for hh in range(A_GROUP):
        sl = slice(hh * HEAD_DIM, (hh + 1) * HEAD_DIM)
        s = lax.dot_general(q_ref[:, sl], k, (((1,), (1,)), ((), ())), preferred_element_type=F32)
        m = jnp.max(s, axis=-1, keepdims=True)
        p = jnp.exp(s - m)
        l = jnp.sum(p, axis=-1, keepdims=True)
        o = jnp.dot(p.astype(v.dtype), v, preferred_element_type=F32)
        o_ref[:, sl] = (o / l).astype(o_ref.dtype)


def _attn_a(qa, kva, batch, seq):
    t = qa.shape[0]
    tq = min(ATT_TQ, seq)
    nq = seq // tq
    gw = A_GROUP * HEAD_DIM
    return pl.pallas_call(
        _attn_a_kernel,
        grid=(batch, A_KV_HEADS, nq),
        in_specs=[
            pl.BlockSpec((tq, gw), lambda b, kh, qi: (b * nq + qi, kh)),
            pl.BlockSpec((seq, HEAD_DIM), lambda b, kh, qi: (b, kh)),
            pl.BlockSpec((seq, HEAD_DIM), lambda b, kh, qi: (b, A_KV_HEADS + kh)),
        ],
        out_specs=pl.BlockSpec((tq, gw), lambda b, kh, qi: (b * nq + qi, kh)),
        out_shape=jax.ShapeDtypeStruct((t, A_Q), MXU_DTYPE),
        compiler_params=_params(("arbitrary", "arbitrary", "arbitrary")),
        name="attn_dense",
    )(qa, kva, kva)


def _t5_bucket(rel):
    nb = REL_BUCKETS // 2
    ret = (rel > 0).astype(np.int32) * nb
    n = np.abs(rel)
    max_exact = nb // 2
    large = max_exact + (np.log(np.maximum(n, 1) / max_exact) / np.log(REL_MAX_DIST / max_exact)
                         * (nb - max_exact)).astype(np.int32)
    large = np.minimum(large, nb - 1)
    return (ret + np.where(n < max_exact, n, large)).astype(np.int32)


def _band_bias(table_g, dilation, tq, w):
    h = table_g.shape[1]
    offs = np.arange(-DIL_SIDE, DIL_SIDE + 1)
    diag = jnp.take(table_g, jnp.asarray(_t5_bucket(offs * dilation)), axis=0).astype(F32).T
    diag = jnp.concatenate([diag, jnp.full((h, 1), NEG, F32)], axis=1)
    n = tq + w
    u = np.arange(n)
    delta = np.where(u <= w - 1, u, u - n)
    out = []
    for var in range(3):
        m = delta - DIL_SIDE * var
        idx = np.where(np.abs(m) <= DIL_SIDE, m + DIL_SIDE, 2 * DIL_SIDE + 1)
        v = jnp.take(diag, jnp.asarray(idx), axis=1)
        toep = jnp.tile(v, (1, tq))[:, :tq * (n - 1)].reshape(h, tq, n - 1)[:, :, :w]
        out.append(toep)
    return jnp.stack(out, axis=0)


def _dil_kernel(q_ref, k_ref, v_ref, bias_ref, o_ref, lse_ref, *, length, tq, w):
    nblk = length // tq
    lane = lax.broadcasted_iota(jnp.int32, (tq, LANES), 1)

    def body(c, carry):
        q0 = pl.multiple_of(c * tq, tq)
        ws = pl.multiple_of(jnp.clip(q0 - DIL_SIDE, 0, length - w), DIL_SIDE)
        var = (q0 - ws) // DIL_SIDE
        lse_blk = jnp.zeros((tq, LANES), F32)
        for hh in range(B_HEADS_PER_GROUP):
            sl = slice(hh * HEAD_DIM, (hh + 1) * HEAD_DIM)
            q = q_ref[pl.ds(q0, tq), sl]
            kw = k_ref[pl.ds(ws, w), sl]
            vw = v_ref[pl.ds(ws, w), sl]
            s = lax.dot_general(q, kw, (((1,), (1,)), ((), ())), preferred_element_type=F32)
            s = s + bias_ref[var, hh]
            m = jnp.max(s, axis=-1, keepdims=True)
            p = jnp.exp(s - m)
            l = jnp.sum(p, axis=-1, keepdims=True)
            o = jnp.dot(p.astype(vw.dtype), vw, preferred_element_type=F32)
            o_ref[pl.ds(q0, tq), sl] = (o / l).astype(o_ref.dtype)
            lse_blk = jnp.where(lane == hh, m + jnp.log(l), lse_blk)
        lse_ref[pl.ds(q0, tq), :] = lse_blk
        return carry

    lax.fori_loop(0, nblk, body, 0)


def _dilated_group(qkv, table_g, window, dilation):
    assert (window // 2) // dilation == DIL_SIDE
    batch, _, length, _ = qkv.shape
    assert length % DIL_SIDE == 0
    tq = min(DIL_TQ, length)
    w = min(length, tq + 2 * DIL_SIDE)
    assert length % tq == 0
    gw = B_HEADS_PER_GROUP * HEAD_DIM
    bias = _band_bias(table_g, dilation, tq, w)
    kern = functools.partial(_dil_kernel, length=length, tq=tq, w=w)
    return pl.pallas_call(
        kern,
        grid=(batch, dilation),
        in_specs=[
            pl.BlockSpec((None, None, length, gw), lambda b, r: (b, r, 0, 0)),
            pl.BlockSpec((None, None, length, gw), lambda b, r: (b, r, 0, 1)),
            pl.BlockSpec((None, None, length, gw), lambda b, r: (b, r, 0, 2)),
            pl.BlockSpec((3, B_HEADS_PER_GROUP, tq, w), lambda b, r: (0, 0, 0, 0)),
        ],
        out_specs=[
            pl.BlockSpec((None, None, length, gw), lambda b, r: (b, r, 0, 0)),
            pl.BlockSpec((None, None, length, LANES), lambda b, r: (b, r, 0, 0)),
        ],
        out_shape=[
            jax.ShapeDtypeStruct((batch, dilation, length, gw), MXU_DTYPE),
            jax.ShapeDtypeStruct((batch, dilation, length, LANES), F32),
        ],
        compiler_params=_params(("arbitrary", "arbitrary")),
        name=f"attn_dilated_{dilation}",
    )(qkv, qkv, qkv, bias)


def _post_kernel(x_ref, ya_ref, o1_ref, o2_ref, o3_ref, l1_ref, l2_ref, l3_ref, ga_ref, gb_ref,
                 mod_ref, n2_ref, wa_ref, wb_ref, wo_ref, wr_ref, br_ref,
                 x1_ref, h2_ref, rt_ref, o_scr, l_scr):
    tm = x_ref.shape[0]
    lses, outs = [], []
    for gi, ((_, dil), o_ref, l_ref) in enumerate(zip(B_PAIRS, (o1_ref, o2_ref, o3_ref), (l1_ref, l2_ref, l3_ref))):
        if dil == 1:
            lses.append(l_ref[0])
            outs.append([o_ref[0, :, jh * HEAD_DIM:(jh + 1) * HEAD_DIM].astype(F32)
                         for jh in range(B_HEADS_PER_GROUP)])
            continue
        for r in range(dil):
            rows = pl.ds(r, tm // dil, stride=dil)
            l_scr[gi, rows, :] = l_ref[r]
            for jh in range(B_HEADS_PER_GROUP):
                o_scr[gi, jh, rows, :] = o_ref[r, :, jh * HEAD_DIM:(jh + 1) * HEAD_DIM].astype(F32)
        lses.append(l_scr[gi])
        outs.append([o_scr[gi, jh] for jh in range(B_HEADS_PER_GROUP)])
    parts = []
    for jh in range(B_HEADS_PER_GROUP):
        lj = [lg[:, jh:jh + 1] for lg in lses]
        mx = jnp.maximum(jnp.maximum(lj[0], lj[1]), lj[2])
        ex = [jnp.exp(v - mx) for v in lj]
        inv = 1.0 / (ex[0] + ex[1] + ex[2])
        acc = (ex[0] * inv) * outs[0][jh]
        acc = acc + (ex[1] * inv) * outs[1][jh]
        acc = acc + (ex[2] * inv) * outs[2][jh]
        parts.append(acc.astype(MXU_DTYPE))
    yb = jnp.concatenate(parts, axis=1)
    a = jnp.dot(ya_ref[...], wa_ref[...], preferred_element_type=F32)
    bb = jnp.dot(yb, wb_ref[...], preferred_element_type=F32)
    merged = ga_ref[...].astype(F32) * a + gb_ref[...].astype(F32) * bb
    out = jnp.dot(merged.astype(MXU_DTYPE), wo_ref[...], preferred_element_type=F32)
    x1 = x_ref[...] + mod_ref[2:3, :] * out
    x1_ref[...] = x1
    h2 = _rms(x1) * n2_ref[...]
    h2 = h2 * (1.0 + mod_ref[4:5, :]) + mod_ref[3:4, :]
    h2_ref[...] = h2

    h_hi = h2.astype(MXU_DTYPE)
    h_lo = (h2 - h_hi.astype(F32)).astype(MXU_DTYPE)
    r_hi = jnp.dot(h_hi, wr_ref[...], preferred_element_type=F32)
    r_lo = jnp.dot(h_lo, wr_ref[:, :LANES], preferred_element_type=F32)
    r = r_hi[:, :LANES] + r_hi[:, LANES:] + r_lo + br_ref[...]
    lane = lax.broadcasted_iota(jnp.int32, r.shape, 1)
    lanef = lane.astype(F32)
    ninf = -jnp.inf
    gmask = lane < N_GROUPS
    gl = jnp.where(gmask, r, ninf)
    gmax = jnp.max(gl, axis=-1, keepdims=True)
    grp = jnp.min(jnp.where(gl == gmax, lanef, float(LANES)), axis=-1, keepdims=True)
    gsum = jnp.sum(jnp.where(gmask, jnp.exp(jnp.where(gmask, r, gmax) - gmax), 0.0), axis=-1, keepdims=True)
    gate = 1.0 / gsum
    lo = N_GROUPS + grp * EXPERTS_PER_GROUP
    emask = jnp.logical_and(lanef >= lo, lanef < lo + EXPERTS_PER_GROUP)
    el = jnp.where(emask, r, ninf)
    v1 = jnp.max(el, axis=-1, keepdims=True)
    i1 = jnp.min(jnp.where(el == v1, lanef, float(LANES)), axis=-1, keepdims=True)
    el2 = jnp.where(lanef == i1, ninf, el)
    v2 = jnp.max(el2, axis=-1, keepdims=True)
    i2 = jnp.min(jnp.where(el2 == v2, lanef, float(LANES)), axis=-1, keepdims=True)
    tt = jnp.exp(v2 - v1)
    w1 = 1.0 / (1.0 + tt)
    w2 = tt * w1
    rt = jnp.where(lane == 0, i1 - N_GROUPS,
                   jnp.where(lane == 1, i2 - N_GROUPS,
                             jnp.where(lane == 2, gate * w1,
                                       jnp.where(lane == 3, gate * w2, 0.0))))
    rt_ref[...] = rt


def _post(x2, ya, obs, lses, gates, mod3, boff, seq, n2, wa, wb, wo, wr, br):
    t, d = x2.shape
    tm = POST_TM
    assert t % tm == 0 and seq % tm == 0
    spb = seq // tm
    row = lambda i: (i, 0)
    const = lambda i: (0, 0)
    res = lambda i: (i // spb, 0, i % spb, 0)
    o_specs, l_specs = [], []
    for _, dil in B_PAIRS:
        assert tm % dil == 0 and (tm // dil) % 16 == 0
        o_specs.append(pl.BlockSpec((None, dil, tm // dil, B_OUT), res))
        l_specs.append(pl.BlockSpec((None, dil, tm // dil, LANES), res))
    return pl.pallas_call(
        _post_kernel,
        grid=(t // tm,),
        in_specs=[
            pl.BlockSpec((tm, d), row),
            pl.BlockSpec((tm, A_Q), row),
            *o_specs,
            *l_specs,
            pl.BlockSpec((tm, d), lambda i: (i, 0)),
            pl.BlockSpec((tm, d), lambda i: (i, 1)),
            pl.BlockSpec((None, 6, d), lambda i: (boff + i // spb, 0, 0)),
            pl.BlockSpec((1, d), const),
            pl.BlockSpec((A_Q, d), const, pipeline_mode=pl.Buffered(1)),
            pl.BlockSpec((B_OUT, d), const, pipeline_mode=pl.Buffered(1)),
            pl.BlockSpec((d, d), const, pipeline_mode=pl.Buffered(1)),
            pl.BlockSpec((d, 2 * LANES), const, pipeline_mode=pl.Buffered(1)),
            pl.BlockSpec((1, LANES), const),
        ],
        out_specs=[pl.BlockSpec((tm, d), row), pl.BlockSpec((tm, d), row), pl.BlockSpec((tm, LANES), row)],
        out_shape=[jax.ShapeDtypeStruct((t, d), F32), jax.ShapeDtypeStruct((t, d), F32),
                   jax.ShapeDtypeStruct((t, LANES), F32)],
        scratch_shapes=[pltpu.VMEM((B_GROUPS, B_HEADS_PER_GROUP, tm, HEAD_DIM), F32),
                        pltpu.VMEM((B_GROUPS, tm, LANES), F32)],
        compiler_params=_params(("arbitrary",)),
        name="post_attn",
    )(x2, ya, obs[0], obs[1], obs[2], lses[0], lses[1], lses[2], gates, gates, mod3, n2.reshape(1, d),
      wa, wb, wo, wr, br)


def _slot_plan(experts):
    a = experts.shape[0]
    c = RANK_CHUNK
    assert a % c == 0 and a % MOE_TM == 0
    oh = (experts[:, None] == jnp.arange(N_EXPERTS, dtype=jnp.int32)[None, :])
    oh3 = oh.astype(jnp.bfloat16).reshape(a // c, c, N_EXPERTS)
    tri = (np.arange(c)[:, None] > np.arange(c)[None, :])
    within = jnp.einsum('ij,cjk->cik', jnp.asarray(tri, jnp.bfloat16), oh3, preferred_element_type=F32)
    tot = jnp.sum(oh3.astype(F32), axis=1)
    off = jnp.cumsum(tot, axis=0) - tot
    counts = jnp.sum(tot, axis=0).astype(jnp.int32)
    padded = (counts + MOE_TM - 1) // MOE_TM * MOE_TM
    pend = jnp.cumsum(padded)
    pstart = pend - padded
    slot = within + off[:, None, :] + pstart.astype(F32)[None, None, :]
    dest = jnp.sum(jnp.where(oh.reshape(a // c, c, N_EXPERTS), slot, 0.0), axis=-1).reshape(a).astype(jnp.int32)
    n_blocks = a // MOE_TM + N_EXPERTS
    idx = jnp.arange(n_blocks, dtype=jnp.int32)
    block_start = idx * MOE_TM
    block_e = jnp.minimum(jnp.sum((pend[None, :] <= block_start[:, None]).astype(jnp.int32), axis=1),
                          N_EXPERTS - 1)
    nb_used = pend[-1] // MOE_TM
    used = idx < nb_used
    last = nb_used - 1
    onehot_e = block_e[:, None] == jnp.arange(N_EXPERTS, dtype=jnp.int32)[None, :]
    seg_end = jnp.sum(jnp.where(onehot_e, (pstart + counts)[None, :], 0), axis=1)
    nv = jnp.where(used, jnp.clip(seg_end - block_start, 0, MOE_TM), 0).astype(jnp.int32)
    be_last = jnp.sum(jnp.where(idx == last, block_e, 0))
    block_e = jnp.where(used, block_e, be_last).astype(jnp.int32)
    f = jnp.arange(2, dtype=jnp.int32)[None, :]
    fhalf = jnp.where(used[:, None], f ^ (idx[:, None] & 1), 1 ^ (last & 1)).astype(jnp.int32)
    key = (block_e[:, None] * 2 + fhalf).reshape(-1)
    wnew = jnp.concatenate([jnp.ones((1,), jnp.int32), (key[1:] != key[:-1]).astype(jnp.int32)])
    plan = dict(block_e=block_e, fhalf=fhalf.reshape(-1), wnew=wnew, nv=nv,
                xrow=jnp.minimum(idx, last).astype(jnp.int32))
    return dest, plan, n_blocks


def _dispatch_kernel(dest_ref, h_ref, xs_in_ref, xs_ref, sem):
    del xs_in_ref
    tm = h_ref.shape[0]

    def row_copy(t, k):
        d = dest_ref[0, TOP_K * t + k]
        return pltpu.make_async_copy(h_ref.at[pl.ds(t, 1)], xs_ref.at[pl.ds(d, 1)], sem)

    def start(t, carry):
        for k in range(TOP_K):
            row_copy(t, k).start()
        return carry

    lax.fori_loop(0, tm, start, 0, unroll=8)
    for _ in range(TOP_K):
        pltpu.make_async_copy(h_ref, xs_ref.at[pl.ds(0, tm)], sem).wait()


def _dispatch(dest_blocks, h2, xs):
    t, d = h2.shape
    tm = DISP_TM
    return pl.pallas_call(
        _dispatch_kernel,
        grid=(t // tm,),
        in_specs=[
            pl.BlockSpec((None, 1, TOP_K * tm), lambda i: (i, 0, 0), memory_space=pltpu.SMEM),
            pl.BlockSpec((tm, d), lambda i: (i, 0)),
            pl.BlockSpec(memory_space=pl.ANY),
        ],
        out_specs=pl.BlockSpec(memory_space=pl.ANY),
        out_shape=jax.ShapeDtypeStruct(xs.shape, xs.dtype),
        scratch_shapes=[pltpu.SemaphoreType.DMA(())],
        input_output_aliases={2: 0},
        compiler_params=_params(("arbitrary",)),
        name="moe_dispatch",
    )(dest_blocks, h2, xs)


def _expert_kernel(be_ref, fh_ref, wnew_ref, nv_ref, xr_ref, x_ref, wg_ref, wu_ref, wd_ref, y_ref,
                   wg_s, wu_s, wd_s):
    del be_ref, fh_ref, xr_ref
    i = pl.program_id(0)
    f = pl.program_id(1)
    nv = nv_ref[i]
    tm = x_ref.shape[0]

    @pl.when(jnp.logical_and(nv > 0, wnew_ref[2 * i + f] > 0))
    def _():
        wg_s[...] = wg_ref[...].astype(wg_s.dtype)
        wu_s[...] = wu_ref[...].astype(wu_s.dtype)
        wd_s[...] = wd_ref[...].astype(wd_s.dtype)

    for sb in range(tm // MOE_SUB):
        rows = slice(sb * MOE_SUB, (sb + 1) * MOE_SUB)
        live = nv > sb * MOE_SUB

        @pl.when(live)
        def _(rows=rows):
            x = x_ref[rows, :].astype(MXU_DTYPE)
            g = jnp.dot(x, wg_s[...], preferred_element_type=F32)
            u = jnp.dot(x, wu_s[...], preferred_element_type=F32)
            h = (g * jax.nn.sigmoid(g)) * u
            part = jnp.dot(h.astype(MXU_DTYPE), wd_s[...], preferred_element_type=F32)

            @pl.when(f == 0)
            def _():
                y_ref[rows, :] = part

            @pl.when(f != 0)
            def _():
                y_ref[rows, :] += part

        @pl.when(jnp.logical_and(jnp.logical_not(live), f == 0))
        def _(rows=rows):
            y_ref[rows, :] = jnp.zeros((MOE_SUB, y_ref.shape[1]), y_ref.dtype)


def _experts(xs, plan, wg, wu, wd):
    n_slots, d = xs.shape
    ff = wg.shape[-1]
    tm = MOE_TM
    n_blocks = n_slots // tm
    assert ff % 2 == 0
    fh = ff // 2
    grid_spec = pltpu.PrefetchScalarGridSpec(
        num_scalar_prefetch=5,
        grid=(n_blocks, 2),
        in_specs=[
            pl.BlockSpec((tm, d), lambda i, f, be, fhalf, wnew, nv, xr: (xr[i], 0)),
            pl.BlockSpec((None, d, fh), lambda i, f, be, fhalf, wnew, nv, xr: (be[i], 0, fhalf[2 * i + f])),
            pl.BlockSpec((None, d, fh), lambda i, f, be, fhalf, wnew, nv, xr: (be[i], 0, fhalf[2 * i + f])),
            pl.BlockSpec((None, fh, d), lambda i, f, be, fhalf, wnew, nv, xr: (be[i], fhalf[2 * i + f], 0)),
        ],
        out_specs=pl.BlockSpec((tm, d), lambda i, f, be, fhalf, wnew, nv, xr: (i, 0)),
        scratch_shapes=[pltpu.VMEM((d, fh), MXU_DTYPE), pltpu.VMEM((d, fh), MXU_DTYPE),
                        pltpu.VMEM((fh, d), MXU_DTYPE)],
    )
    return pl.pallas_call(
        _expert_kernel,
        grid_spec=grid_spec,
        out_shape=jax.ShapeDtypeStruct((n_slots, d), F32),
        compiler_params=_params(("arbitrary", "arbitrary")),
        name="moe_experts",
    )(plan['block_e'], plan['fhalf'], plan['wnew'], plan['nv'], plan['xrow'], xs, wg, wu, wd)


def _combine_kernel(dest_ref, x1_ref, rt_ref, mod_ref, nf_ref, ys_ref, o_ref, ybuf, sem, *, final):
    tm = x1_ref.shape[0]

    def row_copy(t, k):
        d = dest_ref[0, TOP_K * t + k]
        return pltpu.make_async_copy(ys_ref.at[pl.ds(d, 1)], ybuf.at[pl.ds(k * tm + t, 1)], sem)

    def start(t, carry):
        for k in range(TOP_K):
            row_copy(t, k).start()
        return carry

    lax.fori_loop(0, tm, start, 0, unroll=8)
    for k in range(TOP_K):
        pltpu.make_async_copy(ys_ref.at[pl.ds(0, tm)], ybuf.at[pl.ds(k * tm, tm)], sem).wait()
    rt = rt_ref[...]
    moe = ybuf[0:tm, :] * rt[:, 2:3] + ybuf[tm:2 * tm, :] * rt[:, 3:4]
    x2 = x1_ref[...] + mod_ref[5:6, :] * moe
    if final:
        x2 = _rms(x2) * nf_ref[...]
    o_ref[...] = x2


def _combine(dest_blocks, x1, rt, mod3, boff, seq, nf, ys, final):
    t, d = x1.shape
    tm = COMB_TM
    spb = seq // tm
    kern = functools.partial(_combine_kernel, final=final)
    return pl.pallas_call(
        kern,
        grid=(t // tm,),
        in_specs=[
            pl.BlockSpec((None, 1, TOP_K * tm), lambda i: (i, 0, 0), memory_space=pltpu.SMEM),
            pl.BlockSpec((tm, d), lambda i: (i, 0)),
            pl.BlockSpec((tm, LANES), lambda i: (i, 0)),
            pl.BlockSpec((None, 6, d), lambda i: (boff + i // spb, 0, 0)),
            pl.BlockSpec((1, d), lambda i: (0, 0)),
            pl.BlockSpec(memory_space=pl.ANY),
        ],
        out_specs=pl.BlockSpec((tm, d), lambda i: (i, 0)),
        out_shape=jax.ShapeDtypeStruct((t, d), F32),
        scratch_shapes=[pltpu.VMEM((TOP_K * tm, d), F32), pltpu.SemaphoreType.DMA(())],
        compiler_params=_params(("arbitrary",)),
        name="moe_combine",
    )(dest_blocks, x1, rt, mod3, nf.reshape(1, d), ys)


def kernel(x_prompt, x_sample, c_prompt, c_sample, w_ada, b_ada, norm1, w_in, q_norm, k_norm, w_branch_a, w_branch_b, w_out, rel_bias, norm2, w_route_group, b_route_group, w_route_expert, b_route_expert, w_gate_exp, w_up_exp, w_down_exp, norm_final):
    depth = w_ada.shape[0]
    d = x_prompt.shape[-1]
    trunks = []
    boff = 0
    for x, c in ((x_prompt, c_prompt), (x_sample, c_sample)):
        b, s, _ = x.shape
        trunks.append(dict(x=x.reshape(b * s, d), batch=b, seq=s, boff=boff))
        boff += b
    n_cond = boff
    rows = -(-n_cond // 8) * 8
    c_all = jnp.concatenate([c_prompt, c_sample, jnp.zeros((rows - n_cond, d), F32)], axis=0)
    scale = HEAD_DIM ** -0.5

    for l in range(depth):
        mod3 = _adaln(c_all, w_ada[l], b_ada[l]).reshape(rows, 6, d)
        w_in_l = _inproj_weight(w_in[l].astype(MXU_DTYPE))
        wa = w_branch_a[l].astype(MXU_DTYPE)
        wb = w_branch_b[l].astype(MXU_DTYPE)
        wo = w_out[l].astype(MXU_DTYPE)
        wr = jnp.concatenate([w_route_group[l], w_route_expert[l],
                              jnp.zeros((d, LANES - N_GROUPS - N_EXPERTS), F32)], axis=1)
        wr_hi = wr.astype(MXU_DTYPE)
        wr = jnp.concatenate([wr_hi, (wr - wr_hi.astype(F32)).astype(MXU_DTYPE)], axis=1)
        br = jnp.concatenate([b_route_group[l], b_route_expert[l],
                              jnp.zeros((LANES - N_GROUPS - N_EXPERTS,), F32)]).reshape(1, LANES)

        for tr in trunks:
            qa, kva, *qkv_groups, gates = _inproj(tr['x'], mod3, tr['boff'], tr['batch'], tr['seq'], norm1[l],
                                                  w_in_l, q_norm[l] * scale, k_norm[l])
            ya = _attn_a(qa, kva, tr['batch'], tr['seq'])
            obs, lses = [], []
            for gi, (window, dilation) in enumerate(B_PAIRS):
                hs = slice(gi * B_HEADS_PER_GROUP, (gi + 1) * B_HEADS_PER_GROUP)
                o, lse = _dilated_group(qkv_groups[gi], rel_bias[:, hs], window, dilation)
                obs.append(o)
                lses.append(lse)
            tr['x1'], tr['h2'], tr['rt'] = _post(tr['x'], ya, obs, lses, gates, mod3, tr['boff'], tr['seq'],
                                                 norm2[l], wa, wb, wo, wr, br)

        experts = jnp.concatenate([tr['rt'][:, :TOP_K] for tr in trunks], axis=0).astype(jnp.int32).reshape(-1)
        dest, plan, n_blocks = _slot_plan(experts)
        xs = jnp.zeros((n_blocks * MOE_TM, d), F32)
        a0 = 0
        for tr in trunks:
            n_tok = tr['x'].shape[0]
            tr['dest'] = dest[a0:a0 + TOP_K * n_tok]
            a0 += TOP_K * n_tok
            xs = _dispatch(tr['dest'].reshape(n_tok // DISP_TM, 1, TOP_K * DISP_TM), tr['h2'], xs)
        ys = _experts(xs, plan, w_gate_exp[l], w_up_exp[l], w_down_exp[l])
        for tr in trunks:
            n_tok = tr['x'].shape[0]
            tr['x'] = _combine(tr['dest'].reshape(n_tok // COMB_TM, 1, TOP_K * COMB_TM), tr['x1'], tr['rt'],
                               mod3, tr['boff'], tr['seq'], norm_final, ys, final=(l == depth - 1))

    return tuple(tr['x'].reshape(tr['batch'], tr['seq'], d) for tr in trunks)
```

```python
import functools
import math

import numpy as np
import jax
import jax.numpy as jnp
from jax import lax
from jax.experimental import pallas as pl
from jax.experimental.pallas import tpu as pltpu

F32 = jnp.float32
MXU_DTYPE = jnp.bfloat16

HEAD_DIM = 128
A_HEADS = 8
A_KV_HEADS = 2
A_GROUP = A_HEADS // A_KV_HEADS
ROPE_THETA = 10000.0
GRID_W = 64
B_PAIRS = ((128, 1), (512, 4), (2048, 16))
B_GROUPS = len(B_PAIRS)
B_HEADS_PER_GROUP = 4
B_HEADS = B_GROUPS * B_HEADS_PER_GROUP
REL_BUCKETS = 32
REL_MAX_DIST = 1024
A_Q = A_HEADS * HEAD_DIM
A_KV = A_KV_HEADS * HEAD_DIM
B_W = B_HEADS * HEAD_DIM
B_OUT = B_HEADS_PER_GROUP * HEAD_DIM
N_GROUPS = 8
EXPERTS_PER_GROUP = 8
N_EXPERTS = N_GROUPS * EXPERTS_PER_GROUP
TOP_K = 2
EPS = 1e-6
NEG = -1e30
LANES = 128

VMEM_LIMIT_BYTES = 56 * 1024 * 1024

PROJ_TM = 1024
PROJ_TN = 512
PROJ_SUB = 256
ATT_TQ = 256
DIL_TQ = 128
DIL_SIDE = 64
POST_TM = 256
MOE_TM = 512
MOE_SUB = 256
DISP_TM = 512
COMB_TM = 128
RANK_CHUNK = 512


def _params(sem):
    return pltpu.CompilerParams(dimension_semantics=sem, vmem_limit_bytes=VMEM_LIMIT_BYTES)


def _ada_kernel(c_ref, w_ref, b_ref, o_ref):
    c = c_ref[...]
    a = (c * jax.nn.sigmoid(c)).astype(MXU_DTYPE)
    o_ref[...] = jnp.dot(a, w_ref[...].astype(MXU_DTYPE), preferred_element_type=F32) + b_ref[...]


def _adaln(c_all, w, b):
    r, d = c_all.shape
    n = w.shape[1]
    tn = min(n, 1024)
    return pl.pallas_call(
        _ada_kernel,
        grid=(n // tn,),
        in_specs=[pl.BlockSpec((r, d), lambda j: (0, 0)),
                  pl.BlockSpec((d, tn), lambda j: (0, j)),
                  pl.BlockSpec((1, tn), lambda j: (0, j))],
        out_specs=pl.BlockSpec((r, tn), lambda j: (0, j)),
        out_shape=jax.ShapeDtypeStruct((r, n), F32),
        compiler_params=_params(("arbitrary",)),
        name="adaln",
    )(c_all, w, b.reshape(1, n))


def _rms(x):
    return x * lax.rsqrt(jnp.mean(x * x, axis=-1, keepdims=True) + EPS)


def _sigmoid(x):
    return 0.5 * jnp.tanh(0.5 * x) + 0.5


def _pack_bf16_pairs(x):
    n = x.shape[1] // 2
    lo = pltpu.bitcast(x[:, :n].astype(jnp.bfloat16).astype(F32), jnp.uint32)
    hi = pltpu.bitcast(x[:, n:].astype(jnp.bfloat16).astype(F32), jnp.uint32)
    return (hi & jnp.uint32(0xFFFF0000)) | (lo >> 16)


def _unpack_bf16_pairs(u):
    lo = pltpu.bitcast(u << 16, F32)
    hi = pltpu.bitcast(u & jnp.uint32(0xFFFF0000), F32)
    return jnp.concatenate([lo, hi], axis=1)


def _rope_partner(a):
    lead = a.shape[:-1]
    quarter = HEAD_DIM // 4
    a5 = a.reshape(*lead, a.shape[-1] // HEAD_DIM, 2, 2, quarter)
    return a5[..., ::-1, :].reshape(a.shape)


def _norm_rope(x, xp, g2, cos, sin):
    r = lax.rsqrt(jnp.mean(x * x, axis=-1, keepdims=True) + EPS)
    return (x * (cos * g2[0:1, :]) + xp * (sin * g2[1:2, :])) * r


def _inproj_kernel(x_ref, mod_ref, n1_ref, w_ref, wp_ref, cos_ref, sin_ref, qn_ref, kn_ref,
                   qa_ref, kva_ref, qkv1_ref, qkv2_ref, qkv3_ref, g_ref, h_scr, de_scr, *, j_kv, j_b, j_g):
    j = pl.program_id(1)
    tm = x_ref.shape[0]

    @pl.when(j == 0)
    def _():
        h = _rms(x_ref[...]) * n1_ref[...]
        h = h * (1.0 + mod_ref[1:2, :]) + mod_ref[0:1, :]
        h_scr[...] = h.astype(h_scr.dtype)

    subs = PROJ_TN // PROJ_SUB
    sub_heads = PROJ_SUB // HEAD_DIM

    def piece(c, ref=w_ref):
        return jnp.dot(h_scr[...], ref[:, c * PROJ_SUB:(c + 1) * PROJ_SUB], preferred_element_type=F32)

    def head(a, hh):
        return a[:, hh * HEAD_DIM:(hh + 1) * HEAD_DIM]

    @pl.when(j < j_kv)
    def _():
        for c in range(subs):
            acc = piece(c)
            accp = piece(c, wp_ref)
            for hh in range(sub_heads):
                col = (c * sub_heads + hh) * HEAD_DIM
                qa_ref[:, col:col + HEAD_DIM] = _norm_rope(
                    head(acc, hh), head(accp, hh), qn_ref[...], cos_ref[...], sin_ref[...]).astype(qa_ref.dtype)

    @pl.when(j == j_kv)
    def _():
        for c in range(subs):
            acc = piece(c)
            roped = (c + 1) * sub_heads <= A_KV_HEADS
            assert roped or c * sub_heads >= A_KV_HEADS
            accp = piece(c, wp_ref) if roped else None
            for hh in range(sub_heads):
                gh = c * sub_heads + hh
                if roped:
                    val = _norm_rope(head(acc, hh), head(accp, hh), kn_ref[...], cos_ref[...], sin_ref[...])
                else:
                    val = head(acc, hh)
                kva_ref[:, gh * HEAD_DIM:(gh + 1) * HEAD_DIM] = val.astype(kva_ref.dtype)

    for gi, ((_, dil), ref) in enumerate(zip(B_PAIRS, (qkv1_ref, qkv2_ref, qkv3_ref))):
        j0 = j_b + 3 * gi

        @pl.when(jnp.logical_and(j >= j0, j < j0 + 3))
        def _(dil=dil, ref=ref, j0=j0):
            scale = jnp.where(j == j0, HEAD_DIM ** -0.5, 1.0).astype(F32)
            for c in range(subs):
                val = piece(c) * scale
                if dil == 1:
                    ref[0, :, c * PROJ_SUB:(c + 1) * PROJ_SUB] = val.astype(ref.dtype)
                    continue
                for hh in range(sub_heads):
                    de_scr[c * sub_heads + hh] = head(val, hh)
                for r in range(dil):
                    for hh in range(sub_heads):
                        gh = c * sub_heads + hh
                        ref[r, :, gh * HEAD_DIM:(gh + 1) * HEAD_DIM] = de_scr[
                            gh, pl.ds(r, tm // dil, stride=dil), :].astype(ref.dtype)

    @pl.when(j >= j_g)
    def _():
        for c in range(subs):
            g_ref[:, c * PROJ_SUB:(c + 1) * PROJ_SUB] = _sigmoid(piece(c)).astype(g_ref.dtype)


def _rope_tables(s):
    half = HEAD_DIM // 4
    inv = np.power(ROPE_THETA, -np.arange(half, dtype=np.float64) / half)
    t = np.arange(s)
    row = (t // GRID_W).astype(np.float64)
    col = (t % GRID_W).astype(np.float64)
    ang_r = row[:, None] * inv[None, :]
    ang_c = col[:, None] * inv[None, :]
    cos = np.concatenate([np.cos(ang_r), np.cos(ang_r), np.cos(ang_c), np.cos(ang_c)], axis=1)
    sin = np.concatenate([-np.sin(ang_r), np.sin(ang_r), -np.sin(ang_c), np.sin(ang_c)], axis=1)
    return jnp.asarray(cos, F32), jnp.asarray(sin, F32)


def _inproj_weight(w_in_l):
    b0 = A_Q + 2 * A_KV
    parts = [w_in_l[:, :b0]]
    for gi in range(B_GROUPS):
        for part in range(3):
            c0 = b0 + part * B_W + gi * B_OUT
            parts.append(w_in_l[:, c0:c0 + B_OUT])
    parts.append(w_in_l[:, b0 + 3 * B_W:])
    return jnp.concatenate(parts, axis=1)


def _inproj(x2, mod3, boff, batch, seq, n1, w_in, w_partner, qn, kn):
    t, d = x2.shape
    tm, tn = PROJ_TM, PROJ_TN
    assert t % tm == 0 and seq % tm == 0
    assert A_Q % tn == 0 and 2 * A_KV == tn and B_OUT == tn and d % tn == 0
    j_kv = A_Q // tn
    j_b = j_kv + 1
    j_g = j_b + 3 * B_GROUPS
    n_j = j_g + 2 * d // tn
    assert w_in.shape == (d, n_j * tn) and w_partner.shape == (d, (j_kv + 1) * tn)
    cos, sin = _rope_tables(seq)
    spb = seq // tm
    kern = functools.partial(_inproj_kernel, j_kv=j_kv, j_b=j_b, j_g=j_g)
    grp_specs, grp_shapes = [], []
    for gi, (_, dil) in enumerate(B_PAIRS):
        assert tm % dil == 0 and (tm // dil) % 16 == 0
        j0 = j_b + 3 * gi
        grp_specs.append(pl.BlockSpec((None, dil, tm // dil, tn),
                                      lambda i, j, j0=j0: (i // spb, 0, i % spb, jnp.clip(j - j0, 0, 2))))
        grp_shapes.append(jax.ShapeDtypeStruct((batch, dil, seq // dil, 3 * tn), MXU_DTYPE))
    return pl.pallas_call(
        kern,
        grid=(t // tm, n_j),
        in_specs=[
            pl.BlockSpec((tm, d), lambda i, j: (i, 0)),
            pl.BlockSpec((None, 6, d), lambda i, j: (boff + i // spb, 0, 0)),
            pl.BlockSpec((1, d), lambda i, j: (0, 0)),
            pl.BlockSpec((d, tn), lambda i, j: (0, j)),
            pl.BlockSpec((d, tn), lambda i, j: (0, jnp.minimum(j, j_kv))),
            pl.BlockSpec((tm, HEAD_DIM), lambda i, j: (i % spb, 0)),
            pl.BlockSpec((tm, HEAD_DIM), lambda i, j: (i % spb, 0)),
            pl.BlockSpec((2, HEAD_DIM), lambda i, j: (0, 0)),
            pl.BlockSpec((2, HEAD_DIM), lambda i, j: (0, 0)),
        ],
        out_specs=[
            pl.BlockSpec((tm, tn), lambda i, j: (i, jnp.clip(j, 0, j_kv - 1))),
            pl.BlockSpec((tm, tn), lambda i, j: (i, 0)),
            *grp_specs,
            pl.BlockSpec((tm, tn), lambda i, j: (i, jnp.clip(j - j_g, 0, n_j - j_g - 1))),
        ],
        out_shape=[
            jax.ShapeDtypeStruct((t, A_Q), MXU_DTYPE),
            jax.ShapeDtypeStruct((t, 2 * A_KV), MXU_DTYPE),
            *grp_shapes,
            jax.ShapeDtypeStruct((t, 2 * d), MXU_DTYPE),
        ],
        scratch_shapes=[pltpu.VMEM((tm, d), MXU_DTYPE), pltpu.VMEM((tn // HEAD_DIM, tm, HEAD_DIM), F32)],
        compiler_params=_params(("arbitrary", "arbitrary")),
        name="inproj",
    )(x2, mod3, n1.reshape(1, d), w_in, w_partner, cos, sin,
      jnp.stack([qn, _rope_partner(qn)]), jnp.stack([kn, _rope_partner(kn)]))


def _attn_a_kernel(q_ref, k_ref, v_ref, o_ref):
    k = k_ref[...]
    v = v_ref[...]
    for hh in range(A_GROUP):
        sl = slice(hh * HEAD_DIM, (hh + 1) * HEAD_DIM)
        s = lax.dot_general(q_ref[:, sl], k, (((1,), (1,)), ((), ())), preferred_element_type=F32)
        m = jnp.max(s, axis=-1, keepdims=True)
        p = jnp.exp(s - m)
        l = jnp.sum(p, axis=-1, keepdims=True)
        o = jnp.dot(p.astype(v.dtype), v, preferred_element_type=F32)
        o_ref[:, sl] = (o / l).astype(o_ref.dtype)


def _attn_a(qa, kva, batch, seq):
    t = qa.shape[0]
    tq = min(ATT_TQ, seq)
    nq = seq // tq
    gw = A_GROUP * HEAD_DIM
    return pl.pallas_call(
        _attn_a_kernel,
        grid=(batch, A_KV_HEADS, nq),
        in_specs=[
            pl.BlockSpec((tq, gw), lambda b, kh, qi: (b * nq + qi, kh)),
            pl.BlockSpec((seq, HEAD_DIM), lambda b, kh, qi: (b, kh)),
            pl.BlockSpec((seq, HEAD_DIM), lambda b, kh, qi: (b, A_KV_HEADS + kh)),
        ],
        out_specs=pl.BlockSpec((tq, gw), lambda b, kh, qi: (b * nq + qi, kh)),
        out_shape=jax.ShapeDtypeStruct((t, A_Q), MXU_DTYPE),
        compiler_params=_params(("arbitrary", "arbitrary", "arbitrary")),
        name="attn_dense",
    )(qa, kva, kva)


def _t5_bucket(rel):
    nb = REL_BUCKETS // 2
    ret = (rel > 0).astype(np.int32) * nb
    n = np.abs(rel)
    max_exact = nb // 2
    large = max_exact + (np.log(np.maximum(n, 1) / max_exact) / np.log(REL_MAX_DIST / max_exact)
                         * (nb - max_exact)).astype(np.int32)
    large = np.minimum(large, nb - 1)
    return (ret + np.where(n < max_exact, n, large)).astype(np.int32)


def _band_bias(table_g, dilation, tq, w):
    h = table_g.shape[1]
    offs = np.arange(-DIL_SIDE, DIL_SIDE + 1)
    diag = jnp.take(table_g, jnp.asarray(_t5_bucket(offs * dilation)), axis=0).astype(F32).T
    diag = jnp.concatenate([diag, jnp.full((h, 1), NEG, F32)], axis=1)
    n = tq + w
    u = np.arange(n)
    delta = np.where(u <= w - 1, u, u - n)
    out = []
    for var in range(3):
        m = delta - DIL_SIDE * var
        idx = np.where(np.abs(m) <= DIL_SIDE, m + DIL_SIDE, 2 * DIL_SIDE + 1)
        v = jnp.take(diag, jnp.asarray(idx), axis=1)
        toep = jnp.tile(v, (1, tq))[:, :tq * (n - 1)].reshape(h, tq, n - 1)[:, :, :w]
        out.append(toep)
    return jnp.stack(out, axis=0)


def _dil_kernel(q_ref, k_ref, v_ref, bias_ref, o_ref, lse_ref, *, length, tq, w):
    nblk = length // tq
    lane = lax.broadcasted_iota(jnp.int32, (tq, LANES), 1)

    def body(c, carry):
        q0 = pl.multiple_of(c * tq, tq)
        ws = pl.multiple_of(jnp.clip(q0 - DIL_SIDE, 0, length - w), DIL_SIDE)
        var = (q0 - ws) // DIL_SIDE
        lse_blk = jnp.zeros((tq, LANES), F32)
        for hh in range(B_HEADS_PER_GROUP):
            sl = slice(hh * HEAD_DIM, (hh + 1) * HEAD_DIM)
            q = q_ref[pl.ds(q0, tq), sl]
            kw = k_ref[pl.ds(ws, w), sl]
            vw = v_ref[pl.ds(ws, w), sl]
            s = lax.dot_general(q, kw, (((1,), (1,)), ((), ())), preferred_element_type=F32)
            s = s + bias_ref[var, hh]
            m = jnp.max(s, axis=-1, keepdims=True)
            p = jnp.exp(s - m)
            l = jnp.sum(p, axis=-1, keepdims=True)
            o = jnp.dot(p.astype(vw.dtype), vw, preferred_element_type=F32)
            o_ref[pl.ds(q0, tq), sl] = (o / l).astype(o_ref.dtype)
            lse_blk = jnp.where(lane == hh, m + jnp.log(l), lse_blk)
        lse_ref[pl.ds(q0, tq), :] = lse_blk
        return carry

    lax.fori_loop(0, nblk, body, 0)


def _dilated_group(qkv, table_g, window, dilation):
    assert (window // 2) // dilation == DIL_SIDE
    batch, _, length, _ = qkv.shape
    assert length % DIL_SIDE == 0
    tq = min(DIL_TQ, length)
    w = min(length, tq + 2 * DIL_SIDE)
    assert length % tq == 0
    gw = B_HEADS_PER_GROUP * HEAD_DIM
    bias = _band_bias(table_g, dilation, tq, w)
    kern = functools.partial(_dil_kernel, length=length, tq=tq, w=w)
    return pl.pallas_call(
        kern,
        grid=(batch, dilation),
        in_specs=[
            pl.BlockSpec((None, None, length, gw), lambda b, r: (b, r, 0, 0)),
            pl.BlockSpec((None, None, length, gw), lambda b, r: (b, r, 0, 1)),
            pl.BlockSpec((None, None, length, gw), lambda b, r: (b, r, 0, 2)),
            pl.BlockSpec((3, B_HEADS_PER_GROUP, tq, w), lambda b, r: (0, 0, 0, 0)),
        ],
        out_specs=[
            pl.BlockSpec((None, None, length, gw), lambda b, r: (b, r, 0, 0)),
            pl.BlockSpec((None, None, length, LANES), lambda b, r: (b, r, 0, 0)),
        ],
        out_shape=[
            jax.ShapeDtypeStruct((batch, dilation, length, gw), MXU_DTYPE),
            jax.ShapeDtypeStruct((batch, dilation, length, LANES), F32),
        ],
        compiler_params=_params(("arbitrary", "arbitrary")),
        name=f"attn_dilated_{dilation}",
    )(qkv, qkv, qkv, bias)


def _post_kernel(x_ref, ya_ref, o1_ref, o2_ref, o3_ref, l1_ref, l2_ref, l3_ref, ga_ref, gb_ref,
                 mod_ref, n2_ref, wa_ref, wb_ref, wo_ref, wr_ref, br_ref,
                 x1_ref, h2_ref, rt_ref, o_scr, l_scr):
    tm = x_ref.shape[0]
    lses, outs = [], []
    for gi, ((_, dil), o_ref, l_ref) in enumerate(zip(B_PAIRS, (o1_ref, o2_ref, o3_ref), (l1_ref, l2_ref, l3_ref))):
        if dil == 1:
            lses.append(l_ref[0])
            outs.append([o_ref[0, :, jh * HEAD_DIM:(jh + 1) * HEAD_DIM].astype(F32)
                         for jh in range(B_HEADS_PER_GROUP)])
            continue
        for r in range(dil):
            rows = pl.ds(r, tm // dil, stride=dil)
            l_scr[gi, rows, :] = l_ref[r]
            for jh in range(B_HEADS_PER_GROUP):
                o_scr[gi, jh, rows, :] = o_ref[r, :, jh * HEAD_DIM:(jh + 1) * HEAD_DIM].astype(F32)
        lses.append(l_scr[gi])
        outs.append([o_scr[gi, jh] for jh in range(B_HEADS_PER_GROUP)])
    parts = []
    for jh in range(B_HEADS_PER_GROUP):
        lj = [lg[:, jh:jh + 1] for lg in lses]
        mx = jnp.maximum(jnp.maximum(lj[0], lj[1]), lj[2])
        ex = [jnp.exp(v - mx) for v in lj]
        inv = 1.0 / (ex[0] + ex[1] + ex[2])
        acc = (ex[0] * inv) * outs[0][jh]
        acc = acc + (ex[1] * inv) * outs[1][jh]
        acc = acc + (ex[2] * inv) * outs[2][jh]
        parts.append(acc.astype(MXU_DTYPE))
    yb = jnp.concatenate(parts, axis=1)
    a = jnp.dot(ya_ref[...], wa_ref[...], preferred_element_type=F32)
    bb = jnp.dot(yb, wb_ref[...], preferred_element_type=F32)
    merged = ga_ref[...].astype(F32) * a + gb_ref[...].astype(F32) * bb
    out = jnp.dot(merged.astype(MXU_DTYPE), wo_ref[...], preferred_element_type=F32)
    x1 = x_ref[...] + mod_ref[2:3, :] * out
    x1_ref[...] = x1
    h2 = _rms(x1) * n2_ref[...]
    h2 = h2 * (1.0 + mod_ref[4:5, :]) + mod_ref[3:4, :]
    h2_ref[...] = _pack_bf16_pairs(h2)

    h_hi = h2.astype(MXU_DTYPE)
    h_lo = (h2 - h_hi.astype(F32)).astype(MXU_DTYPE)
    r_hi = jnp.dot(h_hi, wr_ref[...], preferred_element_type=F32)
    r_lo = jnp.dot(h_lo, wr_ref[:, :LANES], preferred_element_type=F32)
    r = r_hi[:, :LANES] + r_hi[:, LANES:] + r_lo + br_ref[...]
    lane = lax.broadcasted_iota(jnp.int32, r.shape, 1)
    lanef = lane.astype(F32)
    ninf = -jnp.inf
    gmask = lane < N_GROUPS
    gl = jnp.where(gmask, r, ninf)
    gmax = jnp.max(gl, axis=-1, keepdims=True)
    grp = jnp.min(jnp.where(gl == gmax, lanef, float(LANES)), axis=-1, keepdims=True)
    gsum = jnp.sum(jnp.where(gmask, jnp.exp(jnp.where(gmask, r, gmax) - gmax), 0.0), axis=-1, keepdims=True)
    gate = 1.0 / gsum
    lo = N_GROUPS + grp * EXPERTS_PER_GROUP
    emask = jnp.logical_and(lanef >= lo, lanef < lo + EXPERTS_PER_GROUP)
    el = jnp.where(emask, r, ninf)
    v1 = jnp.max(el, axis=-1, keepdims=True)
    i1 = jnp.min(jnp.where(el == v1, lanef, float(LANES)), axis=-1, keepdims=True)
    el2 = jnp.where(lanef == i1, ninf, el)
    v2 = jnp.max(el2, axis=-1, keepdims=True)
    i2 = jnp.min(jnp.where(el2 == v2, lanef, float(LANES)), axis=-1, keepdims=True)
    tt = jnp.exp(v2 - v1)
    w1 = 1.0 / (1.0 + tt)
    w2 = tt * w1
    rt = jnp.where(lane == 0, i1 - N_GROUPS,
                   jnp.where(lane == 1, i2 - N_GROUPS,
                             jnp.where(lane == 2, gate * w1,
                                       jnp.where(lane == 3, gate * w2, 0.0))))
    rt_ref[...] = rt


def _post(x2, ya, obs, lses, gates, mod3, boff, seq, n2, wa, wb, wo, wr, br):
    t, d = x2.shape
    tm = POST_TM
    assert t % tm == 0 and seq % tm == 0
    spb = seq // tm
    row = lambda i: (i, 0)
    const = lambda i: (0, 0)
    res = lambda i: (i // spb, 0, i % spb, 0)
    o_specs, l_specs = [], []
    for _, dil in B_PAIRS:
        assert tm % dil == 0 and (tm // dil) % 16 == 0
        o_specs.append(pl.BlockSpec((None, dil, tm // dil, B_OUT), res))
        l_specs.append(pl.BlockSpec((None, dil, tm // dil, LANES), res))
    return pl.pallas_call(
        _post_kernel,
        grid=(t // tm,),
        in_specs=[
            pl.BlockSpec((tm, d), row),
            pl.BlockSpec((tm, A_Q), row),
            *o_specs,
            *l_specs,
            pl.BlockSpec((tm, d), lambda i: (i, 0)),
            pl.BlockSpec((tm, d), lambda i: (i, 1)),
            pl.BlockSpec((None, 6, d), lambda i: (boff + i // spb, 0, 0)),
            pl.BlockSpec((1, d), const),
            pl.BlockSpec((A_Q, d), const, pipeline_mode=pl.Buffered(1)),
            pl.BlockSpec((B_OUT, d), const, pipeline_mode=pl.Buffered(1)),
            pl.BlockSpec((d, d), const, pipeline_mode=pl.Buffered(1)),
            pl.BlockSpec((d, 2 * LANES), const, pipeline_mode=pl.Buffered(1)),
            pl.BlockSpec((1, LANES), const),
        ],
        out_specs=[pl.BlockSpec((tm, d), row), pl.BlockSpec((tm, d // 2), row), pl.BlockSpec((tm, LANES), row)],
        out_shape=[jax.ShapeDtypeStruct((t, d), F32), jax.ShapeDtypeStruct((t, d // 2), jnp.uint32),
                   jax.ShapeDtypeStruct((t, LANES), F32)],
        scratch_shapes=[pltpu.VMEM((B_GROUPS, B_HEADS_PER_GROUP, tm, HEAD_DIM), F32),
                        pltpu.VMEM((B_GROUPS, tm, LANES), F32)],
        compiler_params=_params(("arbitrary",)),
        name="post_attn",
    )(x2, ya, obs[0], obs[1], obs[2], lses[0], lses[1], lses[2], gates, gates, mod3, n2.reshape(1, d),
      wa, wb, wo, wr, br)


def _slot_plan(experts):
    a = experts.shape[0]
    c = RANK_CHUNK
    assert a % c == 0 and a % MOE_TM == 0
    oh = (experts[:, None] == jnp.arange(N_EXPERTS, dtype=jnp.int32)[None, :])
    oh3 = oh.astype(jnp.bfloat16).reshape(a // c, c, N_EXPERTS)
    tri = (np.arange(c)[:, None] > np.arange(c)[None, :])
    within = jnp.einsum('ij,cjk->cik', jnp.asarray(tri, jnp.bfloat16), oh3, preferred_element_type=F32)
    tot = jnp.sum(oh3.astype(F32), axis=1)
    off = jnp.cumsum(tot, axis=0) - tot
    counts = jnp.sum(tot, axis=0).astype(jnp.int32)
    padded = (counts + MOE_TM - 1) // MOE_TM * MOE_TM
    pend = jnp.cumsum(padded)
    pstart = pend - padded
    slot = within + off[:, None, :] + pstart.astype(F32)[None, None, :]
    dest = jnp.sum(jnp.where(oh.reshape(a // c, c, N_EXPERTS), slot, 0.0), axis=-1).reshape(a).astype(jnp.int32)
    n_blocks = a // MOE_TM + N_EXPERTS
    idx = jnp.arange(n_blocks, dtype=jnp.int32)
    block_start = idx * MOE_TM
    block_e = jnp.minimum(jnp.sum((pend[None, :] <= block_start[:, None]).astype(jnp.int32), axis=1),
                          N_EXPERTS - 1)
    nb_used = pend[-1] // MOE_TM
    used = idx < nb_used
    last = nb_used - 1
    onehot_e = block_e[:, None] == jnp.arange(N_EXPERTS, dtype=jnp.int32)[None, :]
    seg_end = jnp.sum(jnp.where(onehot_e, (pstart + counts)[None, :], 0), axis=1)
    nv = jnp.where(used, jnp.clip(seg_end - block_start, 0, MOE_TM), 0).astype(jnp.int32)
    be_last = jnp.sum(jnp.where(idx == last, block_e, 0))
    block_e = jnp.where(used, block_e, be_last).astype(jnp.int32)
    f = jnp.arange(2, dtype=jnp.int32)[None, :]
    fhalf = jnp.where(used[:, None], f ^ (idx[:, None] & 1), 1 ^ (last & 1)).astype(jnp.int32)
    plan = dict(block_e=block_e, fhalf=fhalf.reshape(-1), nv=nv,
                xrow=jnp.minimum(idx, last).astype(jnp.int32))
    return dest, plan, n_blocks


def _dispatch_kernel(dest_ref, h_ref, xs_in_ref, xs_ref, sem):
    del xs_in_ref
    tm = h_ref.shape[0]

    def row_copy(t, k):
        d = dest_ref[0, TOP_K * t + k]
        return pltpu.make_async_copy(h_ref.at[pl.ds(t, 1)], xs_ref.at[pl.ds(d, 1)], sem)

    def start(t, carry):
        for k in range(TOP_K):
            row_copy(t, k).start()
        return carry

    lax.fori_loop(0, tm, start, 0, unroll=8)
    for _ in range(TOP_K):
        pltpu.make_async_copy(h_ref, xs_ref.at[pl.ds(0, tm)], sem).wait()


def _dispatch(dest_blocks, h2, xs):
    t, d = h2.shape
    tm = DISP_TM
    return pl.pallas_call(
        _dispatch_kernel,
        grid=(t // tm,),
        in_specs=[
            pl.BlockSpec((None, 1, TOP_K * tm), lambda i: (i, 0, 0), memory_space=pltpu.SMEM),
            pl.BlockSpec((tm, d), lambda i: (i, 0)),
            pl.BlockSpec(memory_space=pl.ANY),
        ],
        out_specs=pl.BlockSpec(memory_space=pl.ANY),
        out_shape=jax.ShapeDtypeStruct(xs.shape, xs.dtype),
        scratch_shapes=[pltpu.SemaphoreType.DMA(())],
        input_output_aliases={2: 0},
        compiler_params=_params(("arbitrary",)),
        name="moe_dispatch",
    )(dest_blocks, h2, xs)


def _expert_kernel(be_ref, fh_ref, nv_ref, xr_ref, x_ref, wg_ref, wu_ref, wd_ref, y_ref,
                   wg_s, wu_s, wd_s):
    del be_ref, fh_ref, xr_ref
    i = pl.program_id(0)
    f = pl.program_id(1)
    nv = nv_ref[i]
    tm = x_ref.shape[0]

    for sb in range(tm // MOE_SUB):
        rows = slice(sb * MOE_SUB, (sb + 1) * MOE_SUB)
        live = nv > sb * MOE_SUB

        @pl.when(live)
        def _(rows=rows, sb=sb):
            if sb == 0:
                wg_s[...] = wg_ref[...].astype(wg_s.dtype)
                wu_s[...] = wu_ref[...].astype(wu_s.dtype)
                wd_s[...] = wd_ref[...].astype(wd_s.dtype)
            x = _unpack_bf16_pairs(x_ref[rows, :]).astype(MXU_DTYPE)
            g = jnp.dot(x, wg_s[...], preferred_element_type=F32)
            u = jnp.dot(x, wu_s[...], preferred_element_type=F32)
            h = (g * _sigmoid(g)) * u
            part = jnp.dot(h.astype(MXU_DTYPE), wd_s[...], preferred_element_type=F32)

            @pl.when(f == 0)
            def _():
                y_ref[rows, :] = part

            @pl.when(f != 0)
            def _():
                y_ref[rows, :] += part

        @pl.when(jnp.logical_and(jnp.logical_not(live), f == 0))
        def _(rows=rows):
            y_ref[rows, :] = jnp.zeros((MOE_SUB, y_ref.shape[1]), y_ref.dtype)


def _experts(xs, plan, wg, wu, wd):
    n_slots = xs.shape[0]
    _, d, ff = wg.shape
    tm = MOE_TM
    n_blocks = n_slots // tm
    assert ff % 2 == 0 and xs.shape[1] * 2 == d
    fh = ff // 2
    grid_spec = pltpu.PrefetchScalarGridSpec(
        num_scalar_prefetch=4,
        grid=(n_blocks, 2),
        in_specs=[
            pl.BlockSpec((tm, d // 2), lambda i, f, be, fhalf, nv, xr: (xr[i], 0)),
            pl.BlockSpec((None, d, fh), lambda i, f, be, fhalf, nv, xr: (be[i], 0, fhalf[2 * i + f])),
            pl.BlockSpec((None, d, fh), lambda i, f, be, fhalf, nv, xr: (be[i], 0, fhalf[2 * i + f])),
            pl.BlockSpec((None, fh, d), lambda i, f, be, fhalf, nv, xr: (be[i], fhalf[2 * i + f], 0)),
        ],
        out_specs=pl.BlockSpec((tm, d), lambda i, f, be, fhalf, nv, xr: (i, 0)),
        scratch_shapes=[pltpu.VMEM((d, fh), MXU_DTYPE), pltpu.VMEM((d, fh), MXU_DTYPE),
                        pltpu.VMEM((fh, d), MXU_DTYPE)],
    )
    return pl.pallas_call(
        _expert_kernel,
        grid_spec=grid_spec,
        out_shape=jax.ShapeDtypeStruct((n_slots, d), F32),
        compiler_params=_params(("arbitrary", "arbitrary")),
        name="moe_experts",
    )(plan['block_e'], plan['fhalf'], plan['nv'], plan['xrow'], xs, wg, wu, wd)


def _combine_kernel(dest_ref, dest_next_ref, x1_ref, rt_ref, mod_ref, nf_ref, ys_ref, o_ref, ybuf, sem, *, final):
    i = pl.program_id(0)
    tm = x1_ref.shape[0]
    slot = i % 2

    def issue(idx_ref, s):
        def start(t, carry):
            for k in range(TOP_K):
                d = idx_ref[0, TOP_K * t + k]
                pltpu.make_async_copy(ys_ref.at[pl.ds(d, 1)], ybuf.at[s, pl.ds(k * tm + t, 1)], sem.at[s]).start()
            return carry

        lax.fori_loop(0, tm, start, 0, unroll=8)

    @pl.when(i == 0)
    def _():
        issue(dest_ref, 0)

    @pl.when(i + 1 < pl.num_programs(0))
    def _():
        issue(dest_next_ref, 1 - slot)

    for k in range(TOP_K):
        pltpu.make_async_copy(ys_ref.at[pl.ds(0, tm)], ybuf.at[slot, pl.ds(k * tm, tm)], sem.at[slot]).wait()
    rt = rt_ref[...]
    moe = ybuf[slot, 0:tm, :] * rt[:, 2:3] + ybuf[slot, tm:2 * tm, :] * rt[:, 3:4]
    x2 = x1_ref[...] + mod_ref[5:6, :] * moe
    if final:
        x2 = _rms(x2) * nf_ref[...]
    o_ref[...] = x2


def _combine(dest_blocks, x1, rt, mod3, boff, seq, nf, ys, final):
    t, d = x1.shape
    tm = COMB_TM
    spb = seq // tm
    kern = functools.partial(_combine_kernel, final=final)
    n_steps = t // tm
    return pl.pallas_call(
        kern,
        grid=(n_steps,),
        in_specs=[
            pl.BlockSpec((None, 1, TOP_K * tm), lambda i: (i, 0, 0), memory_space=pltpu.SMEM),
            pl.BlockSpec((None, 1, TOP_K * tm), lambda i: (jnp.minimum(i + 1, n_steps - 1), 0, 0),
                         memory_space=pltpu.SMEM),
            pl.BlockSpec((tm, d), lambda i: (i, 0)),
            pl.BlockSpec((tm, LANES), lambda i: (i, 0)),
            pl.BlockSpec((None, 6, d), lambda i: (boff + i // spb, 0, 0)),
            pl.BlockSpec((1, d), lambda i: (0, 0)),
            pl.BlockSpec(memory_space=pl.ANY),
        ],
        out_specs=pl.BlockSpec((tm, d), lambda i: (i, 0)),
        out_shape=jax.ShapeDtypeStruct((t, d), F32),
        scratch_shapes=[pltpu.VMEM((2, TOP_K * tm, d), F32), pltpu.SemaphoreType.DMA((2,))],
        compiler_params=_params(("arbitrary",)),
        name="moe_combine",
    )(dest_blocks, dest_blocks, x1, rt, mod3, nf.reshape(1, d), ys)


def kernel(x_prompt, x_sample, c_prompt, c_sample, w_ada, b_ada, norm1, w_in, q_norm, k_norm, w_branch_a, w_branch_b, w_out, rel_bias, norm2, w_route_group, b_route_group, w_route_expert, b_route_expert, w_gate_exp, w_up_exp, w_down_exp, norm_final):
    depth = w_ada.shape[0]
    d = x_prompt.shape[-1]
    trunks = []
    boff = 0
    for x, c in ((x_prompt, c_prompt), (x_sample, c_sample)):
        b, s, _ = x.shape
        trunks.append(dict(x=x.reshape(b * s, d), batch=b, seq=s, boff=boff))
        boff += b
    n_cond = boff
    rows = -(-n_cond // 8) * 8
    c_all = jnp.concatenate([c_prompt, c_sample, jnp.zeros((rows - n_cond, d), F32)], axis=0)
    scale = HEAD_DIM ** -0.5

    for l in range(depth):
        mod3 = _adaln(c_all, w_ada[l], b_ada[l]).reshape(rows, 6, d)
        w_in_l = _inproj_weight(w_in[l].astype(MXU_DTYPE))
        w_partner = _rope_partner(w_in_l[:, :A_Q + 2 * A_KV])
        wa = w_branch_a[l].astype(MXU_DTYPE)
        wb = w_branch_b[l].astype(MXU_DTYPE)
        wo = w_out[l].astype(MXU_DTYPE)
        wr = jnp.concatenate([w_route_group[l], w_route_expert[l],
                              jnp.zeros((d, LANES - N_GROUPS - N_EXPERTS), F32)], axis=1)
        wr_hi = wr.astype(MXU_DTYPE)
        wr = jnp.concatenate([wr_hi, (wr - wr_hi.astype(F32)).astype(MXU_DTYPE)], axis=1)
        br = jnp.concatenate([b_route_group[l], b_route_expert[l],
                              jnp.zeros((LANES - N_GROUPS - N_EXPERTS,), F32)]).reshape(1, LANES)

        for tr in trunks:
            qa, kva, *qkv_groups, gates = _inproj(tr['x'], mod3, tr['boff'], tr['batch'], tr['seq'], norm1[l],
                                                  w_in_l, w_partner, q_norm[l] * scale, k_norm[l])
            ya = _attn_a(qa, kva, tr['batch'], tr['seq'])
            obs, lses = [], []
            for gi, (window, dilation) in enumerate(B_PAIRS):
                hs = slice(gi * B_HEADS_PER_GROUP, (gi + 1) * B_HEADS_PER_GROUP)
                o, lse = _dilated_group(qkv_groups[gi], rel_bias[:, hs], window, dilation)
                obs.append(o)
                lses.append(lse)
            tr['x1'], tr['h2'], tr['rt'] = _post(tr['x'], ya, obs, lses, gates, mod3, tr['boff'], tr['seq'],
                                                 norm2[l], wa, wb, wo, wr, br)

        experts = jnp.concatenate([tr['rt'][:, :TOP_K] for tr in trunks], axis=0).astype(jnp.int32).reshape(-1)
        dest, plan, n_blocks = _slot_plan(experts)
        xs = jnp.zeros((n_blocks * MOE_TM, d // 2), jnp.uint32)
        a0 = 0
        for tr in trunks:
            n_tok = tr['x'].shape[0]
            tr['dest'] = dest[a0:a0 + TOP_K * n_tok]
            a0 += TOP_K * n_tok
            xs = _dispatch(tr['dest'].reshape(n_tok // DISP_TM, 1, TOP_K * DISP_TM), tr['h2'], xs)
        ys = _experts(xs, plan, w_gate_exp[l], w_up_exp[l], w_down_exp[l])
        for tr in trunks:
            n_tok = tr['x'].shape[0]
            tr['x'] = _combine(tr['dest'].reshape(n_tok // COMB_TM, 1, TOP_K * COMB_TM), tr['x1'], tr['rt'],
                               mod3, tr['boff'], tr['seq'], norm_final, ys, final=(l == depth - 1))

    return tuple(tr['x'].reshape(tr['batch'], tr['seq'], d) for tr in trunks)
```

```python
import functools
import math

import numpy as np
import jax
import jax.numpy as jnp
from jax import lax
from jax.experimental import pallas as pl
from jax.experimental.pallas import tpu as pltpu

F32 = jnp.float32
MXU_DTYPE = jnp.bfloat16

HEAD_DIM = 128
A_HEADS = 8
A_KV_HEADS = 2
A_GROUP = A_HEADS // A_KV_HEADS
ROPE_THETA = 10000.0
GRID_W = 64
B_PAIRS = ((128, 1), (512, 4), (2048, 16))
B_GROUPS = len(B_PAIRS)
B_HEADS_PER_GROUP = 4
B_HEADS = B_GROUPS * B_HEADS_PER_GROUP
REL_BUCKETS = 32
REL_MAX_DIST = 1024
A_Q = A_HEADS * HEAD_DIM
A_KV = A_KV_HEADS * HEAD_DIM
B_W = B_HEADS * HEAD_DIM
B_OUT = B_HEADS_PER_GROUP * HEAD_DIM
N_GROUPS = 8
EXPERTS_PER_GROUP = 8
N_EXPERTS = N_GROUPS * EXPERTS_PER_GROUP
TOP_K = 2
EPS = 1e-6
NEG = -1e30
LANES = 128

VMEM_LIMIT_BYTES = 56 * 1024 * 1024

PROJ_TM = 1024
PROJ_TN = 512
PROJ_SUB = 256
ATT_TQ = 256
DIL_TQ = 128
DIL_SIDE = 64
DIL_UNROLL = 4
POST_TM = 256
MOE_TM = 512
MOE_SUB = 256
DISP_TM = 512
COMB_TM = 256
RANK_CHUNK = 512


def _params(sem):
    return pltpu.CompilerParams(dimension_semantics=sem, vmem_limit_bytes=VMEM_LIMIT_BYTES)


def _ada_kernel(c_ref, w_ref, b_ref, o_ref):
    c = c_ref[...]
    a = (c * jax.nn.sigmoid(c)).astype(MXU_DTYPE)
    o_ref[...] = jnp.dot(a, w_ref[...].astype(MXU_DTYPE), preferred_element_type=F32) + b_ref[...]


def _adaln(c_all, w, b):
    r, d = c_all.shape
    n = w.shape[1]
    tn = min(n, 1024)
    return pl.pallas_call(
        _ada_kernel,
        grid=(n // tn,),
        in_specs=[pl.BlockSpec((r, d), lambda j: (0, 0)),
                  pl.BlockSpec((d, tn), lambda j: (0, j)),
                  pl.BlockSpec((1, tn), lambda j: (0, j))],
        out_specs=pl.BlockSpec((r, tn), lambda j: (0, j)),
        out_shape=jax.ShapeDtypeStruct((r, n), F32),
        compiler_params=_params(("arbitrary",)),
        name="adaln",
    )(c_all, w, b.reshape(1, n))


def _rms(x):
    return x * lax.rsqrt(jnp.mean(x * x, axis=-1, keepdims=True) + EPS)


def _sigmoid(x):
    return 0.5 * jnp.tanh(0.5 * x) + 0.5


def _pack_bf16_pairs(x):
    n = x.shape[1] // 2
    lo = pltpu.bitcast(x[:, :n].astype(jnp.bfloat16).astype(F32), jnp.uint32)
    hi = pltpu.bitcast(x[:, n:].astype(jnp.bfloat16).astype(F32), jnp.uint32)
    return (hi & jnp.uint32(0xFFFF0000)) | (lo >> 16)


def _unpack_bf16_pairs(u):
    lo = pltpu.bitcast(u << 16, F32)
    hi = pltpu.bitcast(u & jnp.uint32(0xFFFF0000), F32)
    return jnp.concatenate([lo, hi], axis=1)


def _rope_partner(a):
    lead = a.shape[:-1]
    quarter = HEAD_DIM // 4
    a5 = a.reshape(*lead, a.shape[-1] // HEAD_DIM, 2, 2, quarter)
    return a5[..., ::-1, :].reshape(a.shape)


def _norm_rope(x, xp, g2, cos, sin):
    r = lax.rsqrt(jnp.mean(x * x, axis=-1, keepdims=True) + EPS)
    return (x * (cos * g2[0:1, :]) + xp * (sin * g2[1:2, :])) * r


def _inproj_kernel(x_ref, mod_ref, n1_ref, w_ref, wp_ref, cos_ref, sin_ref, qn_ref, kn_ref,
                   qa_ref, kva_ref, qkv1_ref, qkv2_ref, qkv3_ref, g_ref, h_scr, de_scr, *, j_kv, j_b, j_g):
    j = pl.program_id(1)
    tm = x_ref.shape[0]

    @pl.when(j == 0)
    def _():
        h = _rms(x_ref[...]) * n1_ref[...]
        h = h * (1.0 + mod_ref[1:2, :]) + mod_ref[0:1, :]
        h_scr[...] = h.astype(h_scr.dtype)

    subs = PROJ_TN // PROJ_SUB
    sub_heads = PROJ_SUB // HEAD_DIM

    def piece(c, ref=w_ref):
        return jnp.dot(h_scr[...], ref[:, c * PROJ_SUB:(c + 1) * PROJ_SUB], preferred_element_type=F32)

    def head(a, hh):
        return a[:, hh * HEAD_DIM:(hh + 1) * HEAD_DIM]

    @pl.when(j < j_kv)
    def _():
        for c in range(subs):
            acc = piece(c)
            accp = piece(c, wp_ref)
            for hh in range(sub_heads):
                col = (c * sub_heads + hh) * HEAD_DIM
                qa_ref[:, col:col + HEAD_DIM] = _norm_rope(
                    head(acc, hh), head(accp, hh), qn_ref[...], cos_ref[...], sin_ref[...]).astype(qa_ref.dtype)

    @pl.when(j == j_kv)
    def _():
        for c in range(subs):
            acc = piece(c)
            roped = (c + 1) * sub_heads <= A_KV_HEADS
            assert roped or c * sub_heads >= A_KV_HEADS
            accp = piece(c, wp_ref) if roped else None
            for hh in range(sub_heads):
                gh = c * sub_heads + hh
                if roped:
                    val = _norm_rope(head(acc, hh), head(accp, hh), kn_ref[...], cos_ref[...], sin_ref[...])
                else:
                    val = head(acc, hh)
                kva_ref[:, gh * HEAD_DIM:(gh + 1) * HEAD_DIM] = val.astype(kva_ref.dtype)

    for gi, ((_, dil), ref) in enumerate(zip(B_PAIRS, (qkv1_ref, qkv2_ref, qkv3_ref))):
        j0 = j_b + 3 * gi

        @pl.when(jnp.logical_and(j >= j0, j < j0 + 3))
        def _(dil=dil, ref=ref, j0=j0):
            scale = jnp.where(j == j0, HEAD_DIM ** -0.5, 1.0).astype(F32)
            for c in range(subs):
                val = piece(c) * scale
                if dil == 1:
                    ref[0, :, c * PROJ_SUB:(c + 1) * PROJ_SUB] = val.astype(ref.dtype)
                    continue
                for hh in range(sub_heads):
                    de_scr[c * sub_heads + hh] = head(val, hh)
                for r in range(dil):
                    for hh in range(sub_heads):
                        gh = c * sub_heads + hh
                        ref[r, :, gh * HEAD_DIM:(gh + 1) * HEAD_DIM] = de_scr[
                            gh, pl.ds(r, tm // dil, stride=dil), :].astype(ref.dtype)

    @pl.when(j >= j_g)
    def _():
        for c in range(subs):
            g_ref[:, c * PROJ_SUB:(c + 1) * PROJ_SUB] = _sigmoid(piece(c)).astype(g_ref.dtype)


def _rope_tables(s):
    half = HEAD_DIM // 4
    inv = np.power(ROPE_THETA, -np.arange(half, dtype=np.float64) / half)
    t = np.arange(s)
    row = (t // GRID_W).astype(np.float64)
    col = (t % GRID_W).astype(np.float64)
    ang_r = row[:, None] * inv[None, :]
    ang_c = col[:, None] * inv[None, :]
    cos = np.concatenate([np.cos(ang_r), np.cos(ang_r), np.cos(ang_c), np.cos(ang_c)], axis=1)
    sin = np.concatenate([-np.sin(ang_r), np.sin(ang_r), -np.sin(ang_c), np.sin(ang_c)], axis=1)
    return jnp.asarray(cos, F32), jnp.asarray(sin, F32)


def _inproj_weight(w_in_l):
    b0 = A_Q + 2 * A_KV
    parts = [w_in_l[:, :b0]]
    for gi in range(B_GROUPS):
        for part in range(3):
            c0 = b0 + part * B_W + gi * B_OUT
            parts.append(w_in_l[:, c0:c0 + B_OUT])
    parts.append(w_in_l[:, b0 + 3 * B_W:])
    return jnp.concatenate(parts, axis=1)


def _inproj(x2, mod3, boff, batch, seq, n1, w_in, w_partner, qn, kn):
    t, d = x2.shape
    tm, tn = PROJ_TM, PROJ_TN
    assert t % tm == 0 and seq % tm == 0
    assert A_Q % tn == 0 and 2 * A_KV == tn and B_OUT == tn and d % tn == 0
    j_kv = A_Q // tn
    j_b = j_kv + 1
    j_g = j_b + 3 * B_GROUPS
    n_j = j_g + 2 * d // tn
    assert w_in.shape == (d, n_j * tn) and w_partner.shape == (d, (j_kv + 1) * tn)
    cos, sin = _rope_tables(seq)
    spb = seq // tm
    kern = functools.partial(_inproj_kernel, j_kv=j_kv, j_b=j_b, j_g=j_g)
    grp_specs, grp_shapes = [], []
    for gi, (_, dil) in enumerate(B_PAIRS):
        assert tm % dil == 0 and (tm // dil) % 16 == 0
        j0 = j_b + 3 * gi
        grp_specs.append(pl.BlockSpec((None, dil, tm // dil, tn),
                                      lambda i, j, j0=j0: (i // spb, 0, i % spb, jnp.clip(j - j0, 0, 2))))
        grp_shapes.append(jax.ShapeDtypeStruct((batch, dil, seq // dil, 3 * tn), MXU_DTYPE))
    return pl.pallas_call(
        kern,
        grid=(t // tm, n_j),
        in_specs=[
            pl.BlockSpec((tm, d), lambda i, j: (i, 0)),
            pl.BlockSpec((None, 6, d), lambda i, j: (boff + i // spb, 0, 0)),
            pl.BlockSpec((1, d), lambda i, j: (0, 0)),
            pl.BlockSpec((d, tn), lambda i, j: (0, j)),
            pl.BlockSpec((d, tn), lambda i, j: (0, jnp.minimum(j, j_kv))),
            pl.BlockSpec((tm, HEAD_DIM), lambda i, j: (i % spb, 0)),
            pl.BlockSpec((tm, HEAD_DIM), lambda i, j: (i % spb, 0)),
            pl.BlockSpec((2, HEAD_DIM), lambda i, j: (0, 0)),
            pl.BlockSpec((2, HEAD_DIM), lambda i, j: (0, 0)),
        ],
        out_specs=[
            pl.BlockSpec((tm, tn), lambda i, j: (i, jnp.clip(j, 0, j_kv - 1))),
            pl.BlockSpec((tm, tn), lambda i, j: (i, 0)),
            *grp_specs,
            pl.BlockSpec((tm, tn), lambda i, j: (i, jnp.clip(j - j_g, 0, n_j - j_g - 1))),
        ],
        out_shape=[
            jax.ShapeDtypeStruct((t, A_Q), MXU_DTYPE),
            jax.ShapeDtypeStruct((t, 2 * A_KV), MXU_DTYPE),
            *grp_shapes,
            jax.ShapeDtypeStruct((t, 2 * d), MXU_DTYPE),
        ],
        scratch_shapes=[pltpu.VMEM((tm, d), MXU_DTYPE), pltpu.VMEM((tn // HEAD_DIM, tm, HEAD_DIM), F32)],
        compiler_params=_params(("arbitrary", "arbitrary")),
        name="inproj",
    )(x2, mod3, n1.reshape(1, d), w_in, w_partner, cos, sin,
      jnp.stack([qn, _rope_partner(qn)]), jnp.stack([kn, _rope_partner(kn)]))


def _attn_a_kernel(q_ref, k_ref, v_ref, o_ref):
    k = k_ref[...]
    v = v_ref[...]
    for hh in range(A_GROUP):
        sl = slice(hh * HEAD_DIM, (hh + 1) * HEAD_DIM)
        s = lax.dot_general(q_ref[:, sl], k, (((1,), (1,)), ((), ())), preferred_element_type=F32)
        m = jnp.max(s, axis=-1, keepdims=True)
        p = jnp.exp(s - m)
        l = jnp.sum(p, axis=-1, keepdims=True)
        o = jnp.dot(p.astype(v.dtype), v, preferred_element_type=F32)
        o_ref[:, sl] = (o / l).astype(o_ref.dtype)


def _attn_a(qa, kva, batch, seq):
    t = qa.shape[0]
    tq = min(ATT_TQ, seq)
    nq = seq // tq
    gw = A_GROUP * HEAD_DIM
    return pl.pallas_call(
        _attn_a_kernel,
        grid=(batch, A_KV_HEADS, nq),
        in_specs=[
            pl.BlockSpec((tq, gw), lambda b, kh, qi: (b * nq + qi, kh)),
            pl.BlockSpec((seq, HEAD_DIM), lambda b, kh, qi: (b, kh)),
            pl.BlockSpec((seq, HEAD_DIM), lambda b, kh, qi: (b, A_KV_HEADS + kh)),
        ],
        out_specs=pl.BlockSpec((tq, gw), lambda b, kh, qi: (b * nq + qi, kh)),
        out_shape=jax.ShapeDtypeStruct((t, A_Q), MXU_DTYPE),
        compiler_params=_params(("arbitrary", "arbitrary", "arbitrary")),
        name="attn_dense",
    )(qa, kva, kva)


def _t5_bucket(rel):
    nb = REL_BUCKETS // 2
    ret = (rel > 0).astype(np.int32) * nb
    n = np.abs(rel)
    max_exact = nb // 2
    large = max_exact + (np.log(np.maximum(n, 1) / max_exact) / np.log(REL_MAX_DIST / max_exact)
                         * (nb - max_exact)).astype(np.int32)
    large = np.minimum(large, nb - 1)
    return (ret + np.where(n < max_exact, n, large)).astype(np.int32)


def _band_bias(table_g, dilation, tq, w):
    h = table_g.shape[1]
    offs = np.arange(-DIL_SIDE, DIL_SIDE + 1)
    diag = jnp.take(table_g, jnp.asarray(_t5_bucket(offs * dilation)), axis=0).astype(F32).T
    diag = jnp.concatenate([diag, jnp.full((h, 1), NEG, F32)], axis=1)
    n = tq + w
    u = np.arange(n)
    delta = np.where(u <= w - 1, u, u - n)
    out = []
    for var in range(3):
        m = delta - DIL_SIDE * var
        idx = np.where(np.abs(m) <= DIL_SIDE, m + DIL_SIDE, 2 * DIL_SIDE + 1)
        v = jnp.take(diag, jnp.asarray(idx), axis=1)
        toep = jnp.tile(v, (1, tq))[:, :tq * (n - 1)].reshape(h, tq, n - 1)[:, :, :w]
        out.append(toep)
    return jnp.stack(out, axis=0)


def _dil_kernel(q_ref, k_ref, v_ref, bias_ref, o_ref, lse_ref, *, length, tq, w):
    rp = q_ref.shape[0]
    nblk = length // tq
    lane = lax.broadcasted_iota(jnp.int32, (tq, LANES), 1)

    def block(ri, c):
        if isinstance(c, int):
            q0 = c * tq
            ws = min(max(q0 - DIL_SIDE, 0), length - w)
        else:
            q0 = pl.multiple_of(c * tq, tq)
            ws = pl.multiple_of(jnp.clip(q0 - DIL_SIDE, 0, length - w), DIL_SIDE)
        var = (q0 - ws) // DIL_SIDE
        lse_blk = jnp.zeros((tq, LANES), F32)
        for hh in range(B_HEADS_PER_GROUP):
            sl = slice(hh * HEAD_DIM, (hh + 1) * HEAD_DIM)
            q = q_ref[ri, pl.ds(q0, tq), sl]
            kw = k_ref[ri, pl.ds(ws, w), sl]
            vw = v_ref[ri, pl.ds(ws, w), sl]
            s = lax.dot_general(q, kw, (((1,), (1,)), ((), ())), preferred_element_type=F32)
            s = s + bias_ref[var, hh]
            m = jnp.max(s, axis=-1, keepdims=True)
            p = jnp.exp(s - m)
            l = jnp.sum(p, axis=-1, keepdims=True)
            o = jnp.dot(p.astype(vw.dtype), vw, preferred_element_type=F32)
            o_ref[ri, pl.ds(q0, tq), sl] = (o / l).astype(o_ref.dtype)
            lse_blk = jnp.where(lane == hh, m + jnp.log(l), lse_blk)
        lse_ref[ri, pl.ds(q0, tq), :] = lse_blk

    if rp * nblk <= DIL_UNROLL:
        for ri in range(rp):
            for c in range(nblk):
                block(ri, c)
    else:
        assert rp == 1

        def body(c, carry):
            block(0, c)
            return carry

        lax.fori_loop(0, nblk, body, 0, unroll=DIL_UNROLL)


def _dilated_group(qkv, table_g, window, dilation):
    assert (window // 2) // dilation == DIL_SIDE
    batch, _, length, _ = qkv.shape
    assert length % DIL_SIDE == 0
    tq = min(DIL_TQ, length)
    w = min(length, tq + 2 * DIL_SIDE)
    assert length % tq == 0
    gw = B_HEADS_PER_GROUP * HEAD_DIM
    bias = _band_bias(table_g, dilation, tq, w)
    kern = functools.partial(_dil_kernel, length=length, tq=tq, w=w)
    rp = min(dilation, max(1, DIL_UNROLL // (length // tq)))
    assert dilation % rp == 0
    return pl.pallas_call(
        kern,
        grid=(batch, dilation // rp),
        in_specs=[
            pl.BlockSpec((None, rp, length, gw), lambda b, r: (b, r, 0, 0)),
            pl.BlockSpec((None, rp, length, gw), lambda b, r: (b, r, 0, 1)),
            pl.BlockSpec((None, rp, length, gw), lambda b, r: (b, r, 0, 2)),
            pl.BlockSpec((3, B_HEADS_PER_GROUP, tq, w), lambda b, r: (0, 0, 0, 0)),
        ],
        out_specs=[
            pl.BlockSpec((None, rp, length, gw), lambda b, r: (b, r, 0, 0)),
            pl.BlockSpec((None, rp, length, LANES), lambda b, r: (b, r, 0, 0)),
        ],
        out_shape=[
            jax.ShapeDtypeStruct((batch, dilation, length, gw), MXU_DTYPE),
            jax.ShapeDtypeStruct((batch, dilation, length, LANES), F32),
        ],
        compiler_params=_params(("arbitrary", "arbitrary")),
        name=f"attn_dilated_{dilation}",
    )(qkv, qkv, qkv, bias)


def _post_kernel(x_ref, ya_ref, o1_ref, o2_ref, o3_ref, l1_ref, l2_ref, l3_ref, ga_ref, gb_ref,
                 mod_ref, n2_ref, wa_ref, wb_ref, wo_ref, wr_ref, br_ref,
                 x1_ref, h2_ref, rt_ref, o_scr, l_scr):
    tm = x_ref.shape[0]
    lses, outs = [], []
    for gi, ((_, dil), o_ref, l_ref) in enumerate(zip(B_PAIRS, (o1_ref, o2_ref, o3_ref), (l1_ref, l2_ref, l3_ref))):
        if dil == 1:
            lses.append(l_ref[0])
            outs.append([o_ref[0, :, jh * HEAD_DIM:(jh + 1) * HEAD_DIM].astype(F32)
                         for jh in range(B_HEADS_PER_GROUP)])
            continue
        for r in range(dil):
            rows = pl.ds(r, tm // dil, stride=dil)
            l_scr[gi, rows, :] = l_ref[r]
            for jh in range(B_HEADS_PER_GROUP):
                o_scr[gi, jh, rows, :] = o_ref[r, :, jh * HEAD_DIM:(jh + 1) * HEAD_DIM].astype(F32)
        lses.append(l_scr[gi])
        outs.append([o_scr[gi, jh] for jh in range(B_HEADS_PER_GROUP)])
    parts = []
    for jh in range(B_HEADS_PER_GROUP):
        lj = [lg[:, jh:jh + 1] for lg in lses]
        mx = jnp.maximum(jnp.maximum(lj[0], lj[1]), lj[2])
        ex = [jnp.exp(v - mx) for v in lj]
        inv = 1.0 / (ex[0] + ex[1] + ex[2])
        acc = (ex[0] * inv) * outs[0][jh]
        acc = acc + (ex[1] * inv) * outs[1][jh]
        acc = acc + (ex[2] * inv) * outs[2][jh]
        parts.append(acc.astype(MXU_DTYPE))
    yb = jnp.concatenate(parts, axis=1)
    a = jnp.dot(ya_ref[...], wa_ref[...], preferred_element_type=F32)
    bb = jnp.dot(yb, wb_ref[...], preferred_element_type=F32)
    merged = ga_ref[...].astype(F32) * a + gb_ref[...].astype(F32) * bb
    out = jnp.dot(merged.astype(MXU_DTYPE), wo_ref[...], preferred_element_type=F32)
    x1 = x_ref[...] + mod_ref[2:3, :] * out
    x1_ref[...] = x1
    h2 = _rms(x1) * n2_ref[...]
    h2 = h2 * (1.0 + mod_ref[4:5, :]) + mod_ref[3:4, :]
    h2_ref[...] = _pack_bf16_pairs(h2)

    h_hi = h2.astype(MXU_DTYPE)
    h_lo = (h2 - h_hi.astype(F32)).astype(MXU_DTYPE)
    r_hi = jnp.dot(h_hi, wr_ref[...], preferred_element_type=F32)
    r_lo = jnp.dot(h_lo, wr_ref[:, :LANES], preferred_element_type=F32)
    r = r_hi[:, :LANES] + r_hi[:, LANES:] + r_lo + br_ref[...]
    lane = lax.broadcasted_iota(jnp.int32, r.shape, 1)
    lanef = lane.astype(F32)
    ninf = -jnp.inf
    gmask = lane < N_GROUPS
    gl = jnp.where(gmask, r, ninf)
    gmax = jnp.max(gl, axis=-1, keepdims=True)
    grp = jnp.min(jnp.where(gl == gmax, lanef, float(LANES)), axis=-1, keepdims=True)
    gsum = jnp.sum(jnp.where(gmask, jnp.exp(jnp.where(gmask, r, gmax) - gmax), 0.0), axis=-1, keepdims=True)
    gate = 1.0 / gsum
    lo = N_GROUPS + grp * EXPERTS_PER_GROUP
    emask = jnp.logical_and(lanef >= lo, lanef < lo + EXPERTS_PER_GROUP)
    el = jnp.where(emask, r, ninf)
    v1 = jnp.max(el, axis=-1, keepdims=True)
    i1 = jnp.min(jnp.where(el == v1, lanef, float(LANES)), axis=-1, keepdims=True)
    el2 = jnp.where(lanef == i1, ninf, el)
    v2 = jnp.max(el2, axis=-1, keepdims=True)
    i2 = jnp.min(jnp.where(el2 == v2, lanef, float(LANES)), axis=-1, keepdims=True)
    tt = jnp.exp(v2 - v1)
    w1 = 1.0 / (1.0 + tt)
    w2 = tt * w1
    rt = jnp.where(lane == 0, i1 - N_GROUPS,
                   jnp.where(lane == 1, i2 - N_GROUPS,
                             jnp.where(lane == 2, gate * w1,
                                       jnp.where(lane == 3, gate * w2, 0.0))))
    rt_ref[...] = rt


def _post(x2, ya, obs, lses, gates, mod3, boff, seq, n2, wa, wb, wo, wr, br):
    t, d = x2.shape
    tm = POST_TM
    assert t % tm == 0 and seq % tm == 0
    spb = seq // tm
    row = lambda i: (i, 0)
    const = lambda i: (0, 0)
    res = lambda i: (i // spb, 0, i % spb, 0)
    o_specs, l_specs = [], []
    for _, dil in B_PAIRS:
        assert tm % dil == 0 and (tm // dil) % 16 == 0
        o_specs.append(pl.BlockSpec((None, dil, tm // dil, B_OUT), res))
        l_specs.append(pl.BlockSpec((None, dil, tm // dil, LANES), res))
    return pl.pallas_call(
        _post_kernel,
        grid=(t // tm,),
        in_specs=[
            pl.BlockSpec((tm, d), row),
            pl.BlockSpec((tm, A_Q), row),
            *o_specs,
            *l_specs,
            pl.BlockSpec((tm, d), lambda i: (i, 0)),
            pl.BlockSpec((tm, d), lambda i: (i, 1)),
            pl.BlockSpec((None, 6, d), lambda i: (boff + i // spb, 0, 0)),
            pl.BlockSpec((1, d), const),
            pl.BlockSpec((A_Q, d), const, pipeline_mode=pl.Buffered(1)),
            pl.BlockSpec((B_OUT, d), const, pipeline_mode=pl.Buffered(1)),
            pl.BlockSpec((d, d), const, pipeline_mode=pl.Buffered(1)),
            pl.BlockSpec((d, 2 * LANES), const, pipeline_mode=pl.Buffered(1)),
            pl.BlockSpec((1, LANES), const),
        ],
        out_specs=[pl.BlockSpec((tm, d), row), pl.BlockSpec((tm, d // 2), row), pl.BlockSpec((tm, LANES), row)],
        out_shape=[jax.ShapeDtypeStruct((t, d), F32), jax.ShapeDtypeStruct((t, d // 2), jnp.uint32),
                   jax.ShapeDtypeStruct((t, LANES), F32)],
        scratch_shapes=[pltpu.VMEM((B_GROUPS, B_HEADS_PER_GROUP, tm, HEAD_DIM), F32),
                        pltpu.VMEM((B_GROUPS, tm, LANES), F32)],
        compiler_params=_params(("arbitrary",)),
        name="post_attn",
    )(x2, ya, obs[0], obs[1], obs[2], lses[0], lses[1], lses[2], gates, gates, mod3, n2.reshape(1, d),
      wa, wb, wo, wr, br)


def _slot_plan(experts):
    a = experts.shape[0]
    c = RANK_CHUNK
    assert a % c == 0 and a % MOE_TM == 0
    oh = (experts[:, None] == jnp.arange(N_EXPERTS, dtype=jnp.int32)[None, :])
    oh3 = oh.astype(jnp.bfloat16).reshape(a // c, c, N_EXPERTS)
    tri = (np.arange(c)[:, None] > np.arange(c)[None, :])
    within = jnp.einsum('ij,cjk->cik', jnp.asarray(tri, jnp.bfloat16), oh3, preferred_element_type=F32)
    tot = jnp.sum(oh3.astype(F32), axis=1)
    off = jnp.cumsum(tot, axis=0) - tot
    counts = jnp.sum(tot, axis=0).astype(jnp.int32)
    padded = (counts + MOE_TM - 1) // MOE_TM * MOE_TM
    pend = jnp.cumsum(padded)
    pstart = pend - padded
    slot = within + off[:, None, :] + pstart.astype(F32)[None, None, :]
    dest = jnp.sum(jnp.where(oh.reshape(a // c, c, N_EXPERTS), slot, 0.0), axis=-1).reshape(a).astype(jnp.int32)
    n_blocks = a // MOE_TM + N_EXPERTS
    idx = jnp.arange(n_blocks, dtype=jnp.int32)
    block_start = idx * MOE_TM
    block_e = jnp.minimum(jnp.sum((pend[None, :] <= block_start[:, None]).astype(jnp.int32), axis=1),
                          N_EXPERTS - 1)
    nb_used = pend[-1] // MOE_TM
    used = idx < nb_used
    last = nb_used - 1
    onehot_e = block_e[:, None] == jnp.arange(N_EXPERTS, dtype=jnp.int32)[None, :]
    seg_end = jnp.sum(jnp.where(onehot_e, (pstart + counts)[None, :], 0), axis=1)
    nv = jnp.where(used, jnp.clip(seg_end - block_start, 0, MOE_TM), 0).astype(jnp.int32)
    be_last = jnp.sum(jnp.where(idx == last, block_e, 0))
    block_e = jnp.where(used, block_e, be_last).astype(jnp.int32)
    f = jnp.arange(2, dtype=jnp.int32)[None, :]
    fhalf = jnp.where(used[:, None], f ^ (idx[:, None] & 1), 1 ^ (last & 1)).astype(jnp.int32)
    plan = dict(block_e=block_e, fhalf=fhalf.reshape(-1), nv=nv,
                xrow=jnp.minimum(idx, last).astype(jnp.int32))
    return dest, plan, n_blocks


def _dispatch_kernel(dest_ref, h_ref, xs_in_ref, xs_ref, sem):
    del xs_in_ref
    tm = h_ref.shape[0]

    def row_copy(t, k):
        d = dest_ref[0, TOP_K * t + k]
        return pltpu.make_async_copy(h_ref.at[pl.ds(t, 1)], xs_ref.at[pl.ds(d, 1)], sem)

    def start(t, carry):
        for k in range(TOP_K):
            row_copy(t, k).start()
        return carry

    lax.fori_loop(0, tm, start, 0, unroll=8)
    for _ in range(TOP_K):
        pltpu.make_async_copy(h_ref, xs_ref.at[pl.ds(0, tm)], sem).wait()


def _dispatch(dest_blocks, h2, xs):
    t, d = h2.shape
    tm = DISP_TM
    return pl.pallas_call(
        _dispatch_kernel,
        grid=(t // tm,),
        in_specs=[
            pl.BlockSpec((None, 1, TOP_K * tm), lambda i: (i, 0, 0), memory_space=pltpu.SMEM),
            pl.BlockSpec((tm, d), lambda i: (i, 0)),
            pl.BlockSpec(memory_space=pl.ANY),
        ],
        out_specs=pl.BlockSpec(memory_space=pl.ANY),
        out_shape=jax.ShapeDtypeStruct(xs.shape, xs.dtype),
        scratch_shapes=[pltpu.SemaphoreType.DMA(())],
        input_output_aliases={2: 0},
        compiler_params=_params(("arbitrary",)),
        name="moe_dispatch",
    )(dest_blocks, h2, xs)


def _expert_kernel(be_ref, fh_ref, nv_ref, xr_ref, x_ref, wg_ref, wu_ref, wd_ref, y_ref,
                   wg_s, wu_s, wd_s):
    del be_ref, fh_ref, xr_ref
    i = pl.program_id(0)
    f = pl.program_id(1)
    nv = nv_ref[i]
    tm = x_ref.shape[0]

    def compute(n_rows):
        wg_s[...] = wg_ref[...].astype(wg_s.dtype)
        wu_s[...] = wu_ref[...].astype(wu_s.dtype)
        wd_s[...] = wd_ref[...].astype(wd_s.dtype)
        x = _unpack_bf16_pairs(x_ref[0:n_rows, :]).astype(MXU_DTYPE)
        g = jnp.dot(x, wg_s[...], preferred_element_type=F32)
        u = jnp.dot(x, wu_s[...], preferred_element_type=F32)
        h = (g * _sigmoid(g)) * u
        part = jnp.dot(h.astype(MXU_DTYPE), wd_s[...], preferred_element_type=F32)

        @pl.when(f == 0)
        def _():
            y_ref[0:n_rows, :] = part
            if n_rows < tm:
                y_ref[n_rows:tm, :] = jnp.zeros((tm - n_rows, y_ref.shape[1]), y_ref.dtype)

        @pl.when(f != 0)
        def _():
            y_ref[0:n_rows, :] += part

    pl.when(nv > MOE_SUB)(functools.partial(compute, tm))
    pl.when(jnp.logical_and(nv > 0, nv <= MOE_SUB))(functools.partial(compute, MOE_SUB))

    @pl.when(jnp.logical_and(nv == 0, f == 0))
    def _():
        y_ref[...] = jnp.zeros_like(y_ref)


def _experts(xs, plan, wg, wu, wd):
    n_slots = xs.shape[0]
    _, d, ff = wg.shape
    tm = MOE_TM
    n_blocks = n_slots // tm
    assert ff % 2 == 0 and xs.shape[1] * 2 == d
    fh = ff // 2
    grid_spec = pltpu.PrefetchScalarGridSpec(
        num_scalar_prefetch=4,
        grid=(n_blocks, 2),
        in_specs=[
            pl.BlockSpec((tm, d // 2), lambda i, f, be, fhalf, nv, xr: (xr[i], 0)),
            pl.BlockSpec((None, d, fh), lambda i, f, be, fhalf, nv, xr: (be[i], 0, fhalf[2 * i + f])),
            pl.BlockSpec((None, d, fh), lambda i, f, be, fhalf, nv, xr: (be[i], 0, fhalf[2 * i + f])),
            pl.BlockSpec((None, fh, d), lambda i, f, be, fhalf, nv, xr: (be[i], fhalf[2 * i + f], 0)),
        ],
        out_specs=pl.BlockSpec((tm, d), lambda i, f, be, fhalf, nv, xr: (i, 0)),
        scratch_shapes=[pltpu.VMEM((d, fh), MXU_DTYPE), pltpu.VMEM((d, fh), MXU_DTYPE),
                        pltpu.VMEM((fh, d), MXU_DTYPE)],
    )
    return pl.pallas_call(
        _expert_kernel,
        grid_spec=grid_spec,
        out_shape=jax.ShapeDtypeStruct((n_slots, d), F32),
        compiler_params=_params(("arbitrary", "arbitrary")),
        name="moe_experts",
    )(plan['block_e'], plan['fhalf'], plan['nv'], plan['xrow'], xs, wg, wu, wd)


def _combine_kernel(dest_ref, dest_next_ref, x1_ref, rt_ref, mod_ref, nf_ref, ys_ref, o_ref, ybuf, sem, *, final):
    i = pl.program_id(0)
    tm = x1_ref.shape[0]
    slot = i % 2

    def issue(idx_ref, s):
        def start(t, carry):
            for k in range(TOP_K):
                d = idx_ref[0, TOP_K * t + k]
                pltpu.make_async_copy(ys_ref.at[pl.ds(d, 1)], ybuf.at[s, pl.ds(k * tm + t, 1)], sem.at[s]).start()
            return carry

        lax.fori_loop(0, tm, start, 0, unroll=8)

    @pl.when(i == 0)
    def _():
        issue(dest_ref, 0)

    @pl.when(i + 1 < pl.num_programs(0))
    def _():
        issue(dest_next_ref, 1 - slot)

    for k in range(TOP_K):
        pltpu.make_async_copy(ys_ref.at[pl.ds(0, tm)], ybuf.at[slot, pl.ds(k * tm, tm)], sem.at[slot]).wait()
    rt = rt_ref[...]
    moe = ybuf[slot, 0:tm, :] * rt[:, 2:3] + ybuf[slot, tm:2 * tm, :] * rt[:, 3:4]
    x2 = x1_ref[...] + mod_ref[5:6, :] * moe
    if final:
        x2 = _rms(x2) * nf_ref[...]
    o_ref[...] = x2


def _combine(dest_blocks, x1, rt, mod3, boff, seq, nf, ys, final):
    t, d = x1.shape
    tm = COMB_TM
    spb = seq // tm
    kern = functools.partial(_combine_kernel, final=final)
    n_steps = t // tm
    return pl.pallas_call(
        kern,
        grid=(n_steps,),
        in_specs=[
            pl.BlockSpec((None, 1, TOP_K * tm), lambda i: (i, 0, 0), memory_space=pltpu.SMEM),
            pl.BlockSpec((None, 1, TOP_K * tm), lambda i: (jnp.minimum(i + 1, n_steps - 1), 0, 0),
                         memory_space=pltpu.SMEM),
            pl.BlockSpec((tm, d), lambda i: (i, 0)),
            pl.BlockSpec((tm, LANES), lambda i: (i, 0)),
            pl.BlockSpec((None, 6, d), lambda i: (boff + i // spb, 0, 0)),
            pl.BlockSpec((1, d), lambda i: (0, 0)),
            pl.BlockSpec(memory_space=pl.ANY),
        ],
        out_specs=pl.BlockSpec((tm, d), lambda i: (i, 0)),
        out_shape=jax.ShapeDtypeStruct((t, d), F32),
        scratch_shapes=[pltpu.VMEM((2, TOP_K * tm, d), F32), pltpu.SemaphoreType.DMA((2,))],
        compiler_params=_params(("arbitrary",)),
        name="moe_combine",
    )(dest_blocks, dest_blocks, x1, rt, mod3, nf.reshape(1, d), ys)


def kernel(x_prompt, x_sample, c_prompt, c_sample, w_ada, b_ada, norm1, w_in, q_norm, k_norm, w_branch_a, w_branch_b, w_out, rel_bias, norm2, w_route_group, b_route_group, w_route_expert, b_route_expert, w_gate_exp, w_up_exp, w_down_exp, norm_final):
    depth = w_ada.shape[0]
    d = x_prompt.shape[-1]
    trunks = []
    boff = 0
    for x, c in ((x_prompt, c_prompt), (x_sample, c_sample)):
        b, s, _ = x.shape
        trunks.append(dict(x=x.reshape(b * s, d), batch=b, seq=s, boff=boff))
        boff += b
    n_cond = boff
    rows = -(-n_cond // 8) * 8
    c_all = jnp.concatenate([c_prompt, c_sample, jnp.zeros((rows - n_cond, d), F32)], axis=0)
    scale = HEAD_DIM ** -0.5

    for l in range(depth):
        mod3 = _adaln(c_all, w_ada[l], b_ada[l]).reshape(rows, 6, d)
        w_in_l = _inproj_weight(w_in[l].astype(MXU_DTYPE))
        w_partner = _rope_partner(w_in_l[:, :A_Q + 2 * A_KV])
        wa = w_branch_a[l].astype(MXU_DTYPE)
        wb = w_branch_b[l].astype(MXU_DTYPE)
        wo = w_out[l].astype(MXU_DTYPE)
        wr = jnp.concatenate([w_route_group[l], w_route_expert[l],
                              jnp.zeros((d, LANES - N_GROUPS - N_EXPERTS), F32)], axis=1)
        wr_hi = wr.astype(MXU_DTYPE)
        wr = jnp.concatenate([wr_hi, (wr - wr_hi.astype(F32)).astype(MXU_DTYPE)], axis=1)
        br = jnp.concatenate([b_route_group[l], b_route_expert[l],
                              jnp.zeros((LANES - N_GROUPS - N_EXPERTS,), F32)]).reshape(1, LANES)

        for tr in trunks:
            qa, kva, *qkv_groups, gates = _inproj(tr['x'], mod3, tr['boff'], tr['batch'], tr['seq'], norm1[l],
                                                  w_in_l, w_partner, q_norm[l] * scale, k_norm[l])
            ya = _attn_a(qa, kva, tr['batch'], tr['seq'])
            obs, lses = [], []
            for gi, (window, dilation) in enumerate(B_PAIRS):
                hs = slice(gi * B_HEADS_PER_GROUP, (gi + 1) * B_HEADS_PER_GROUP)
                o, lse = _dilated_group(qkv_groups[gi], rel_bias[:, hs], window, dilation)
                obs.append(o)
                lses.append(lse)
            tr['x1'], tr['h2'], tr['rt'] = _post(tr['x'], ya, obs, lses, gates, mod3, tr['boff'], tr['seq'],
                                                 norm2[l], wa, wb, wo, wr, br)

        experts = jnp.concatenate([tr['rt'][:, :TOP_K] for tr in trunks], axis=0).astype(jnp.int32).reshape(-1)
        dest, plan, n_blocks = _slot_plan(experts)
        xs = jnp.zeros((n_blocks * MOE_TM, d // 2), jnp.uint32)
        a0 = 0
        for tr in trunks:
            n_tok = tr['x'].shape[0]
            tr['dest'] = dest[a0:a0 + TOP_K * n_tok]
            a0 += TOP_K * n_tok
            xs = _dispatch(tr['dest'].reshape(n_tok // DISP_TM, 1, TOP_K * DISP_TM), tr['h2'], xs)
        ys = _experts(xs, plan, w_gate_exp[l], w_up_exp[l], w_down_exp[l])
        for tr in trunks:
            n_tok = tr['x'].shape[0]
            tr['x'] = _combine(tr['dest'].reshape(n_tok // COMB_TM, 1, TOP_K * COMB_TM), tr['x1'], tr['rt'],
                               mod3, tr['boff'], tr['seq'], norm_final, ys, final=(l == depth - 1))

    return tuple(tr['x'].reshape(tr['batch'], tr['seq'], d) for tr in trunks)
```

```python
import functools
import math

import numpy as np
import jax
import jax.numpy as jnp
from jax import lax
from jax.experimental import pallas as pl
from jax.experimental.pallas import tpu as pltpu

F32 = jnp.float32
MXU_DTYPE = jnp.bfloat16

HEAD_DIM = 128
A_HEADS = 8
A_KV_HEADS = 2
A_GROUP = A_HEADS // A_KV_HEADS
ROPE_THETA = 10000.0
GRID_W = 64
B_PAIRS = ((128, 1), (512, 4), (2048, 16))
B_GROUPS = len(B_PAIRS)
B_HEADS_PER_GROUP = 4
B_HEADS = B_GROUPS * B_HEADS_PER_GROUP
REL_BUCKETS = 32
REL_MAX_DIST = 1024
A_Q = A_HEADS * HEAD_DIM
A_KV = A_KV_HEADS * HEAD_DIM
B_W = B_HEADS * HEAD_DIM
B_OUT = B_HEADS_PER_GROUP * HEAD_DIM
N_GROUPS = 8
EXPERTS_PER_GROUP = 8
N_EXPERTS = N_GROUPS * EXPERTS_PER_GROUP
TOP_K = 2
EPS = 1e-6
NEG = -1e30
LANES = 128

VMEM_LIMIT_BYTES = 56 * 1024 * 1024

PROJ_TM = 1024
PROJ_TN = 512
PROJ_SUB = 256
ATT_TQ = 256
ATT_ROW_SPLIT = 2
DIL_TQ = 128
DIL_SIDE = 64
DIL_UNROLL = 4
POST_TM = 256
POST_SPLIT = 2
MOE_TM = 512
MOE_SUB = 256
MOE_PAIR = 2
EXP_FIRST = 1
EXP_LAST = 2
EXP_ZERO = 4
EXP_SLOT = 8
DISP_TM = 512
COMB_TM = 256
RANK_CHUNK = 512


def _params(sem):
    return pltpu.CompilerParams(dimension_semantics=sem, vmem_limit_bytes=VMEM_LIMIT_BYTES)


def _ada_kernel(c_ref, w_ref, b_ref, o_ref):
    c = c_ref[...]
    a = (c * jax.nn.sigmoid(c)).astype(MXU_DTYPE)
    o_ref[...] = jnp.dot(a, w_ref[...].astype(MXU_DTYPE), preferred_element_type=F32) + b_ref[...]


def _adaln(c_all, w, b):
    r, d = c_all.shape
    n = w.shape[1]
    tn = min(n, 1024)
    return pl.pallas_call(
        _ada_kernel,
        grid=(n // tn,),
        in_specs=[pl.BlockSpec((r, d), lambda j: (0, 0)),
                  pl.BlockSpec((d, tn), lambda j: (0, j)),
                  pl.BlockSpec((1, tn), lambda j: (0, j))],
        out_specs=pl.BlockSpec((r, tn), lambda j: (0, j)),
        out_shape=jax.ShapeDtypeStruct((r, n), F32),
        compiler_params=_params(("arbitrary",)),
        name="adaln",
    )(c_all, w, b.reshape(1, n))


def _rms(x):
    return x * lax.rsqrt(jnp.mean(x * x, axis=-1, keepdims=True) + EPS)


def _sigmoid(x):
    return 0.5 * jnp.tanh(0.5 * x) + 0.5


def _pack_bf16_pairs(x):
    n = x.shape[1] // 2
    lo = pltpu.bitcast(x[:, :n].astype(jnp.bfloat16).astype(F32), jnp.uint32)
    hi = pltpu.bitcast(x[:, n:].astype(jnp.bfloat16).astype(F32), jnp.uint32)
    return (hi & jnp.uint32(0xFFFF0000)) | (lo >> 16)


def _unpack_bf16_pairs(u):
    lo = pltpu.bitcast(u << 16, F32)
    hi = pltpu.bitcast(u & jnp.uint32(0xFFFF0000), F32)
    return jnp.concatenate([lo, hi], axis=1)


def _rope_partner(a):
    lead = a.shape[:-1]
    quarter = HEAD_DIM // 4
    a5 = a.reshape(*lead, a.shape[-1] // HEAD_DIM, 2, 2, quarter)
    return a5[..., ::-1, :].reshape(a.shape)


def _norm_rope(x, xp, g2, cos, sin):
    r = lax.rsqrt(jnp.mean(x * x, axis=-1, keepdims=True) + EPS)
    return (x * (cos * g2[0:1, :]) + xp * (sin * g2[1:2, :])) * r


def _inproj_kernel(x_ref, mod_ref, n1_ref, w_ref, wp_ref, cos_ref, sin_ref, qn_ref, kn_ref,
                   qa_ref, kva_ref, qkv1_ref, qkv2_ref, qkv3_ref, g_ref, h_scr, de_scr, *, j_kv, j_b, j_g):
    j = pl.program_id(1)
    tm = x_ref.shape[0]

    @pl.when(j == 0)
    def _():
        h = _rms(x_ref[...]) * n1_ref[...]
        h = h * (1.0 + mod_ref[1:2, :]) + mod_ref[0:1, :]
        h_scr[...] = h.astype(h_scr.dtype)

    subs = PROJ_TN // PROJ_SUB
    sub_heads = PROJ_SUB // HEAD_DIM

    def piece(c, ref=w_ref):
        return jnp.dot(h_scr[...], ref[:, c * PROJ_SUB:(c + 1) * PROJ_SUB], preferred_element_type=F32)

    def head(a, hh):
        return a[:, hh * HEAD_DIM:(hh + 1) * HEAD_DIM]

    @pl.when(j < j_kv)
    def _():
        for c in range(subs):
            acc = piece(c)
            accp = piece(c, wp_ref)
            for hh in range(sub_heads):
                col = (c * sub_heads + hh) * HEAD_DIM
                qa_ref[:, col:col + HEAD_DIM] = _norm_rope(
                    head(acc, hh), head(accp, hh), qn_ref[...], cos_ref[...], sin_ref[...]).astype(qa_ref.dtype)

    @pl.when(j == j_kv)
    def _():
        for c in range(subs):
            acc = piece(c)
            roped = (c + 1) * sub_heads <= A_KV_HEADS
            assert roped or c * sub_heads >= A_KV_HEADS
            accp = piece(c, wp_ref) if roped else None
            for hh in range(sub_heads):
                gh = c * sub_heads + hh
                if roped:
                    val = _norm_rope(head(acc, hh), head(accp, hh), kn_ref[...], cos_ref[...], sin_ref[...])
                else:
                    val = head(acc, hh)
                kva_ref[:, gh * HEAD_DIM:(gh + 1) * HEAD_DIM] = val.astype(kva_ref.dtype)

    for gi, ((_, dil), ref) in enumerate(zip(B_PAIRS, (qkv1_ref, qkv2_ref, qkv3_ref))):
        j0 = j_b + 3 * gi

        @pl.when(jnp.logical_and(j >= j0, j < j0 + 3))
        def _(dil=dil, ref=ref, j0=j0):
            scale = jnp.where(j == j0, HEAD_DIM ** -0.5, 1.0).astype(F32)
            for c in range(subs):
                val = piece(c) * scale
                if dil == 1:
                    ref[0, :, c * PROJ_SUB:(c + 1) * PROJ_SUB] = val.astype(ref.dtype)
                    continue
                for hh in range(sub_heads):
                    de_scr[c * sub_heads + hh] = head(val, hh)
                for r in range(dil):
                    for hh in range(sub_heads):
                        gh = c * sub_heads + hh
                        ref[r, :, gh * HEAD_DIM:(gh + 1) * HEAD_DIM] = de_scr[
                            gh, pl.ds(r, tm // dil, stride=dil), :].astype(ref.dtype)

    @pl.when(j >= j_g)
    def _():
        for c in range(subs):
            g_ref[:, c * PROJ_SUB:(c + 1) * PROJ_SUB] = _sigmoid(piece(c)).astype(g_ref.dtype)


def _rope_tables(s):
    half = HEAD_DIM // 4
    inv = np.power(ROPE_THETA, -np.arange(half, dtype=np.float64) / half)
    t = np.arange(s)
    row = (t // GRID_W).astype(np.float64)
    col = (t % GRID_W).astype(np.float64)
    ang_r = row[:, None] * inv[None, :]
    ang_c = col[:, None] * inv[None, :]
    cos = np.concatenate([np.cos(ang_r), np.cos(ang_r), np.cos(ang_c), np.cos(ang_c)], axis=1)
    sin = np.concatenate([-np.sin(ang_r), np.sin(ang_r), -np.sin(ang_c), np.sin(ang_c)], axis=1)
    return jnp.asarray(cos, F32), jnp.asarray(sin, F32)


def _inproj_weight(w_in_l):
    b0 = A_Q + 2 * A_KV
    parts = [w_in_l[:, :b0]]
    for gi in range(B_GROUPS):
        for part in range(3):
            c0 = b0 + part * B_W + gi * B_OUT
            parts.append(w_in_l[:, c0:c0 + B_OUT])
    parts.append(w_in_l[:, b0 + 3 * B_W:])
    return jnp.concatenate(parts, axis=1)


def _inproj(x2, mod3, boff, batch, seq, n1, w_in, w_partner, qn, kn):
    t, d = x2.shape
    tm, tn = PROJ_TM, PROJ_TN
    assert t % tm == 0 and seq % tm == 0
    assert A_Q % tn == 0 and 2 * A_KV == tn and B_OUT == tn and d % tn == 0
    j_kv = A_Q // tn
    j_b = j_kv + 1
    j_g = j_b + 3 * B_GROUPS
    n_j = j_g + 2 * d // tn
    assert w_in.shape == (d, n_j * tn) and w_partner.shape == (d, (j_kv + 1) * tn)
    cos, sin = _rope_tables(seq)
    spb = seq // tm
    kern = functools.partial(_inproj_kernel, j_kv=j_kv, j_b=j_b, j_g=j_g)
    grp_specs, grp_shapes = [], []
    for gi, (_, dil) in enumerate(B_PAIRS):
        assert tm % dil == 0 and (tm // dil) % 16 == 0
        j0 = j_b + 3 * gi
        grp_specs.append(pl.BlockSpec((None, dil, tm // dil, tn),
                                      lambda i, j, j0=j0: (i // spb, 0, i % spb, jnp.clip(j - j0, 0, 2))))
        grp_shapes.append(jax.ShapeDtypeStruct((batch, dil, seq // dil, 3 * tn), MXU_DTYPE))
    return pl.pallas_call(
        kern,
        grid=(t // tm, n_j),
        in_specs=[
            pl.BlockSpec((tm, d), lambda i, j: (i, 0)),
            pl.BlockSpec((None, 6, d), lambda i, j: (boff + i // spb, 0, 0)),
            pl.BlockSpec((1, d), lambda i, j: (0, 0)),
            pl.BlockSpec((d, tn), lambda i, j: (0, j)),
            pl.BlockSpec((d, tn), lambda i, j: (0, jnp.minimum(j, j_kv))),
            pl.BlockSpec((tm, HEAD_DIM), lambda i, j: (i % spb, 0)),
            pl.BlockSpec((tm, HEAD_DIM), lambda i, j: (i % spb, 0)),
            pl.BlockSpec((2, HEAD_DIM), lambda i, j: (0, 0)),
            pl.BlockSpec((2, HEAD_DIM), lambda i, j: (0, 0)),
        ],
        out_specs=[
            pl.BlockSpec((tm, tn), lambda i, j: (i, jnp.clip(j, 0, j_kv - 1))),
            pl.BlockSpec((tm, tn), lambda i, j: (i, 0)),
            *grp_specs,
            pl.BlockSpec((tm, tn), lambda i, j: (i, jnp.clip(j - j_g, 0, n_j - j_g - 1))),
        ],
        out_shape=[
            jax.ShapeDtypeStruct((t, A_Q), MXU_DTYPE),
            jax.ShapeDtypeStruct((t, 2 * A_KV), MXU_DTYPE),
            *grp_shapes,
            jax.ShapeDtypeStruct((t, 2 * d), MXU_DTYPE),
        ],
        scratch_shapes=[pltpu.VMEM((tm, d), MXU_DTYPE), pltpu.VMEM((tn // HEAD_DIM, tm, HEAD_DIM), F32)],
        compiler_params=_params(("arbitrary", "arbitrary")),
        name="inproj",
    )(x2, mod3, n1.reshape(1, d), w_in, w_partner, cos, sin,
      jnp.stack([qn, _rope_partner(qn)]), jnp.stack([kn, _rope_partner(kn)]))


def _attn_a_kernel(q_ref, k_ref, v_ref, o_ref):
    k = k_ref[...]
    v = v_ref[...]

    tq = q_ref.shape[0]
    units = [(slice(r * tq // ATT_ROW_SPLIT, (r + 1) * tq // ATT_ROW_SPLIT),
              slice(hh * HEAD_DIM, (hh + 1) * HEAD_DIM))
             for hh in range(A_GROUP) for r in range(ATT_ROW_SPLIT)]

    def scores(u):
        rows, cols = units[u]
        return lax.dot_general(q_ref[rows, cols], k, (((1,), (1,)), ((), ())), preferred_element_type=F32)

    s_next = scores(0)
    for u, (rows, cols) in enumerate(units):
        s = s_next
        m = jnp.max(s, axis=-1, keepdims=True)
        p = jnp.exp(s - m)
        l = jnp.sum(p, axis=-1, keepdims=True)
        if u + 1 < len(units):
            s_next = scores(u + 1)
        o = jnp.dot(p.astype(v.dtype), v, preferred_element_type=F32)
        o_ref[rows, cols] = (o / l).astype(o_ref.dtype)


def _attn_a(qa, kva, batch, seq):
    t = qa.shape[0]
    tq = min(ATT_TQ, seq)
    nq = seq // tq
    gw = A_GROUP * HEAD_DIM
    return pl.pallas_call(
        _attn_a_kernel,
        grid=(batch, A_KV_HEADS, nq),
        in_specs=[
            pl.BlockSpec((tq, gw), lambda b, kh, qi: (b * nq + qi, kh)),
            pl.BlockSpec((seq, HEAD_DIM), lambda b, kh, qi: (b, kh)),
            pl.BlockSpec((seq, HEAD_DIM), lambda b, kh, qi: (b, A_KV_HEADS + kh)),
        ],
        out_specs=pl.BlockSpec((tq, gw), lambda b, kh, qi: (b * nq + qi, kh)),
        out_shape=jax.ShapeDtypeStruct((t, A_Q), MXU_DTYPE),
        compiler_params=_params(("arbitrary", "arbitrary", "arbitrary")),
        name="attn_dense",
    )(qa, kva, kva)


def _t5_bucket(rel):
    nb = REL_BUCKETS // 2
    ret = (rel > 0).astype(np.int32) * nb
    n = np.abs(rel)
    max_exact = nb // 2
    large = max_exact + (np.log(np.maximum(n, 1) / max_exact) / np.log(REL_MAX_DIST / max_exact)
                         * (nb - max_exact)).astype(np.int32)
    large = np.minimum(large, nb - 1)
    return (ret + np.where(n < max_exact, n, large)).astype(np.int32)


def _band_bias(table_g, dilation, tq, w):
    h = table_g.shape[1]
    offs = np.arange(-DIL_SIDE, DIL_SIDE + 1)
    diag = jnp.take(table_g, jnp.asarray(_t5_bucket(offs * dilation)), axis=0).astype(F32).T
    diag = jnp.concatenate([diag, jnp.full((h, 1), NEG, F32)], axis=1)
    n = tq + w
    u = np.arange(n)
    delta = np.where(u <= w - 1, u, u - n)
    out = []
    for var in range(3):
        m = delta - DIL_SIDE * var
        idx = np.where(np.abs(m) <= DIL_SIDE, m + DIL_SIDE, 2 * DIL_SIDE + 1)
        v = jnp.take(diag, jnp.asarray(idx), axis=1)
        toep = jnp.tile(v, (1, tq))[:, :tq * (n - 1)].reshape(h, tq, n - 1)[:, :, :w]
        out.append(toep)
    return jnp.stack(out, axis=0)


def _dil_kernel(q_ref, k_ref, v_ref, bias_ref, o_ref, lse_ref, *, length, tq, w):
    rp = q_ref.shape[0]
    nblk = length // tq
    lane = lax.broadcasted_iota(jnp.int32, (tq, LANES), 1)

    def blocks(units):
        work = []
        for ri, c in units:
            if isinstance(c, int):
                q0 = c * tq
                ws = min(max(q0 - DIL_SIDE, 0), length - w)
            else:
                q0 = pl.multiple_of(c * tq, tq)
                ws = pl.multiple_of(jnp.clip(q0 - DIL_SIDE, 0, length - w), DIL_SIDE)
            var = (q0 - ws) // DIL_SIDE
            for hh in range(B_HEADS_PER_GROUP):
                sl = slice(hh * HEAD_DIM, (hh + 1) * HEAD_DIM)
                q = q_ref[ri, pl.ds(q0, tq), sl]
                kw = k_ref[ri, pl.ds(ws, w), sl]
                s = lax.dot_general(q, kw, (((1,), (1,)), ((), ())), preferred_element_type=F32)
                work.append(dict(ri=ri, q0=q0, ws=ws, hh=hh, sl=sl, s=s + bias_ref[var, hh]))
        for it in work:
            it['m'] = jnp.max(it['s'], axis=-1, keepdims=True)
            p = jnp.exp(it['s'] - it['m'])
            it['l'] = jnp.sum(p, axis=-1, keepdims=True)
            it['p'] = p.astype(v_ref.dtype)
        for it in work:
            vw = v_ref[it['ri'], pl.ds(it['ws'], w), it['sl']]
            o = jnp.dot(it['p'], vw, preferred_element_type=F32)
            o_ref[it['ri'], pl.ds(it['q0'], tq), it['sl']] = (o / it['l']).astype(o_ref.dtype)
        for u in range(len(units)):
            lse_blk = jnp.zeros((tq, LANES), F32)
            for it in work[u * B_HEADS_PER_GROUP:(u + 1) * B_HEADS_PER_GROUP]:
                lse_blk = jnp.where(lane == it['hh'], it['m'] + jnp.log(it['l']), lse_blk)
            lse_ref[work[u * B_HEADS_PER_GROUP]['ri'], pl.ds(work[u * B_HEADS_PER_GROUP]['q0'], tq), :] = lse_blk

    if rp * nblk <= DIL_UNROLL:
        blocks([(ri, c) for ri in range(rp) for c in range(nblk)])
    else:
        assert rp == 1 and nblk % DIL_UNROLL == 0

        def body(c4, carry):
            blocks([(0, c4 * DIL_UNROLL + j) for j in range(DIL_UNROLL)])
            return carry

        lax.fori_loop(0, nblk // DIL_UNROLL, body, 0)


def _dilated_group(qkv, table_g, window, dilation):
    assert (window // 2) // dilation == DIL_SIDE
    batch, _, length, _ = qkv.shape
    assert length % DIL_SIDE == 0
    tq = min(DIL_TQ, length)
    w = min(length, tq + 2 * DIL_SIDE)
    assert length % tq == 0
    gw = B_HEADS_PER_GROUP * HEAD_DIM
    bias = _band_bias(table_g, dilation, tq, w)
    kern = functools.partial(_dil_kernel, length=length, tq=tq, w=w)
    rp = min(dilation, max(1, DIL_UNROLL // (length // tq)))
    assert dilation % rp == 0
    return pl.pallas_call(
        kern,
        grid=(batch, dilation // rp),
        in_specs=[
            pl.BlockSpec((None, rp, length, gw), lambda b, r: (b, r, 0, 0)),
            pl.BlockSpec((None, rp, length, gw), lambda b, r: (b, r, 0, 1)),
            pl.BlockSpec((None, rp, length, gw), lambda b, r: (b, r, 0, 2)),
            pl.BlockSpec((3, B_HEADS_PER_GROUP, tq, w), lambda b, r: (0, 0, 0, 0)),
        ],
        out_specs=[
            pl.BlockSpec((None, rp, length, gw), lambda b, r: (b, r, 0, 0)),
            pl.BlockSpec((None, rp, length, LANES), lambda b, r: (b, r, 0, 0)),
        ],
        out_shape=[
            jax.ShapeDtypeStruct((batch, dilation, length, gw), MXU_DTYPE),
            jax.ShapeDtypeStruct((batch, dilation, length, LANES), F32),
        ],
        compiler_params=_params(("arbitrary", "arbitrary")),
        name=f"attn_dilated_{dilation}",
    )(qkv, qkv, qkv, bias)


def _post_kernel(x_ref, ya_ref, o1_ref, o2_ref, o3_ref, l1_ref, l2_ref, l3_ref, ga_ref, gb_ref,
                 mod_ref, n2_ref, wa_ref, wb_ref, wo_ref, wr_ref, br_ref,
                 x1_ref, h2_ref, rt_ref, o_scr, l_scr):
    tm = x_ref.shape[0]
    o_refs, l_refs = (o1_ref, o2_ref, o3_ref), (l1_ref, l2_ref, l3_ref)
    for gi, (_, dil) in enumerate(B_PAIRS):
        if dil == 1:
            continue
        for r in range(dil):
            rows = pl.ds(r, tm // dil, stride=dil)
            l_scr[gi, rows, :] = l_refs[gi][r]
            for jh in range(B_HEADS_PER_GROUP):
                o_scr[gi, jh, rows, :] = o_refs[gi][r, :, jh * HEAD_DIM:(jh + 1) * HEAD_DIM].astype(F32)

    def lse_rows(gi, rs):
        return l_refs[gi][0, rs, :] if B_PAIRS[gi][1] == 1 else l_scr[gi, rs, :]

    def out_rows(gi, jh, rs):
        if B_PAIRS[gi][1] == 1:
            return o_refs[gi][0, rs, jh * HEAD_DIM:(jh + 1) * HEAD_DIM].astype(F32)
        return o_scr[gi, jh, rs, :]

    groups = [dict(rs=slice(g * tm // POST_SPLIT, (g + 1) * tm // POST_SPLIT)) for g in range(POST_SPLIT)]
    for it in groups:
        rs = it['rs']
        lses = [lse_rows(gi, rs) for gi in range(B_GROUPS)]
        parts = []
        for jh in range(B_HEADS_PER_GROUP):
            lj = [lg[:, jh:jh + 1] for lg in lses]
            mx = jnp.maximum(jnp.maximum(lj[0], lj[1]), lj[2])
            ex = [jnp.exp(v - mx) for v in lj]
            inv = 1.0 / (ex[0] + ex[1] + ex[2])
            acc = (ex[0] * inv) * out_rows(0, jh, rs)
            acc = acc + (ex[1] * inv) * out_rows(1, jh, rs)
            acc = acc + (ex[2] * inv) * out_rows(2, jh, rs)
            parts.append(acc.astype(MXU_DTYPE))
        it['yb'] = jnp.concatenate(parts, axis=1)
    for it in groups:
        it['a'] = jnp.dot(ya_ref[it['rs'], :], wa_ref[...], preferred_element_type=F32)
        it['bb'] = jnp.dot(it['yb'], wb_ref[...], preferred_element_type=F32)
    for it in groups:
        rs = it['rs']
        merged = ga_ref[rs, :].astype(F32) * it['a'] + gb_ref[rs, :].astype(F32) * it['bb']
        it['merged'] = merged.astype(MXU_DTYPE)
    for it in groups:
        it['out'] = jnp.dot(it['merged'], wo_ref[...], preferred_element_type=F32)
    for it in groups:
        rs = it['rs']
        x1 = x_ref[rs, :] + mod_ref[2:3, :] * it['out']
        x1_ref[rs, :] = x1
        h2 = _rms(x1) * n2_ref[...]
        h2 = h2 * (1.0 + mod_ref[4:5, :]) + mod_ref[3:4, :]
        h2_ref[rs, :] = _pack_bf16_pairs(h2)
        it['h_hi'] = h2.astype(MXU_DTYPE)
        it['h_lo'] = (h2 - it['h_hi'].astype(F32)).astype(MXU_DTYPE)
    for it in groups:
        r_hi = jnp.dot(it['h_hi'], wr_ref[...], preferred_element_type=F32)
        r_lo = jnp.dot(it['h_lo'], wr_ref[:, :LANES], preferred_element_type=F32)
        it['r'] = r_hi[:, :LANES] + r_hi[:, LANES:] + r_lo + br_ref[...]
    for it in groups:
        rt_ref[it['rs'], :] = _route(it['r'])


def _route(r):
    lane = lax.broadcasted_iota(jnp.int32, r.shape, 1)
    lanef = lane.astype(F32)
    ninf = -jnp.inf
    gmask = lane < N_GROUPS
    gl = jnp.where(gmask, r, ninf)
    gmax = jnp.max(gl, axis=-1, keepdims=True)
    grp = jnp.min(jnp.where(gl == gmax, lanef, float(LANES)), axis=-1, keepdims=True)
    gsum = jnp.sum(jnp.where(gmask, jnp.exp(jnp.where(gmask, r, gmax) - gmax), 0.0), axis=-1, keepdims=True)
    gate = 1.0 / gsum
    lo = N_GROUPS + grp * EXPERTS_PER_GROUP
    emask = jnp.logical_and(lanef >= lo, lanef < lo + EXPERTS_PER_GROUP)
    el = jnp.where(emask, r, ninf)
    v1 = jnp.max(el, axis=-1, keepdims=True)
    i1 = jnp.min(jnp.where(el == v1, lanef, float(LANES)), axis=-1, keepdims=True)
    el2 = jnp.where(lanef == i1, ninf, el)
    v2 = jnp.max(el2, axis=-1, keepdims=True)
    i2 = jnp.min(jnp.where(el2 == v2, lanef, float(LANES)), axis=-1, keepdims=True)
    tt = jnp.exp(v2 - v1)
    w1 = 1.0 / (1.0 + tt)
    w2 = tt * w1
    return jnp.where(lane == 0, i1 - N_GROUPS,
                     jnp.where(lane == 1, i2 - N_GROUPS,
                               jnp.where(lane == 2, gate * w1,
                                         jnp.where(lane == 3, gate * w2, 0.0))))


def _post(x2, ya, obs, lses, gates, mod3, boff, seq, n2, wa, wb, wo, wr, br):
    t, d = x2.shape
    tm = POST_TM
    assert t % tm == 0 and seq % tm == 0
    spb = seq // tm
    row = lambda i: (i, 0)
    const = lambda i: (0, 0)
    res = lambda i: (i // spb, 0, i % spb, 0)
    o_specs, l_specs = [], []
    for _, dil in B_PAIRS:
        assert tm % dil == 0 and (tm // dil) % 16 == 0
        o_specs.append(pl.BlockSpec((None, dil, tm // dil, B_OUT), res))
        l_specs.append(pl.BlockSpec((None, dil, tm // dil, LANES), res))
    return pl.pallas_call(
        _post_kernel,
        grid=(t // tm,),
        in_specs=[
            pl.BlockSpec((tm, d), row),
            pl.BlockSpec((tm, A_Q), row),
            *o_specs,
            *l_specs,
            pl.BlockSpec((tm, d), lambda i: (i, 0)),
            pl.BlockSpec((tm, d), lambda i: (i, 1)),
            pl.BlockSpec((None, 6, d), lambda i: (boff + i // spb, 0, 0)),
            pl.BlockSpec((1, d), const),
            pl.BlockSpec((A_Q, d), const, pipeline_mode=pl.Buffered(1)),
            pl.BlockSpec((B_OUT, d), const, pipeline_mode=pl.Buffered(1)),
            pl.BlockSpec((d, d), const, pipeline_mode=pl.Buffered(1)),
            pl.BlockSpec((d, 2 * LANES), const, pipeline_mode=pl.Buffered(1)),
            pl.BlockSpec((1, LANES), const),
        ],
        out_specs=[pl.BlockSpec((tm, d), row), pl.BlockSpec((tm, d // 2), row), pl.BlockSpec((tm, LANES), row)],
        out_shape=[jax.ShapeDtypeStruct((t, d), F32), jax.ShapeDtypeStruct((t, d // 2), jnp.uint32),
                   jax.ShapeDtypeStruct((t, LANES), F32)],
        scratch_shapes=[pltpu.VMEM((B_GROUPS, B_HEADS_PER_GROUP, tm, HEAD_DIM), F32),
                        pltpu.VMEM((B_GROUPS, tm, LANES), F32)],
        compiler_params=_params(("arbitrary",)),
        name="post_attn",
    )(x2, ya, obs[0], obs[1], obs[2], lses[0], lses[1], lses[2], gates, gates, mod3, n2.reshape(1, d),
      wa, wb, wo, wr, br)


def _slot_plan(experts):
    a = experts.shape[0]
    c = RANK_CHUNK
    assert a % c == 0 and a % MOE_TM == 0
    oh = (experts[:, None] == jnp.arange(N_EXPERTS, dtype=jnp.int32)[None, :])
    oh3 = oh.astype(jnp.bfloat16).reshape(a // c, c, N_EXPERTS)
    tri = (np.arange(c)[:, None] > np.arange(c)[None, :])
    within = jnp.einsum('ij,cjk->cik', jnp.asarray(tri, jnp.bfloat16), oh3, preferred_element_type=F32)
    tot = jnp.sum(oh3.astype(F32), axis=1)
    off = jnp.cumsum(tot, axis=0) - tot
    counts = jnp.sum(tot, axis=0).astype(jnp.int32)
    padded = (counts + MOE_TM - 1) // MOE_TM * MOE_TM
    pend = jnp.cumsum(padded)
    pstart = pend - padded
    slot = within + off[:, None, :] + pstart.astype(F32)[None, None, :]
    dest = jnp.sum(jnp.where(oh.reshape(a // c, c, N_EXPERTS), slot, 0.0), axis=-1).reshape(a).astype(jnp.int32)
    n_blocks = a // MOE_TM + N_EXPERTS
    idx = jnp.arange(n_blocks, dtype=jnp.int32)
    block_start = idx * MOE_TM
    block_e = jnp.minimum(jnp.sum((pend[None, :] <= block_start[:, None]).astype(jnp.int32), axis=1),
                          N_EXPERTS - 1)
    nb_used = pend[-1] // MOE_TM
    used = idx < nb_used
    last = nb_used - 1
    onehot_e = block_e[:, None] == jnp.arange(N_EXPERTS, dtype=jnp.int32)[None, :]
    seg_end = jnp.sum(jnp.where(onehot_e, (pstart + counts)[None, :], 0), axis=1)
    nv = jnp.where(used, jnp.clip(seg_end - block_start, 0, MOE_TM), 0).astype(jnp.int32)
    def at_expert(table):
        return jnp.sum(jnp.where(onehot_e, table[None, :], 0), axis=1)

    nblk_e = padded // MOE_TM
    pos_in_e = idx - at_expert(pstart // MOE_TM)
    local = pos_in_e % MOE_PAIR
    pair_size = jnp.minimum(MOE_PAIR, at_expert(nblk_e) - (pos_in_e - local))
    pairs_e = (nblk_e + MOE_PAIR - 1) // MOE_PAIR
    pair_id = at_expert(jnp.cumsum(pairs_e) - pairs_e) + pos_in_e // MOE_PAIR
    f_first = pair_id & 1
    base = 2 * (idx - local)
    pos_a = jnp.where(used, base + local, 2 * idx)
    pos_b = jnp.where(used, base + jnp.where(pair_size == 1, 1, 3 - local), 2 * idx + 1)
    be_last = jnp.sum(jnp.where(idx == last, block_e, 0))
    e_col = jnp.where(used, block_e, be_last)
    x_col = jnp.minimum(idx, last)
    slot_col = jnp.where(used, local, 0)
    o_first = jnp.where(jnp.logical_and(used, pair_size == MOE_PAIR), idx - local + 1, idx)
    zeros = jnp.zeros_like(idx)
    flag_a = jnp.where(used, EXP_FIRST, EXP_ZERO) + EXP_SLOT * slot_col
    flag_b = jnp.where(used, EXP_LAST, 0) + EXP_SLOT * slot_col
    rows_a = jnp.stack([x_col, e_col, jnp.where(used, f_first, zeros), nv, flag_a, o_first], axis=1)
    rows_b = jnp.stack([x_col, e_col, jnp.where(used, 1 - f_first, zeros), nv, flag_b, idx], axis=1)
    steps = jnp.zeros((2 * n_blocks, 6), jnp.int32).at[jnp.concatenate([pos_a, pos_b])].set(
        jnp.concatenate([rows_a, rows_b]).astype(jnp.int32))
    step_idx = jnp.arange(2 * n_blocks, dtype=jnp.int32)
    f_last = jnp.sum(jnp.where(step_idx == 2 * nb_used - 1, steps[:, 2], 0))
    f_col = jnp.where(step_idx < 2 * nb_used, steps[:, 2], f_last)
    plan = dict(xblk=steps[:, 0], expert=steps[:, 1], fhalf=f_col, nv=steps[:, 3], flags=steps[:, 4],
                oblk=steps[:, 5])
    return dest, plan, n_blocks


def _dispatch_kernel(dest_ref, h_ref, xs_in_ref, xs_ref, sem):
    del xs_in_ref
    tm = h_ref.shape[0]

    def row_copy(t, k):
        d = dest_ref[0, TOP_K * t + k]
        return pltpu.make_async_copy(h_ref.at[pl.ds(t, 1)], xs_ref.at[pl.ds(d, 1)], sem)

    def start(t, carry):
        for k in range(TOP_K):
            row_copy(t, k).start()
        return carry

    lax.fori_loop(0, tm, start, 0, unroll=8)
    for _ in range(TOP_K):
        pltpu.make_async_copy(h_ref, xs_ref.at[pl.ds(0, tm)], sem).wait()


def _dispatch(dest_blocks, h2, xs):
    t, d = h2.shape
    tm = DISP_TM
    return pl.pallas_call(
        _dispatch_kernel,
        grid=(t // tm,),
        in_specs=[
            pl.BlockSpec((None, 1, TOP_K * tm), lambda i: (i, 0, 0), memory_space=pltpu.SMEM),
            pl.BlockSpec((tm, d), lambda i: (i, 0)),
            pl.BlockSpec(memory_space=pl.ANY),
        ],
        out_specs=pl.BlockSpec(memory_space=pl.ANY),
        out_shape=jax.ShapeDtypeStruct(xs.shape, xs.dtype),
        scratch_shapes=[pltpu.SemaphoreType.DMA(())],
        input_output_aliases={2: 0},
        compiler_params=_params(("arbitrary",)),
        name="moe_dispatch",
    )(dest_blocks, h2, xs)


def _expert_kernel(xb_ref, e_ref, fh_ref, nv_ref, fl_ref, ob_ref, x_ref, wg_ref, wu_ref, wd_ref, y_ref,
                   wg_s, wu_s, wd_s, acc):
    del xb_ref, e_ref, fh_ref, ob_ref
    s = pl.program_id(0)
    nv = nv_ref[s]
    flags = fl_ref[s]
    first = (flags & EXP_FIRST) > 0
    last = (flags & EXP_LAST) > 0
    slot = (flags // EXP_SLOT) & 1
    tm = x_ref.shape[0]

    def compute(n_rows):
        wg_s[...] = wg_ref[...].astype(wg_s.dtype)
        wu_s[...] = wu_ref[...].astype(wu_s.dtype)
        wd_s[...] = wd_ref[...].astype(wd_s.dtype)
        x = _unpack_bf16_pairs(x_ref[0:n_rows, :]).astype(MXU_DTYPE)
        g = jnp.dot(x, wg_s[...], preferred_element_type=F32)
        u = jnp.dot(x, wu_s[...], preferred_element_type=F32)
        h = (g * _sigmoid(g)) * u
        part = jnp.dot(h.astype(MXU_DTYPE), wd_s[...], preferred_element_type=F32)

        @pl.when(first)
        def _():
            acc[slot, 0:n_rows, :] = part

        @pl.when(last)
        def _():
            y_ref[0:n_rows, :] = _pack_bf16_pairs(acc[slot, 0:n_rows, :] + part)
            if n_rows < tm:
                y_ref[n_rows:tm, :] = jnp.zeros((tm - n_rows, y_ref.shape[1]), y_ref.dtype)

    pl.when(nv > MOE_SUB)(functools.partial(compute, tm))
    pl.when(jnp.logical_and(nv > 0, nv <= MOE_SUB))(functools.partial(compute, MOE_SUB))

    @pl.when((flags & EXP_ZERO) > 0)
    def _():
        y_ref[...] = jnp.zeros_like(y_ref)


def _experts(xs, plan, wg, wu, wd):
    n_slots = xs.shape[0]
    _, d, ff = wg.shape
    tm = MOE_TM
    n_blocks = n_slots // tm
    assert ff % 2 == 0 and xs.shape[1] * 2 == d and MOE_PAIR == 2
    fh = ff // 2
    grid_spec = pltpu.PrefetchScalarGridSpec(
        num_scalar_prefetch=6,
        grid=(2 * n_blocks,),
        in_specs=[
            pl.BlockSpec((tm, d // 2), lambda s, xb, e, fhalf, nv, fl, ob: (xb[s], 0)),
            pl.BlockSpec((None, d, fh), lambda s, xb, e, fhalf, nv, fl, ob: (e[s], 0, fhalf[s])),
            pl.BlockSpec((None, d, fh), lambda s, xb, e, fhalf, nv, fl, ob: (e[s], 0, fhalf[s])),
            pl.BlockSpec((None, fh, d), lambda s, xb, e, fhalf, nv, fl, ob: (e[s], fhalf[s], 0)),
        ],
        out_specs=pl.BlockSpec((tm, d // 2), lambda s, xb, e, fhalf, nv, fl, ob: (ob[s], 0)),
        scratch_shapes=[pltpu.VMEM((d, fh), MXU_DTYPE), pltpu.VMEM((d, fh), MXU_DTYPE),
                        pltpu.VMEM((fh, d), MXU_DTYPE), pltpu.VMEM((MOE_PAIR, tm, d), F32)],
    )
    return pl.pallas_call(
        _expert_kernel,
        grid_spec=grid_spec,
        out_shape=jax.ShapeDtypeStruct((n_slots, d // 2), jnp.uint32),
        compiler_params=_params(("arbitrary",)),
        name="moe_experts",
    )(plan['xblk'], plan['expert'], plan['fhalf'], plan['nv'], plan['flags'], plan['oblk'], xs, wg, wu, wd)


def _combine_kernel(dest_ref, dest_next_ref, x1_ref, rt_ref, mod_ref, nf_ref, ys_ref, o_ref, ybuf, sem, *, final):
    i = pl.program_id(0)
    tm = x1_ref.shape[0]
    slot = i % 2

    def issue(idx_ref, s):
        def start(t, carry):
            for k in range(TOP_K):
                d = idx_ref[0, TOP_K * t + k]
                pltpu.make_async_copy(ys_ref.at[pl.ds(d, 1)], ybuf.at[s, pl.ds(k * tm + t, 1)], sem.at[s]).start()
            return carry

        lax.fori_loop(0, tm, start, 0, unroll=8)

    @pl.when(i == 0)
    def _():
        issue(dest_ref, 0)

    @pl.when(i + 1 < pl.num_programs(0))
    def _():
        issue(dest_next_ref, 1 - slot)

    for k in range(TOP_K):
        pltpu.make_async_copy(ys_ref.at[pl.ds(0, tm)], ybuf.at[slot, pl.ds(k * tm, tm)], sem.at[slot]).wait()
    rt = rt_ref[...]
    moe = (_unpack_bf16_pairs(ybuf[slot, 0:tm, :]) * rt[:, 2:3]
           + _unpack_bf16_pairs(ybuf[slot, tm:2 * tm, :]) * rt[:, 3:4])
    x2 = x1_ref[...] + mod_ref[5:6, :] * moe
    if final:
        x2 = _rms(x2) * nf_ref[...]
    o_ref[...] = x2


def _combine(dest_blocks, x1, rt, mod3, boff, seq, nf, ys, final):
    t, d = x1.shape
    tm = COMB_TM
    spb = seq // tm
    kern = functools.partial(_combine_kernel, final=final)
    n_steps = t // tm
    return pl.pallas_call(
        kern,
        grid=(n_steps,),
        in_specs=[
            pl.BlockSpec((None, 1, TOP_K * tm), lambda i: (i, 0, 0), memory_space=pltpu.SMEM),
            pl.BlockSpec((None, 1, TOP_K * tm), lambda i: (jnp.minimum(i + 1, n_steps - 1), 0, 0),
                         memory_space=pltpu.SMEM),
            pl.BlockSpec((tm, d), lambda i: (i, 0)),
            pl.BlockSpec((tm, LANES), lambda i: (i, 0)),
            pl.BlockSpec((None, 6, d), lambda i: (boff + i // spb, 0, 0)),
            pl.BlockSpec((1, d), lambda i: (0, 0)),
            pl.BlockSpec(memory_space=pl.ANY),
        ],
        out_specs=pl.BlockSpec((tm, d), lambda i: (i, 0)),
        out_shape=jax.ShapeDtypeStruct((t, d), F32),
        scratch_shapes=[pltpu.VMEM((2, TOP_K * tm, d // 2), jnp.uint32), pltpu.SemaphoreType.DMA((2,))],
        compiler_params=_params(("arbitrary",)),
        name="moe_combine",
    )(dest_blocks, dest_blocks, x1, rt, mod3, nf.reshape(1, d), ys)


def kernel(x_prompt, x_sample, c_prompt, c_sample, w_ada, b_ada, norm1, w_in, q_norm, k_norm, w_branch_a, w_branch_b, w_out, rel_bias, norm2, w_route_group, b_route_group, w_route_expert, b_route_expert, w_gate_exp, w_up_exp, w_down_exp, norm_final):
    depth = w_ada.shape[0]
    d = x_prompt.shape[-1]
    trunks = []
    boff = 0
    for x, c in ((x_prompt, c_prompt), (x_sample, c_sample)):
        b, s, _ = x.shape
        trunks.append(dict(x=x.reshape(b * s, d), batch=b, seq=s, boff=boff))
        boff += b
    n_cond = boff
    rows = -(-n_cond // 8) * 8
    c_all = jnp.concatenate([c_prompt, c_sample, jnp.zeros((rows - n_cond, d), F32)], axis=0)
    scale = HEAD_DIM ** -0.5

    for l in range(depth):
        mod3 = _adaln(c_all, w_ada[l], b_ada[l]).reshape(rows, 6, d)
        w_in_l = _inproj_weight(w_in[l].astype(MXU_DTYPE))
        w_partner = _rope_partner(w_in_l[:, :A_Q + 2 * A_KV])
        wa = w_branch_a[l].astype(MXU_DTYPE)
        wb = w_branch_b[l].astype(MXU_DTYPE)
        wo = w_out[l].astype(MXU_DTYPE)
        wr = jnp.concatenate([w_route_group[l], w_route_expert[l],
                              jnp.zeros((d, LANES - N_GROUPS - N_EXPERTS), F32)], axis=1)
        wr_hi = wr.astype(MXU_DTYPE)
        wr = jnp.concatenate([wr_hi, (wr - wr_hi.astype(F32)).astype(MXU_DTYPE)], axis=1)
        br = jnp.concatenate([b_route_group[l], b_route_expert[l],
                              jnp.zeros((LANES - N_GROUPS - N_EXPERTS,), F32)]).reshape(1, LANES)

        for tr in trunks:
            qa, kva, *qkv_groups, gates = _inproj(tr['x'], mod3, tr['boff'], tr['batch'], tr['seq'], norm1[l],
                                                  w_in_l, w_partner, q_norm[l] * scale, k_norm[l])
            ya = _attn_a(qa, kva, tr['batch'], tr['seq'])
            obs, lses = [], []
            for gi, (window, dilation) in enumerate(B_PAIRS):
                hs = slice(gi * B_HEADS_PER_GROUP, (gi + 1) * B_HEADS_PER_GROUP)
                o, lse = _dilated_group(qkv_groups[gi], rel_bias[:, hs], window, dilation)
                obs.append(o)
                lses.append(lse)
            tr['x1'], tr['h2'], tr['rt'] = _post(tr['x'], ya, obs, lses, gates, mod3, tr['boff'], tr['seq'],
                                                 norm2[l], wa, wb, wo, wr, br)

        experts = jnp.concatenate([tr['rt'][:, :TOP_K] for tr in trunks], axis=0).astype(jnp.int32).reshape(-1)
        dest, plan, n_blocks = _slot_plan(experts)
        xs = jnp.zeros((n_blocks * MOE_TM, d // 2), jnp.uint32)
        a0 = 0
        for tr in trunks:
            n_tok = tr['x'].shape[0]
            tr['dest'] = dest[a0:a0 + TOP_K * n_tok]
            a0 += TOP_K * n_tok
            xs = _dispatch(tr['dest'].reshape(n_tok // DISP_TM, 1, TOP_K * DISP_TM), tr['h2'], xs)
        ys = _experts(xs, plan, w_gate_exp[l], w_up_exp[l], w_down_exp[l])
        for tr in trunks:
            n_tok = tr['x'].shape[0]
            tr['x'] = _combine(tr['dest'].reshape(n_tok // COMB_TM, 1, TOP_K * COMB_TM), tr['x1'], tr['rt'],
                               mod3, tr['boff'], tr['seq'], norm_final, ys, final=(l == depth - 1))

    return tuple(tr['x'].reshape(tr['batch'], tr['seq'], d) for tr in trunks)
```

```python
import functools
import math

import numpy as np
import jax
import jax.numpy as jnp
from jax import lax
from jax.experimental import pallas as pl
from jax.experimental.pallas import tpu as pltpu

F32 = jnp.float32
MXU_DTYPE = jnp.bfloat16

HEAD_DIM = 128
A_HEADS = 8
A_KV_HEADS = 2
A_GROUP = A_HEADS // A_KV_HEADS
ROPE_THETA = 10000.0
GRID_W = 64
B_PAIRS = ((128, 1), (512, 4), (2048, 16))
B_GROUPS = len(B_PAIRS)
B_HEADS_PER_GROUP = 4
B_HEADS = B_GROUPS * B_HEADS_PER_GROUP
REL_BUCKETS = 32
REL_MAX_DIST = 1024
A_Q = A_HEADS * HEAD_DIM
A_KV = A_KV_HEADS * HEAD_DIM
B_W = B_HEADS * HEAD_DIM
B_OUT = B_HEADS_PER_GROUP * HEAD_DIM
N_GROUPS = 8
EXPERTS_PER_GROUP = 8
N_EXPERTS = N_GROUPS * EXPERTS_PER_GROUP
TOP_K = 2
EPS = 1e-6
NEG = -1e30
LANES = 128

VMEM_LIMIT_BYTES = 56 * 1024 * 1024

PROJ_TM = 1024
PROJ_TN = 512
PROJ_SUB = 256
ATT_TQ = 512
ATT_ROW_SPLIT = 4
DIL_TQ = 128
DIL_SIDE = 64
DIL_UNROLL = 4
POST_TM = 256
POST_SPLIT = 2
MOE_TM = 512
MOE_SUB = 256
MOE_PAIR = 2
EXP_FIRST = 1
EXP_LAST = 2
EXP_ZERO = 4
EXP_SLOT = 8
DISP_TM = 512
COMB_TM = 256
RANK_CHUNK = 512


def _params(sem):
    return pltpu.CompilerParams(dimension_semantics=sem, vmem_limit_bytes=VMEM_LIMIT_BYTES)


def _ada_kernel(c_ref, w_ref, b_ref, o_ref):
    c = c_ref[...]
    a = (c * jax.nn.sigmoid(c)).astype(MXU_DTYPE)
    o_ref[...] = jnp.dot(a, w_ref[...].astype(MXU_DTYPE), preferred_element_type=F32) + b_ref[...]


def _adaln(c_all, w, b):
    r, d = c_all.shape
    n = w.shape[1]
    tn = min(n, 1024)
    return pl.pallas_call(
        _ada_kernel,
        grid=(n // tn,),
        in_specs=[pl.BlockSpec((r, d), lambda j: (0, 0)),
                  pl.BlockSpec((d, tn), lambda j: (0, j)),
                  pl.BlockSpec((1, tn), lambda j: (0, j))],
        out_specs=pl.BlockSpec((r, tn), lambda j: (0, j)),
        out_shape=jax.ShapeDtypeStruct((r, n), F32),
        compiler_params=_params(("arbitrary",)),
        name="adaln",
    )(c_all, w, b.reshape(1, n))


def _rms(x):
    return x * lax.rsqrt(jnp.mean(x * x, axis=-1, keepdims=True) + EPS)


def _sigmoid(x):
    return 0.5 * jnp.tanh(0.5 * x) + 0.5


def _pack_bf16_pairs(x):
    n = x.shape[1] // 2
    lo = pltpu.bitcast(x[:, :n].astype(jnp.bfloat16).astype(F32), jnp.uint32)
    hi = pltpu.bitcast(x[:, n:].astype(jnp.bfloat16).astype(F32), jnp.uint32)
    return (hi & jnp.uint32(0xFFFF0000)) | (lo >> 16)


def _unpack_bf16_pairs(u):
    lo = pltpu.bitcast(u << 16, F32)
    hi = pltpu.bitcast(u & jnp.uint32(0xFFFF0000), F32)
    return jnp.concatenate([lo, hi], axis=1)


def _rope_partner(a):
    lead = a.shape[:-1]
    quarter = HEAD_DIM // 4
    a5 = a.reshape(*lead, a.shape[-1] // HEAD_DIM, 2, 2, quarter)
    return a5[..., ::-1, :].reshape(a.shape)


def _norm_rope(x, xp, g2, cos, sin):
    r = lax.rsqrt(jnp.mean(x * x, axis=-1, keepdims=True) + EPS)
    return (x * (cos * g2[0:1, :]) + xp * (sin * g2[1:2, :])) * r


def _inproj_kernel(x_ref, mod_ref, n1_ref, w_ref, wp_ref, cos_ref, sin_ref, qn_ref, kn_ref,
                   qa_ref, kva_ref, qkv1_ref, qkv2_ref, qkv3_ref, g_ref, h_scr, de_scr, *, j_kv, j_b, j_g):
    j = pl.program_id(1)
    tm = x_ref.shape[0]

    @pl.when(j == 0)
    def _():
        h = _rms(x_ref[...]) * n1_ref[...]
        h = h * (1.0 + mod_ref[1:2, :]) + mod_ref[0:1, :]
        h_scr[...] = h.astype(h_scr.dtype)

    subs = PROJ_TN // PROJ_SUB
    sub_heads = PROJ_SUB // HEAD_DIM

    def piece(c, ref=w_ref):
        return jnp.dot(h_scr[...], ref[:, c * PROJ_SUB:(c + 1) * PROJ_SUB], preferred_element_type=F32)

    def head(a, hh):
        return a[:, hh * HEAD_DIM:(hh + 1) * HEAD_DIM]

    @pl.when(j < j_kv)
    def _():
        for c in range(subs):
            acc = piece(c)
            accp = piece(c, wp_ref)
            for hh in range(sub_heads):
                col = (c * sub_heads + hh) * HEAD_DIM
                qa_ref[:, col:col + HEAD_DIM] = _norm_rope(
                    head(acc, hh), head(accp, hh), qn_ref[...], cos_ref[...], sin_ref[...]).astype(qa_ref.dtype)

    @pl.when(j == j_kv)
    def _():
        for c in range(subs):
            acc = piece(c)
            roped = (c + 1) * sub_heads <= A_KV_HEADS
            assert roped or c * sub_heads >= A_KV_HEADS
            accp = piece(c, wp_ref) if roped else None
            for hh in range(sub_heads):
                gh = c * sub_heads + hh
                if roped:
                    val = _norm_rope(head(acc, hh), head(accp, hh), kn_ref[...], cos_ref[...], sin_ref[...])
                else:
                    val = head(acc, hh)
                kva_ref[:, gh * HEAD_DIM:(gh + 1) * HEAD_DIM] = val.astype(kva_ref.dtype)

    for gi, ((_, dil), ref) in enumerate(zip(B_PAIRS, (qkv1_ref, qkv2_ref, qkv3_ref))):
        j0 = j_b + 3 * gi

        @pl.when(jnp.logical_and(j >= j0, j < j0 + 3))
        def _(dil=dil, ref=ref, j0=j0):
            scale = jnp.where(j == j0, HEAD_DIM ** -0.5, 1.0).astype(F32)
            for c in range(subs):
                val = piece(c) * scale
                if dil == 1:
                    ref[0, :, c * PROJ_SUB:(c + 1) * PROJ_SUB] = val.astype(ref.dtype)
                    continue
                for hh in range(sub_heads):
                    de_scr[c * sub_heads + hh] = head(val, hh)
                for r in range(dil):
                    for hh in range(sub_heads):
                        gh = c * sub_heads + hh
                        ref[r, :, gh * HEAD_DIM:(gh + 1) * HEAD_DIM] = de_scr[
                            gh, pl.ds(r, tm // dil, stride=dil), :].astype(ref.dtype)

    @pl.when(j >= j_g)
    def _():
        for c in range(subs):
            g_ref[:, c * PROJ_SUB:(c + 1) * PROJ_SUB] = _sigmoid(piece(c)).astype(g_ref.dtype)


def _rope_tables(s):
    half = HEAD_DIM // 4
    inv = np.power(ROPE_THETA, -np.arange(half, dtype=np.float64) / half)
    t = np.arange(s)
    row = (t // GRID_W).astype(np.float64)
    col = (t % GRID_W).astype(np.float64)
    ang_r = row[:, None] * inv[None, :]
    ang_c = col[:, None] * inv[None, :]
    cos = np.concatenate([np.cos(ang_r), np.cos(ang_r), np.cos(ang_c), np.cos(ang_c)], axis=1)
    sin = np.concatenate([-np.sin(ang_r), np.sin(ang_r), -np.sin(ang_c), np.sin(ang_c)], axis=1)
    return jnp.asarray(cos, F32), jnp.asarray(sin, F32)


def _inproj_weight(w_in_l):
    b0 = A_Q + 2 * A_KV
    parts = [w_in_l[:, :b0]]
    for gi in range(B_GROUPS):
        for part in range(3):
            c0 = b0 + part * B_W + gi * B_OUT
            parts.append(w_in_l[:, c0:c0 + B_OUT])
    parts.append(w_in_l[:, b0 + 3 * B_W:])
    return jnp.concatenate(parts, axis=1)


def _inproj(x2, mod3, boff, batch, seq, n1, w_in, w_partner, qn, kn):
    t, d = x2.shape
    tm, tn = PROJ_TM, PROJ_TN
    assert t % tm == 0 and seq % tm == 0
    assert A_Q % tn == 0 and 2 * A_KV == tn and B_OUT == tn and d % tn == 0
    j_kv = A_Q // tn
    j_b = j_kv + 1
    j_g = j_b + 3 * B_GROUPS
    n_j = j_g + 2 * d // tn
    assert w_in.shape == (d, n_j * tn) and w_partner.shape == (d, (j_kv + 1) * tn)
    cos, sin = _rope_tables(seq)
    spb = seq // tm
    kern = functools.partial(_inproj_kernel, j_kv=j_kv, j_b=j_b, j_g=j_g)
    grp_specs, grp_shapes = [], []
    for gi, (_, dil) in enumerate(B_PAIRS):
        assert tm % dil == 0 and (tm // dil) % 16 == 0
        j0 = j_b + 3 * gi
        grp_specs.append(pl.BlockSpec((None, dil, tm // dil, tn),
                                      lambda i, j, j0=j0: (i // spb, 0, i % spb, jnp.clip(j - j0, 0, 2))))
        grp_shapes.append(jax.ShapeDtypeStruct((batch, dil, seq // dil, 3 * tn), MXU_DTYPE))
    return pl.pallas_call(
        kern,
        grid=(t // tm, n_j),
        in_specs=[
            pl.BlockSpec((tm, d), lambda i, j: (i, 0)),
            pl.BlockSpec((None, 6, d), lambda i, j: (boff + i // spb, 0, 0)),
            pl.BlockSpec((1, d), lambda i, j: (0, 0)),
            pl.BlockSpec((d, tn), lambda i, j: (0, j)),
            pl.BlockSpec((d, tn), lambda i, j: (0, jnp.minimum(j, j_kv))),
            pl.BlockSpec((tm, HEAD_DIM), lambda i, j: (i % spb, 0)),
            pl.BlockSpec((tm, HEAD_DIM), lambda i, j: (i % spb, 0)),
            pl.BlockSpec((2, HEAD_DIM), lambda i, j: (0, 0)),
            pl.BlockSpec((2, HEAD_DIM), lambda i, j: (0, 0)),
        ],
        out_specs=[
            pl.BlockSpec((tm, tn), lambda i, j: (i, jnp.clip(j, 0, j_kv - 1))),
            pl.BlockSpec((tm, tn), lambda i, j: (i, 0)),
            *grp_specs,
            pl.BlockSpec((tm, tn), lambda i, j: (i, jnp.clip(j - j_g, 0, n_j - j_g - 1))),
        ],
        out_shape=[
            jax.ShapeDtypeStruct((t, A_Q), MXU_DTYPE),
            jax.ShapeDtypeStruct((t, 2 * A_KV), MXU_DTYPE),
            *grp_shapes,
            jax.ShapeDtypeStruct((t, 2 * d), MXU_DTYPE),
        ],
        scratch_shapes=[pltpu.VMEM((tm, d), MXU_DTYPE), pltpu.VMEM((tn // HEAD_DIM, tm, HEAD_DIM), F32)],
        compiler_params=_params(("arbitrary", "arbitrary")),
        name="inproj",
    )(x2, mod3, n1.reshape(1, d), w_in, w_partner, cos, sin,
      jnp.stack([qn, _rope_partner(qn)]), jnp.stack([kn, _rope_partner(kn)]))


def _attn_a_kernel(q_ref, k_ref, v_ref, o_ref):
    k = k_ref[...]
    v = v_ref[...]

    tq = q_ref.shape[0]
    units = [(slice(r * tq // ATT_ROW_SPLIT, (r + 1) * tq // ATT_ROW_SPLIT),
              slice(hh * HEAD_DIM, (hh + 1) * HEAD_DIM))
             for hh in range(A_GROUP) for r in range(ATT_ROW_SPLIT)]

    def scores(u):
        rows, cols = units[u]
        return lax.dot_general(q_ref[rows, cols], k, (((1,), (1,)), ((), ())), preferred_element_type=F32)

    s_next = scores(0)
    for u, (rows, cols) in enumerate(units):
        s = s_next
        if u + 1 < len(units):
            s_next = scores(u + 1)
        m = jnp.max(s, axis=-1, keepdims=True)
        p = jnp.exp(s - m)
        l = jnp.sum(p, axis=-1, keepdims=True)
        o = jnp.dot(p.astype(v.dtype), v, preferred_element_type=F32)
        o_ref[rows, cols] = (o / l).astype(o_ref.dtype)


def _attn_a(qa, kva, batch, seq):
    t = qa.shape[0]
    tq = min(ATT_TQ, seq)
    nq = seq // tq
    gw = A_GROUP * HEAD_DIM
    return pl.pallas_call(
        _attn_a_kernel,
        grid=(batch, A_KV_HEADS, nq),
        in_specs=[
            pl.BlockSpec((tq, gw), lambda b, kh, qi: (b * nq + qi, kh)),
            pl.BlockSpec((seq, HEAD_DIM), lambda b, kh, qi: (b, kh)),
            pl.BlockSpec((seq, HEAD_DIM), lambda b, kh, qi: (b, A_KV_HEADS + kh)),
        ],
        out_specs=pl.BlockSpec((tq, gw), lambda b, kh, qi: (b * nq + qi, kh)),
        out_shape=jax.ShapeDtypeStruct((t, A_Q), MXU_DTYPE),
        compiler_params=_params(("arbitrary", "arbitrary", "arbitrary")),
        name="attn_dense",
    )(qa, kva, kva)


def _t5_bucket(rel):
    nb = REL_BUCKETS // 2
    ret = (rel > 0).astype(np.int32) * nb
    n = np.abs(rel)
    max_exact = nb // 2
    large = max_exact + (np.log(np.maximum(n, 1) / max_exact) / np.log(REL_MAX_DIST / max_exact)
                         * (nb - max_exact)).astype(np.int32)
    large = np.minimum(large, nb - 1)
    return (ret + np.where(n < max_exact, n, large)).astype(np.int32)


def _band_bias(table_g, dilation, tq, w):
    h = table_g.shape[1]
    offs = np.arange(-DIL_SIDE, DIL_SIDE + 1)
    diag = jnp.take(table_g, jnp.asarray(_t5_bucket(offs * dilation)), axis=0).astype(F32).T
    diag = jnp.concatenate([diag, jnp.full((h, 1), NEG, F32)], axis=1)
    n = tq + w
    u = np.arange(n)
    delta = np.where(u <= w - 1, u, u - n)
    out = []
    for var in range(3):
        m = delta - DIL_SIDE * var
        idx = np.where(np.abs(m) <= DIL_SIDE, m + DIL_SIDE, 2 * DIL_SIDE + 1)
        v = jnp.take(diag, jnp.asarray(idx), axis=1)
        toep = jnp.tile(v, (1, tq))[:, :tq * (n - 1)].reshape(h, tq, n - 1)[:, :, :w]
        out.append(toep)
    return jnp.stack(out, axis=0)


def _dil_kernel(q_ref, k_ref, v_ref, bias_ref, o_ref, lse_ref, *, length, tq, w):
    rp = q_ref.shape[0]
    nblk = length // tq
    lane = lax.broadcasted_iota(jnp.int32, (tq, LANES), 1)

    def blocks(units):
        work = []
        for ri, c in units:
            if isinstance(c, int):
                q0 = c * tq
                ws = min(max(q0 - DIL_SIDE, 0), length - w)
            else:
                q0 = pl.multiple_of(c * tq, tq)
                ws = pl.multiple_of(jnp.clip(q0 - DIL_SIDE, 0, length - w), DIL_SIDE)
            var = (q0 - ws) // DIL_SIDE
            for hh in range(B_HEADS_PER_GROUP):
                sl = slice(hh * HEAD_DIM, (hh + 1) * HEAD_DIM)
                q = q_ref[ri, pl.ds(q0, tq), sl]
                kw = k_ref[ri, pl.ds(ws, w), sl]
                s = lax.dot_general(q, kw, (((1,), (1,)), ((), ())), preferred_element_type=F32)
                work.append(dict(ri=ri, q0=q0, ws=ws, hh=hh, sl=sl, s=s + bias_ref[var, hh]))
        for it in work:
            it['m'] = jnp.max(it['s'], axis=-1, keepdims=True)
            p = jnp.exp(it['s'] - it['m'])
            it['l'] = jnp.sum(p, axis=-1, keepdims=True)
            it['p'] = p.astype(v_ref.dtype)
        for it in work:
            vw = v_ref[it['ri'], pl.ds(it['ws'], w), it['sl']]
            o = jnp.dot(it['p'], vw, preferred_element_type=F32)
            o_ref[it['ri'], pl.ds(it['q0'], tq), it['sl']] = (o / it['l']).astype(o_ref.dtype)
        for u in range(len(units)):
            lse_blk = jnp.zeros((tq, LANES), F32)
            for it in work[u * B_HEADS_PER_GROUP:(u + 1) * B_HEADS_PER_GROUP]:
                lse_blk = jnp.where(lane == it['hh'], it['m'] + jnp.log(it['l']), lse_blk)
            lse_ref[work[u * B_HEADS_PER_GROUP]['ri'], pl.ds(work[u * B_HEADS_PER_GROUP]['q0'], tq), :] = lse_blk

    if rp * nblk <= DIL_UNROLL:
        blocks([(ri, c) for ri in range(rp) for c in range(nblk)])
    else:
        assert rp == 1 and nblk % DIL_UNROLL == 0

        def body(c4, carry):
            blocks([(0, c4 * DIL_UNROLL + j) for j in range(DIL_UNROLL)])
            return carry

        lax.fori_loop(0, nblk // DIL_UNROLL, body, 0)


def _dilated_group(qkv, table_g, window, dilation):
    assert (window // 2) // dilation == DIL_SIDE
    batch, _, length, _ = qkv.shape
    assert length % DIL_SIDE == 0
    tq = min(DIL_TQ, length)
    w = min(length, tq + 2 * DIL_SIDE)
    assert length % tq == 0
    gw = B_HEADS_PER_GROUP * HEAD_DIM
    bias = _band_bias(table_g, dilation, tq, w)
    kern = functools.partial(_dil_kernel, length=length, tq=tq, w=w)
    rp = min(dilation, max(1, DIL_UNROLL // (length // tq)))
    assert dilation % rp == 0
    return pl.pallas_call(
        kern,
        grid=(batch, dilation // rp),
        in_specs=[
            pl.BlockSpec((None, rp, length, gw), lambda b, r: (b, r, 0, 0)),
            pl.BlockSpec((None, rp, length, gw), lambda b, r: (b, r, 0, 1)),
            pl.BlockSpec((None, rp, length, gw), lambda b, r: (b, r, 0, 2)),
            pl.BlockSpec((3, B_HEADS_PER_GROUP, tq, w), lambda b, r: (0, 0, 0, 0)),
        ],
        out_specs=[
            pl.BlockSpec((None, rp, length, gw), lambda b, r: (b, r, 0, 0)),
            pl.BlockSpec((None, rp, length, LANES), lambda b, r: (b, r, 0, 0)),
        ],
        out_shape=[
            jax.ShapeDtypeStruct((batch, dilation, length, gw), MXU_DTYPE),
            jax.ShapeDtypeStruct((batch, dilation, length, LANES), F32),
        ],
        compiler_params=_params(("arbitrary", "arbitrary")),
        name=f"attn_dilated_{dilation}",
    )(qkv, qkv, qkv, bias)


def _post_kernel(x_ref, ya_ref, o1_ref, o2_ref, o3_ref, l1_ref, l2_ref, l3_ref, ga_ref, gb_ref,
                 mod_ref, n2_ref, wa_ref, wb_ref, wo_ref, wr_ref, br_ref,
                 x1_ref, h2_ref, rt_ref, o_scr, l_scr):
    tm = x_ref.shape[0]
    o_refs, l_refs = (o1_ref, o2_ref, o3_ref), (l1_ref, l2_ref, l3_ref)
    for gi, (_, dil) in enumerate(B_PAIRS):
        if dil == 1:
            continue
        for r in range(dil):
            rows = pl.ds(r, tm // dil, stride=dil)
            l_scr[gi, rows, :] = l_refs[gi][r]
            for jh in range(B_HEADS_PER_GROUP):
                o_scr[gi, jh, rows, :] = o_refs[gi][r, :, jh * HEAD_DIM:(jh + 1) * HEAD_DIM].astype(F32)

    def lse_rows(gi, rs):
        return l_refs[gi][0, rs, :] if B_PAIRS[gi][1] == 1 else l_scr[gi, rs, :]

    def out_rows(gi, jh, rs):
        if B_PAIRS[gi][1] == 1:
            return o_refs[gi][0, rs, jh * HEAD_DIM:(jh + 1) * HEAD_DIM].astype(F32)
        return o_scr[gi, jh, rs, :]

    groups = [dict(rs=slice(g * tm // POST_SPLIT, (g + 1) * tm // POST_SPLIT)) for g in range(POST_SPLIT)]
    for it in groups:
        rs = it['rs']
        lses = [lse_rows(gi, rs) for gi in range(B_GROUPS)]
        parts = []
        for jh in range(B_HEADS_PER_GROUP):
            lj = [lg[:, jh:jh + 1] for lg in lses]
            mx = jnp.maximum(jnp.maximum(lj[0], lj[1]), lj[2])
            ex = [jnp.exp(v - mx) for v in lj]
            inv = 1.0 / (ex[0] + ex[1] + ex[2])
            acc = (ex[0] * inv) * out_rows(0, jh, rs)
            acc = acc + (ex[1] * inv) * out_rows(1, jh, rs)
            acc = acc + (ex[2] * inv) * out_rows(2, jh, rs)
            parts.append(acc.astype(MXU_DTYPE))
        it['yb'] = jnp.concatenate(parts, axis=1)
    for it in groups:
        it['a'] = jnp.dot(ya_ref[it['rs'], :], wa_ref[...], preferred_element_type=F32)
        it['bb'] = jnp.dot(it['yb'], wb_ref[...], preferred_element_type=F32)
    for it in groups:
        rs = it['rs']
        merged = ga_ref[rs, :].astype(F32) * it['a'] + gb_ref[rs, :].astype(F32) * it['bb']
        it['merged'] = merged.astype(MXU_DTYPE)
    for it in groups:
        it['out'] = jnp.dot(it['merged'], wo_ref[...], preferred_element_type=F32)
    for it in groups:
        rs = it['rs']
        x1 = x_ref[rs, :] + mod_ref[2:3, :] * it['out']
        x1_ref[rs, :] = x1
        h2 = _rms(x1) * n2_ref[...]
        h2 = h2 * (1.0 + mod_ref[4:5, :]) + mod_ref[3:4, :]
        h2_ref[rs, :] = _pack_bf16_pairs(h2)
        it['h_hi'] = h2.astype(MXU_DTYPE)
        it['h_lo'] = (h2 - it['h_hi'].astype(F32)).astype(MXU_DTYPE)
    for it in groups:
        r_hi = jnp.dot(it['h_hi'], wr_ref[...], preferred_element_type=F32)
        r_lo = jnp.dot(it['h_lo'], wr_ref[:, :LANES], preferred_element_type=F32)
        it['r'] = r_hi[:, :LANES] + r_hi[:, LANES:] + r_lo + br_ref[...]
    for it in groups:
        rt_ref[it['rs'], :] = _route(it['r'])


def _route(r):
    lane = lax.broadcasted_iota(jnp.int32, r.shape, 1)
    lanef = lane.astype(F32)
    ninf = -jnp.inf
    gmask = lane < N_GROUPS
    gl = jnp.where(gmask, r, ninf)
    gmax = jnp.max(gl, axis=-1, keepdims=True)
    grp = jnp.min(jnp.where(gl == gmax, lanef, float(LANES)), axis=-1, keepdims=True)
    gsum = jnp.sum(jnp.where(gmask, jnp.exp(jnp.where(gmask, r, gmax) - gmax), 0.0), axis=-1, keepdims=True)
    gate = 1.0 / gsum
    lo = N_GROUPS + grp * EXPERTS_PER_GROUP
    emask = jnp.logical_and(lanef >= lo, lanef < lo + EXPERTS_PER_GROUP)
    el = jnp.where(emask, r, ninf)
    v1 = jnp.max(el, axis=-1, keepdims=True)
    i1 = jnp.min(jnp.where(el == v1, lanef, float(LANES)), axis=-1, keepdims=True)
    el2 = jnp.where(lanef == i1, ninf, el)
    v2 = jnp.max(el2, axis=-1, keepdims=True)
    i2 = jnp.min(jnp.where(el2 == v2, lanef, float(LANES)), axis=-1, keepdims=True)
    tt = jnp.exp(v2 - v1)
    w1 = 1.0 / (1.0 + tt)
    w2 = tt * w1
    return jnp.where(lane == 0, i1 - N_GROUPS,
                     jnp.where(lane == 1, i2 - N_GROUPS,
                               jnp.where(lane == 2, gate * w1,
                                         jnp.where(lane == 3, gate * w2, 0.0))))


def _post(x2, ya, obs, lses, gates, mod3, boff, seq, n2, wa, wb, wo, wr, br):
    t, d = x2.shape
    tm = POST_TM
    assert t % tm == 0 and seq % tm == 0
    spb = seq // tm
    row = lambda i: (i, 0)
    const = lambda i: (0, 0)
    res = lambda i: (i // spb, 0, i % spb, 0)
    o_specs, l_specs = [], []
    for _, dil in B_PAIRS:
        assert tm % dil == 0 and (tm // dil) % 16 == 0
        o_specs.append(pl.BlockSpec((None, dil, tm // dil, B_OUT), res))
        l_specs.append(pl.BlockSpec((None, dil, tm // dil, LANES), res))
    return pl.pallas_call(
        _post_kernel,
        grid=(t // tm,),
        in_specs=[
            pl.BlockSpec((tm, d), row),
            pl.BlockSpec((tm, A_Q), row),
            *o_specs,
            *l_specs,
            pl.BlockSpec((tm, d), lambda i: (i, 0)),
            pl.BlockSpec((tm, d), lambda i: (i, 1)),
            pl.BlockSpec((None, 6, d), lambda i: (boff + i // spb, 0, 0)),
            pl.BlockSpec((1, d), const),
            pl.BlockSpec((A_Q, d), const, pipeline_mode=pl.Buffered(1)),
            pl.BlockSpec((B_OUT, d), const, pipeline_mode=pl.Buffered(1)),
            pl.BlockSpec((d, d), const, pipeline_mode=pl.Buffered(1)),
            pl.BlockSpec((d, 2 * LANES), const, pipeline_mode=pl.Buffered(1)),
            pl.BlockSpec((1, LANES), const),
        ],
        out_specs=[pl.BlockSpec((tm, d), row), pl.BlockSpec((tm, d // 2), row), pl.BlockSpec((tm, LANES), row)],
        out_shape=[jax.ShapeDtypeStruct((t, d), F32), jax.ShapeDtypeStruct((t, d // 2), jnp.uint32),
                   jax.ShapeDtypeStruct((t, LANES), F32)],
        scratch_shapes=[pltpu.VMEM((B_GROUPS, B_HEADS_PER_GROUP, tm, HEAD_DIM), F32),
                        pltpu.VMEM((B_GROUPS, tm, LANES), F32)],
        compiler_params=_params(("arbitrary",)),
        name="post_attn",
    )(x2, ya, obs[0], obs[1], obs[2], lses[0], lses[1], lses[2], gates, gates, mod3, n2.reshape(1, d),
      wa, wb, wo, wr, br)


def _slot_plan(experts):
    a = experts.shape[0]
    c = RANK_CHUNK
    assert a % c == 0 and a % MOE_TM == 0
    oh = (experts[:, None] == jnp.arange(N_EXPERTS, dtype=jnp.int32)[None, :])
    oh3 = oh.astype(jnp.bfloat16).reshape(a // c, c, N_EXPERTS)
    tri = (np.arange(c)[:, None] > np.arange(c)[None, :])
    within = jnp.einsum('ij,cjk->cik', jnp.asarray(tri, jnp.bfloat16), oh3, preferred_element_type=F32)
    tot = jnp.sum(oh3.astype(F32), axis=1)
    off = jnp.cumsum(tot, axis=0) - tot
    counts = jnp.sum(tot, axis=0).astype(jnp.int32)
    padded = (counts + MOE_TM - 1) // MOE_TM * MOE_TM
    pend = jnp.cumsum(padded)
    pstart = pend - padded
    slot = within + off[:, None, :] + pstart.astype(F32)[None, None, :]
    dest = jnp.sum(jnp.where(oh.reshape(a // c, c, N_EXPERTS), slot, 0.0), axis=-1).reshape(a).astype(jnp.int32)
    n_blocks = a // MOE_TM + N_EXPERTS
    idx = jnp.arange(n_blocks, dtype=jnp.int32)
    block_start = idx * MOE_TM
    block_e = jnp.minimum(jnp.sum((pend[None, :] <= block_start[:, None]).astype(jnp.int32), axis=1),
                          N_EXPERTS - 1)
    nb_used = pend[-1] // MOE_TM
    used = idx < nb_used
    last = nb_used - 1
    onehot_e = block_e[:, None] == jnp.arange(N_EXPERTS, dtype=jnp.int32)[None, :]
    seg_end = jnp.sum(jnp.where(onehot_e, (pstart + counts)[None, :], 0), axis=1)
    nv = jnp.where(used, jnp.clip(seg_end - block_start, 0, MOE_TM), 0).astype(jnp.int32)
    def at_expert(table):
        return jnp.sum(jnp.where(onehot_e, table[None, :], 0), axis=1)

    nblk_e = padded // MOE_TM
    pos_in_e = idx - at_expert(pstart // MOE_TM)
    local = pos_in_e % MOE_PAIR
    pair_size = jnp.minimum(MOE_PAIR, at_expert(nblk_e) - (pos_in_e - local))
    pairs_e = (nblk_e + MOE_PAIR - 1) // MOE_PAIR
    pair_id = at_expert(jnp.cumsum(pairs_e) - pairs_e) + pos_in_e // MOE_PAIR
    f_first = pair_id & 1
    base = 2 * (idx - local)
    pos_a = jnp.where(used, base + local, 2 * idx)
    pos_b = jnp.where(used, base + jnp.where(pair_size == 1, 1, 3 - local), 2 * idx + 1)
    be_last = jnp.sum(jnp.where(idx == last, block_e, 0))
    e_col = jnp.where(used, block_e, be_last)
    x_col = jnp.minimum(idx, last)
    slot_col = jnp.where(used, local, 0)
    o_first = jnp.where(jnp.logical_and(used, pair_size == MOE_PAIR), idx - local + 1, idx)
    zeros = jnp.zeros_like(idx)
    flag_a = jnp.where(used, EXP_FIRST, EXP_ZERO) + EXP_SLOT * slot_col
    flag_b = jnp.where(used, EXP_LAST, 0) + EXP_SLOT * slot_col
    rows_a = jnp.stack([x_col, e_col, jnp.where(used, f_first, zeros), nv, flag_a, o_first], axis=1)
    rows_b = jnp.stack([x_col, e_col, jnp.where(used, 1 - f_first, zeros), nv, flag_b, idx], axis=1)
    steps = jnp.zeros((2 * n_blocks, 6), jnp.int32).at[jnp.concatenate([pos_a, pos_b])].set(
        jnp.concatenate([rows_a, rows_b]).astype(jnp.int32))
    step_idx = jnp.arange(2 * n_blocks, dtype=jnp.int32)
    f_last = jnp.sum(jnp.where(step_idx == 2 * nb_used - 1, steps[:, 2], 0))
    f_col = jnp.where(step_idx < 2 * nb_used, steps[:, 2], f_last)
    plan = dict(xblk=steps[:, 0], expert=steps[:, 1], fhalf=f_col, nv=steps[:, 3], flags=steps[:, 4],
                oblk=steps[:, 5], nv_block=nv)
    return dest, plan, n_blocks


def _dispatch_kernel(nvb_ref, dest_ref, ha_ref, hb_ref, xs_ref, zbuf, sem, zsem, *, steps_a):
    i = pl.program_id(0)
    tm = ha_ref.shape[0]
    n_blocks = nvb_ref.shape[0]

    def zero_block(b):
        return pltpu.make_async_copy(zbuf, xs_ref.at[pl.ds(pl.multiple_of(b * MOE_TM, MOE_TM), MOE_TM)], zsem)

    @pl.when(i == 0)
    def _():
        zbuf[...] = jnp.zeros_like(zbuf)

        def zstart(b, carry):
            pl.when(nvb_ref[b] < MOE_TM)(lambda: zero_block(b).start())
            return carry

        def zwait(b, carry):
            pl.when(nvb_ref[b] < MOE_TM)(lambda: zero_block(b).wait())
            return carry

        lax.fori_loop(0, n_blocks, zstart, 0)
        lax.fori_loop(0, n_blocks, zwait, 0)

    def scatter(h_ref):
        def start(t, carry):
            for k in range(TOP_K):
                d = dest_ref[0, TOP_K * t + k]
                pltpu.make_async_copy(h_ref.at[pl.ds(t, 1)], xs_ref.at[pl.ds(d, 1)], sem).start()
            return carry

        lax.fori_loop(0, tm, start, 0, unroll=8)
        for _ in range(TOP_K):
            pltpu.make_async_copy(h_ref, xs_ref.at[pl.ds(0, tm)], sem).wait()

    pl.when(i < steps_a)(lambda: scatter(ha_ref))
    pl.when(i >= steps_a)(lambda: scatter(hb_ref))


def _dispatch(dest, nv_block, h_a, h_b):
    ta, d = h_a.shape
    tb = h_b.shape[0]
    tm = DISP_TM
    assert ta % tm == 0 and tb % tm == 0
    steps_a, steps_b = ta // tm, tb // tm
    n_slots = nv_block.shape[0] * MOE_TM
    grid_spec = pltpu.PrefetchScalarGridSpec(
        num_scalar_prefetch=1,
        grid=(steps_a + steps_b,),
        in_specs=[
            pl.BlockSpec((None, 1, TOP_K * tm), lambda i, nvb: (i, 0, 0), memory_space=pltpu.SMEM),
            pl.BlockSpec((tm, d), lambda i, nvb: (jnp.minimum(i, steps_a - 1), 0)),
            pl.BlockSpec((tm, d), lambda i, nvb: (jnp.maximum(i - steps_a, 0), 0)),
        ],
        out_specs=pl.BlockSpec(memory_space=pl.ANY),
        scratch_shapes=[pltpu.VMEM((MOE_TM, d), h_a.dtype), pltpu.SemaphoreType.DMA(()),
                        pltpu.SemaphoreType.DMA(())],
    )
    return pl.pallas_call(
        functools.partial(_dispatch_kernel, steps_a=steps_a),
        grid_spec=grid_spec,
        out_shape=jax.ShapeDtypeStruct((n_slots, d), h_a.dtype),
        compiler_params=_params(("arbitrary",)),
        name="moe_dispatch",
    )(nv_block, dest.reshape(steps_a + steps_b, 1, TOP_K * tm), h_a, h_b)


def _expert_kernel(xb_ref, e_ref, fh_ref, nv_ref, fl_ref, ob_ref, x_ref, wg_ref, wu_ref, wd_ref, y_ref,
                   wg_s, wu_s, wd_s, acc):
    del xb_ref, e_ref, fh_ref, ob_ref
    s = pl.program_id(0)
    nv = nv_ref[s]
    flags = fl_ref[s]
    first = (flags & EXP_FIRST) > 0
    last = (flags & EXP_LAST) > 0
    slot = (flags // EXP_SLOT) & 1
    tm = x_ref.shape[0]

    def compute(n_rows):
        wg_s[...] = wg_ref[...].astype(wg_s.dtype)
        wu_s[...] = wu_ref[...].astype(wu_s.dtype)
        wd_s[...] = wd_ref[...].astype(wd_s.dtype)
        x = _unpack_bf16_pairs(x_ref[0:n_rows, :]).astype(MXU_DTYPE)
        g = jnp.dot(x, wg_s[...], preferred_element_type=F32)
        u = jnp.dot(x, wu_s[...], preferred_element_type=F32)
        h = (g * _sigmoid(g)) * u
        part = jnp.dot(h.astype(MXU_DTYPE), wd_s[...], preferred_element_type=F32)

        @pl.when(first)
        def _():
            acc[slot, 0:n_rows, :] = part

        @pl.when(last)
        def _():
            y_ref[0:n_rows, :] = _pack_bf16_pairs(acc[slot, 0:n_rows, :] + part)
            if n_rows < tm:
                y_ref[n_rows:tm, :] = jnp.zeros((tm - n_rows, y_ref.shape[1]), y_ref.dtype)

    pl.when(nv > MOE_SUB)(functools.partial(compute, tm))
    pl.when(jnp.logical_and(nv > 0, nv <= MOE_SUB))(functools.partial(compute, MOE_SUB))

    @pl.when((flags & EXP_ZERO) > 0)
    def _():
        y_ref[...] = jnp.zeros_like(y_ref)


def _experts(xs, plan, wg, wu, wd):
    n_slots = xs.shape[0]
    _, d, ff = wg.shape
    tm = MOE_TM
    n_blocks = n_slots // tm
    assert ff % 2 == 0 and xs.shape[1] * 2 == d and MOE_PAIR == 2
    fh = ff // 2
    grid_spec = pltpu.PrefetchScalarGridSpec(
        num_scalar_prefetch=6,
        grid=(2 * n_blocks,),
        in_specs=[
            pl.BlockSpec((tm, d // 2), lambda s, xb, e, fhalf, nv, fl, ob: (xb[s], 0)),
            pl.BlockSpec((None, d, fh), lambda s, xb, e, fhalf, nv, fl, ob: (e[s], 0, fhalf[s])),
            pl.BlockSpec((None, d, fh), lambda s, xb, e, fhalf, nv, fl, ob: (e[s], 0, fhalf[s])),
            pl.BlockSpec((None, fh, d), lambda s, xb, e, fhalf, nv, fl, ob: (e[s], fhalf[s], 0)),
        ],
        out_specs=pl.BlockSpec((tm, d // 2), lambda s, xb, e, fhalf, nv, fl, ob: (ob[s], 0)),
        scratch_shapes=[pltpu.VMEM((d, fh), MXU_DTYPE), pltpu.VMEM((d, fh), MXU_DTYPE),
                        pltpu.VMEM((fh, d), MXU_DTYPE), pltpu.VMEM((MOE_PAIR, tm, d), F32)],
    )
    return pl.pallas_call(
        _expert_kernel,
        grid_spec=grid_spec,
        out_shape=jax.ShapeDtypeStruct((n_slots, d // 2), jnp.uint32),
        compiler_params=_params(("arbitrary",)),
        name="moe_experts",
    )(plan['xblk'], plan['expert'], plan['fhalf'], plan['nv'], plan['flags'], plan['oblk'], xs, wg, wu, wd)


def _combine_kernel(dest_ref, dest_next_ref, x1_ref, rt_ref, mod_ref, nf_ref, ys_ref, o_ref, ybuf, sem, *, final):
    i = pl.program_id(0)
    tm = x1_ref.shape[0]
    slot = i % 2

    def issue(idx_ref, s):
        def start(t, carry):
            for k in range(TOP_K):
                d = idx_ref[0, TOP_K * t + k]
                pltpu.make_async_copy(ys_ref.at[pl.ds(d, 1)], ybuf.at[s, pl.ds(k * tm + t, 1)], sem.at[s]).start()
            return carry

        lax.fori_loop(0, tm, start, 0, unroll=8)

    @pl.when(i == 0)
    def _():
        issue(dest_ref, 0)

    @pl.when(i + 1 < pl.num_programs(0))
    def _():
        issue(dest_next_ref, 1 - slot)

    for k in range(TOP_K):
        pltpu.make_async_copy(ys_ref.at[pl.ds(0, tm)], ybuf.at[slot, pl.ds(k * tm, tm)], sem.at[slot]).wait()
    rt = rt_ref[...]
    moe = (_unpack_bf16_pairs(ybuf[slot, 0:tm, :]) * rt[:, 2:3]
           + _unpack_bf16_pairs(ybuf[slot, tm:2 * tm, :]) * rt[:, 3:4])
    x2 = x1_ref[...] + mod_ref[5:6, :] * moe
    if final:
        x2 = _rms(x2) * nf_ref[...]
    o_ref[...] = x2


def _combine(dest_blocks, x1, rt, mod3, boff, seq, nf, ys, final):
    t, d = x1.shape
    tm = COMB_TM
    spb = seq // tm
    kern = functools.partial(_combine_kernel, final=final)
    n_steps = t // tm
    return pl.pallas_call(
        kern,
        grid=(n_steps,),
        in_specs=[
            pl.BlockSpec((None, 1, TOP_K * tm), lambda i: (i, 0, 0), memory_space=pltpu.SMEM),
            pl.BlockSpec((None, 1, TOP_K * tm), lambda i: (jnp.minimum(i + 1, n_steps - 1), 0, 0),
                         memory_space=pltpu.SMEM),
            pl.BlockSpec((tm, d), lambda i: (i, 0)),
            pl.BlockSpec((tm, LANES), lambda i: (i, 0)),
            pl.BlockSpec((None, 6, d), lambda i: (boff + i // spb, 0, 0)),
            pl.BlockSpec((1, d), lambda i: (0, 0)),
            pl.BlockSpec(memory_space=pl.ANY),
        ],
        out_specs=pl.BlockSpec((tm, d), lambda i: (i, 0)),
        out_shape=jax.ShapeDtypeStruct((t, d), F32),
        scratch_shapes=[pltpu.VMEM((2, TOP_K * tm, d // 2), jnp.uint32), pltpu.SemaphoreType.DMA((2,))],
        compiler_params=_params(("arbitrary",)),
        name="moe_combine",
    )(dest_blocks, dest_blocks, x1, rt, mod3, nf.reshape(1, d), ys)


def kernel(x_prompt, x_sample, c_prompt, c_sample, w_ada, b_ada, norm1, w_in, q_norm, k_norm, w_branch_a, w_branch_b, w_out, rel_bias, norm2, w_route_group, b_route_group, w_route_expert, b_route_expert, w_gate_exp, w_up_exp, w_down_exp, norm_final):
    depth = w_ada.shape[0]
    d = x_prompt.shape[-1]
    trunks = []
    boff = 0
    for x, c in ((x_prompt, c_prompt), (x_sample, c_sample)):
        b, s, _ = x.shape
        trunks.append(dict(x=x.reshape(b * s, d), batch=b, seq=s, boff=boff))
        boff += b
    n_cond = boff
    rows = -(-n_cond // 8) * 8
    c_all = jnp.concatenate([c_prompt, c_sample, jnp.zeros((rows - n_cond, d), F32)], axis=0)
    scale = HEAD_DIM ** -0.5

    for l in range(depth):
        mod3 = _adaln(c_all, w_ada[l], b_ada[l]).reshape(rows, 6, d)
        w_in_l = _inproj_weight(w_in[l].astype(MXU_DTYPE))
        w_partner = _rope_partner(w_in_l[:, :A_Q + 2 * A_KV])
        wa = w_branch_a[l].astype(MXU_DTYPE)
        wb = w_branch_b[l].astype(MXU_DTYPE)
        wo = w_out[l].astype(MXU_DTYPE)
        wr = jnp.concatenate([w_route_group[l], w_route_expert[l],
                              jnp.zeros((d, LANES - N_GROUPS - N_EXPERTS), F32)], axis=1)
        wr_hi = wr.astype(MXU_DTYPE)
        wr = jnp.concatenate([wr_hi, (wr - wr_hi.astype(F32)).astype(MXU_DTYPE)], axis=1)
        br = jnp.concatenate([b_route_group[l], b_route_expert[l],
                              jnp.zeros((LANES - N_GROUPS - N_EXPERTS,), F32)]).reshape(1, LANES)

        for tr in trunks:
            qa, kva, *qkv_groups, gates = _inproj(tr['x'], mod3, tr['boff'], tr['batch'], tr['seq'], norm1[l],
                                                  w_in_l, w_partner, q_norm[l] * scale, k_norm[l])
            ya = _attn_a(qa, kva, tr['batch'], tr['seq'])
            obs, lses = [], []
            for gi, (window, dilation) in enumerate(B_PAIRS):
                hs = slice(gi * B_HEADS_PER_GROUP, (gi + 1) * B_HEADS_PER_GROUP)
                o, lse = _dilated_group(qkv_groups[gi], rel_bias[:, hs], window, dilation)
                obs.append(o)
                lses.append(lse)
            tr['x1'], tr['h2'], tr['rt'] = _post(tr['x'], ya, obs, lses, gates, mod3, tr['boff'], tr['seq'],
                                                 norm2[l], wa, wb, wo, wr, br)

        experts = jnp.concatenate([tr['rt'][:, :TOP_K] for tr in trunks], axis=0).astype(jnp.int32).reshape(-1)
        dest, plan, n_blocks = _slot_plan(experts)
        a0 = 0
        for tr in trunks:
            n_tok = tr['x'].shape[0]
            tr['dest'] = dest[a0:a0 + TOP_K * n_tok]
            a0 += TOP_K * n_tok
        xs = _dispatch(dest, plan['nv_block'], trunks[0]['h2'], trunks[1]['h2'])
        ys = _experts(xs, plan, w_gate_exp[l], w_up_exp[l], w_down_exp[l])
        for tr in trunks:
            n_tok = tr['x'].shape[0]
            tr['x'] = _combine(tr['dest'].reshape(n_tok // COMB_TM, 1, TOP_K * COMB_TM), tr['x1'], tr['rt'],
                               mod3, tr['boff'], tr['seq'], norm_final, ys, final=(l == depth - 1))

    return tuple(tr['x'].reshape(tr['batch'], tr['seq'], d) for tr in trunks)
```

```python
import functools
import math

import numpy as np
import jax
import jax.numpy as jnp
from jax import lax
from jax.experimental import pallas as pl
from jax.experimental.pallas import tpu as pltpu

F32 = jnp.float32
MXU_DTYPE = jnp.bfloat16

HEAD_DIM = 128
A_HEADS = 8
A_KV_HEADS = 2
A_GROUP = A_HEADS // A_KV_HEADS
ROPE_THETA = 10000.0
GRID_W = 64
B_PAIRS = ((128, 1), (512, 4), (2048, 16))
B_GROUPS = len(B_PAIRS)
B_HEADS_PER_GROUP = 4
B_HEADS = B_GROUPS * B_HEADS_PER_GROUP
REL_BUCKETS = 32
REL_MAX_DIST = 1024
A_Q = A_HEADS * HEAD_DIM
A_KV = A_KV_HEADS * HEAD_DIM
B_W = B_HEADS * HEAD_DIM
B_OUT = B_HEADS_PER_GROUP * HEAD_DIM
N_GROUPS = 8
EXPERTS_PER_GROUP = 8
N_EXPERTS = N_GROUPS * EXPERTS_PER_GROUP
TOP_K = 2
EPS = 1e-6
NEG = -1e30
LANES = 128

VMEM_LIMIT_BYTES = 56 * 1024 * 1024
MOE_VMEM_LIMIT_BYTES = 62 * 1024 * 1024

PROJ_TM = 1024
PROJ_TN = 512
PROJ_SUB = 256
ATT_TQ = 512
ATT_ROW_SPLIT = 4
DIL_TQ = 128
DIL_SIDE = 64
DIL_UNROLL = 4
POST_TM = 256
POST_SPLIT = 2
MOE_TM = 512
MOE_SUB = 256
MOE_PAIR = 4
EXP_FIRST = 1
EXP_LAST = 2
EXP_ZERO = 4
EXP_SLOT = 8
DISP_TM = 512
COMB_TM = 256
RANK_CHUNK = 512


def _params(sem, vmem_limit=VMEM_LIMIT_BYTES):
    return pltpu.CompilerParams(dimension_semantics=sem, vmem_limit_bytes=vmem_limit)


def _ada_kernel(c_ref, w_ref, b_ref, o_ref):
    c = c_ref[...]
    a = (c * jax.nn.sigmoid(c)).astype(MXU_DTYPE)
    o_ref[...] = jnp.dot(a, w_ref[...].astype(MXU_DTYPE), preferred_element_type=F32) + b_ref[...]


def _adaln(c_all, w, b):
    r, d = c_all.shape
    n = w.shape[1]
    tn = min(n, 1024)
    return pl.pallas_call(
        _ada_kernel,
        grid=(n // tn,),
        in_specs=[pl.BlockSpec((r, d), lambda j: (0, 0)),
                  pl.BlockSpec((d, tn), lambda j: (0, j)),
                  pl.BlockSpec((1, tn), lambda j: (0, j))],
        out_specs=pl.BlockSpec((r, tn), lambda j: (0, j)),
        out_shape=jax.ShapeDtypeStruct((r, n), F32),
        compiler_params=_params(("arbitrary",)),
        name="adaln",
    )(c_all, w, b.reshape(1, n))


def _rms(x):
    return x * lax.rsqrt(jnp.mean(x * x, axis=-1, keepdims=True) + EPS)


def _sigmoid(x):
    return 0.5 * jnp.tanh(0.5 * x) + 0.5


def _pack_bf16_pairs(x):
    n = x.shape[1] // 2
    lo = pltpu.bitcast(x[:, :n].astype(jnp.bfloat16).astype(F32), jnp.uint32)
    hi = pltpu.bitcast(x[:, n:].astype(jnp.bfloat16).astype(F32), jnp.uint32)
    return (hi & jnp.uint32(0xFFFF0000)) | (lo >> 16)


def _unpack_bf16_pairs(u):
    lo = pltpu.bitcast(u << 16, F32)
    hi = pltpu.bitcast(u & jnp.uint32(0xFFFF0000), F32)
    return jnp.concatenate([lo, hi], axis=1)


def _rope_partner(a):
    lead = a.shape[:-1]
    quarter = HEAD_DIM // 4
    a5 = a.reshape(*lead, a.shape[-1] // HEAD_DIM, 2, 2, quarter)
    return a5[..., ::-1, :].reshape(a.shape)


def _norm_rope(x, xp, g2, cos, sin):
    r = lax.rsqrt(jnp.mean(x * x, axis=-1, keepdims=True) + EPS)
    return (x * (cos * g2[0:1, :]) + xp * (sin * g2[1:2, :])) * r


def _inproj_kernel(x_ref, mod_ref, n1_ref, w_ref, wp_ref, cos_ref, sin_ref, qn_ref, kn_ref,
                   qa_ref, kva_ref, qkv1_ref, qkv2_ref, qkv3_ref, g_ref, h_scr, de_scr, *, j_kv, j_b, j_g):
    j = pl.program_id(1)
    tm = x_ref.shape[0]

    @pl.when(j == 0)
    def _():
        h = _rms(x_ref[...]) * n1_ref[...]
        h = h * (1.0 + mod_ref[1:2, :]) + mod_ref[0:1, :]
        h_scr[...] = h.astype(h_scr.dtype)

    subs = PROJ_TN // PROJ_SUB
    sub_heads = PROJ_SUB // HEAD_DIM

    def piece(c, ref=w_ref):
        return jnp.dot(h_scr[...], ref[:, c * PROJ_SUB:(c + 1) * PROJ_SUB], preferred_element_type=F32)

    def head(a, hh):
        return a[:, hh * HEAD_DIM:(hh + 1) * HEAD_DIM]

    @pl.when(j < j_kv)
    def _():
        for c in range(subs):
            acc = piece(c)
            accp = piece(c, wp_ref)
            for hh in range(sub_heads):
                col = (c * sub_heads + hh) * HEAD_DIM
                qa_ref[:, col:col + HEAD_DIM] = _norm_rope(
                    head(acc, hh), head(accp, hh), qn_ref[...], cos_ref[...], sin_ref[...]).astype(qa_ref.dtype)

    @pl.when(j == j_kv)
    def _():
        for c in range(subs):
            acc = piece(c)
            roped = (c + 1) * sub_heads <= A_KV_HEADS
            assert roped or c * sub_heads >= A_KV_HEADS
            accp = piece(c, wp_ref) if roped else None
            for hh in range(sub_heads):
                gh = c * sub_heads + hh
                if roped:
                    val = _norm_rope(head(acc, hh), head(accp, hh), kn_ref[...], cos_ref[...], sin_ref[...])
                else:
                    val = head(acc, hh)
                kva_ref[:, gh * HEAD_DIM:(gh + 1) * HEAD_DIM] = val.astype(kva_ref.dtype)

    for gi, ((_, dil), ref) in enumerate(zip(B_PAIRS, (qkv1_ref, qkv2_ref, qkv3_ref))):
        j0 = j_b + 3 * gi

        @pl.when(jnp.logical_and(j >= j0, j < j0 + 3))
        def _(dil=dil, ref=ref, j0=j0):
            scale = jnp.where(j == j0, HEAD_DIM ** -0.5, 1.0).astype(F32)
            for c in range(subs):
                val = piece(c) * scale
                if dil == 1:
                    ref[0, :, c * PROJ_SUB:(c + 1) * PROJ_SUB] = val.astype(ref.dtype)
                    continue
                for hh in range(sub_heads):
                    de_scr[c * sub_heads + hh] = head(val, hh)
                for r in range(dil):
                    for hh in range(sub_heads):
                        gh = c * sub_heads + hh
                        ref[r, :, gh * HEAD_DIM:(gh + 1) * HEAD_DIM] = de_scr[
                            gh, pl.ds(r, tm // dil, stride=dil), :].astype(ref.dtype)

    @pl.when(j >= j_g)
    def _():
        for c in range(subs):
            g_ref[:, c * PROJ_SUB:(c + 1) * PROJ_SUB] = _sigmoid(piece(c)).astype(g_ref.dtype)


def _rope_tables(s):
    half = HEAD_DIM // 4
    inv = np.power(ROPE_THETA, -np.arange(half, dtype=np.float64) / half)
    t = np.arange(s)
    row = (t // GRID_W).astype(np.float64)
    col = (t % GRID_W).astype(np.float64)
    ang_r = row[:, None] * inv[None, :]
    ang_c = col[:, None] * inv[None, :]
    cos = np.concatenate([np.cos(ang_r), np.cos(ang_r), np.cos(ang_c), np.cos(ang_c)], axis=1)
    sin = np.concatenate([-np.sin(ang_r), np.sin(ang_r), -np.sin(ang_c), np.sin(ang_c)], axis=1)
    return jnp.asarray(cos, F32), jnp.asarray(sin, F32)


def _inproj_weight(w_in_l):
    b0 = A_Q + 2 * A_KV
    parts = [w_in_l[:, :b0]]
    for gi in range(B_GROUPS):
        for part in range(3):
            c0 = b0 + part * B_W + gi * B_OUT
            parts.append(w_in_l[:, c0:c0 + B_OUT])
    parts.append(w_in_l[:, b0 + 3 * B_W:])
    return jnp.concatenate(parts, axis=1)


def _inproj(x2, mod3, boff, batch, seq, n1, w_in, w_partner, qn, kn):
    t, d = x2.shape
    tm, tn = PROJ_TM, PROJ_TN
    assert t % tm == 0 and seq % tm == 0
    assert A_Q % tn == 0 and 2 * A_KV == tn and B_OUT == tn and d % tn == 0
    j_kv = A_Q // tn
    j_b = j_kv + 1
    j_g = j_b + 3 * B_GROUPS
    n_j = j_g + 2 * d // tn
    assert w_in.shape == (d, n_j * tn) and w_partner.shape == (d, (j_kv + 1) * tn)
    cos, sin = _rope_tables(seq)
    spb = seq // tm
    kern = functools.partial(_inproj_kernel, j_kv=j_kv, j_b=j_b, j_g=j_g)
    grp_specs, grp_shapes = [], []
    for gi, (_, dil) in enumerate(B_PAIRS):
        assert tm % dil == 0 and (tm // dil) % 16 == 0
        j0 = j_b + 3 * gi
        grp_specs.append(pl.BlockSpec((None, dil, tm // dil, tn),
                                      lambda i, j, j0=j0: (i // spb, 0, i % spb, jnp.clip(j - j0, 0, 2))))
        grp_shapes.append(jax.ShapeDtypeStruct((batch, dil, seq // dil, 3 * tn), MXU_DTYPE))
    return pl.pallas_call(
        kern,
        grid=(t // tm, n_j),
        in_specs=[
            pl.BlockSpec((tm, d), lambda i, j: (i, 0)),
            pl.BlockSpec((None, 6, d), lambda i, j: (boff + i // spb, 0, 0)),
            pl.BlockSpec((1, d), lambda i, j: (0, 0)),
            pl.BlockSpec((d, tn), lambda i, j: (0, j)),
            pl.BlockSpec((d, tn), lambda i, j: (0, jnp.minimum(j, j_kv))),
            pl.BlockSpec((tm, HEAD_DIM), lambda i, j: (i % spb, 0)),
            pl.BlockSpec((tm, HEAD_DIM), lambda i, j: (i % spb, 0)),
            pl.BlockSpec((2, HEAD_DIM), lambda i, j: (0, 0)),
            pl.BlockSpec((2, HEAD_DIM), lambda i, j: (0, 0)),
        ],
        out_specs=[
            pl.BlockSpec((tm, tn), lambda i, j: (i, jnp.clip(j, 0, j_kv - 1))),
            pl.BlockSpec((tm, tn), lambda i, j: (i, 0)),
            *grp_specs,
            pl.BlockSpec((tm, tn), lambda i, j: (i, jnp.clip(j - j_g, 0, n_j - j_g - 1))),
        ],
        out_shape=[
            jax.ShapeDtypeStruct((t, A_Q), MXU_DTYPE),
            jax.ShapeDtypeStruct((t, 2 * A_KV), MXU_DTYPE),
            *grp_shapes,
            jax.ShapeDtypeStruct((t, 2 * d), MXU_DTYPE),
        ],
        scratch_shapes=[pltpu.VMEM((tm, d), MXU_DTYPE), pltpu.VMEM((tn // HEAD_DIM, tm, HEAD_DIM), F32)],
        compiler_params=_params(("arbitrary", "arbitrary")),
        name="inproj",
    )(x2, mod3, n1.reshape(1, d), w_in, w_partner, cos, sin,
      jnp.stack([qn, _rope_partner(qn)]), jnp.stack([kn, _rope_partner(kn)]))


def _attn_a_kernel(q_ref, k_ref, v_ref, o_ref):
    k = k_ref[...]
    v = v_ref[...]

    tq = q_ref.shape[0]
    units = [(slice(r * tq // ATT_ROW_SPLIT, (r + 1) * tq // ATT_ROW_SPLIT),
              slice(hh * HEAD_DIM, (hh + 1) * HEAD_DIM))
             for hh in range(A_GROUP) for r in range(ATT_ROW_SPLIT)]

    def scores(u):
        rows, cols = units[u]
        return lax.dot_general(q_ref[rows, cols], k, (((1,), (1,)), ((), ())), preferred_element_type=F32)

    s_next = scores(0)
    for u, (rows, cols) in enumerate(units):
        s = s_next
        if u + 1 < len(units):
            s_next = scores(u + 1)
        m = jnp.max(s, axis=-1, keepdims=True)
        p = jnp.exp(s - m)
        l = jnp.sum(p, axis=-1, keepdims=True)
        o = jnp.dot(p.astype(v.dtype), v, preferred_element_type=F32)
        o_ref[rows, cols] = (o / l).astype(o_ref.dtype)


def _attn_a(qa, kva, batch, seq):
    t = qa.shape[0]
    tq = min(ATT_TQ, seq)
    nq = seq // tq
    gw = A_GROUP * HEAD_DIM
    return pl.pallas_call(
        _attn_a_kernel,
        grid=(batch, A_KV_HEADS, nq),
        in_specs=[
            pl.BlockSpec((tq, gw), lambda b, kh, qi: (b * nq + qi, kh)),
            pl.BlockSpec((seq, HEAD_DIM), lambda b, kh, qi: (b, kh)),
            pl.BlockSpec((seq, HEAD_DIM), lambda b, kh, qi: (b, A_KV_HEADS + kh)),
        ],
        out_specs=pl.BlockSpec((tq, gw), lambda b, kh, qi: (b * nq + qi, kh)),
        out_shape=jax.ShapeDtypeStruct((t, A_Q), MXU_DTYPE),
        compiler_params=_params(("arbitrary", "arbitrary", "arbitrary")),
        name="attn_dense",
    )(qa, kva, kva)


def _t5_bucket(rel):
    nb = REL_BUCKETS // 2
    ret = (rel > 0).astype(np.int32) * nb
    n = np.abs(rel)
    max_exact = nb // 2
    large = max_exact + (np.log(np.maximum(n, 1) / max_exact) / np.log(REL_MAX_DIST / max_exact)
                         * (nb - max_exact)).astype(np.int32)
    large = np.minimum(large, nb - 1)
    return (ret + np.where(n < max_exact, n, large)).astype(np.int32)


def _band_bias(table_g, dilation, tq, w):
    h = table_g.shape[1]
    offs = np.arange(-DIL_SIDE, DIL_SIDE + 1)
    diag = jnp.take(table_g, jnp.asarray(_t5_bucket(offs * dilation)), axis=0).astype(F32).T
    diag = jnp.concatenate([diag, jnp.full((h, 1), NEG, F32)], axis=1)
    n = tq + w
    u = np.arange(n)
    delta = np.where(u <= w - 1, u, u - n)
    out = []
    for var in range(3):
        m = delta - DIL_SIDE * var
        idx = np.where(np.abs(m) <= DIL_SIDE, m + DIL_SIDE, 2 * DIL_SIDE + 1)
        v = jnp.take(diag, jnp.asarray(idx), axis=1)
        toep = jnp.tile(v, (1, tq))[:, :tq * (n - 1)].reshape(h, tq, n - 1)[:, :, :w]
        out.append(toep)
    return jnp.stack(out, axis=0)


def _dil_kernel(q_ref, k_ref, v_ref, bias_ref, o_ref, lse_ref, *, length, tq, w):
    rp = q_ref.shape[0]
    nblk = length // tq
    lane = lax.broadcasted_iota(jnp.int32, (tq, LANES), 1)

    def blocks(units):
        work = []
        for ri, c in units:
            if isinstance(c, int):
                q0 = c * tq
                ws = min(max(q0 - DIL_SIDE, 0), length - w)
            else:
                q0 = pl.multiple_of(c * tq, tq)
                ws = pl.multiple_of(jnp.clip(q0 - DIL_SIDE, 0, length - w), DIL_SIDE)
            var = (q0 - ws) // DIL_SIDE
            for hh in range(B_HEADS_PER_GROUP):
                sl = slice(hh * HEAD_DIM, (hh + 1) * HEAD_DIM)
                q = q_ref[ri, pl.ds(q0, tq), sl]
                kw = k_ref[ri, pl.ds(ws, w), sl]
                s = lax.dot_general(q, kw, (((1,), (1,)), ((), ())), preferred_element_type=F32)
                work.append(dict(ri=ri, q0=q0, ws=ws, hh=hh, sl=sl, s=s + bias_ref[var, hh]))
        for it in work:
            it['m'] = jnp.max(it['s'], axis=-1, keepdims=True)
            p = jnp.exp(it['s'] - it['m'])
            it['l'] = jnp.sum(p, axis=-1, keepdims=True)
            it['p'] = p.astype(v_ref.dtype)
        for it in work:
            vw = v_ref[it['ri'], pl.ds(it['ws'], w), it['sl']]
            o = jnp.dot(it['p'], vw, preferred_element_type=F32)
            o_ref[it['ri'], pl.ds(it['q0'], tq), it['sl']] = (o / it['l']).astype(o_ref.dtype)
        for u in range(len(units)):
            lse_blk = jnp.zeros((tq, LANES), F32)
            for it in work[u * B_HEADS_PER_GROUP:(u + 1) * B_HEADS_PER_GROUP]:
                lse_blk = jnp.where(lane == it['hh'], it['m'] + jnp.log(it['l']), lse_blk)
            lse_ref[work[u * B_HEADS_PER_GROUP]['ri'], pl.ds(work[u * B_HEADS_PER_GROUP]['q0'], tq), :] = lse_blk

    if rp * nblk <= DIL_UNROLL:
        blocks([(ri, c) for ri in range(rp) for c in range(nblk)])
    else:
        assert rp == 1 and nblk % DIL_UNROLL == 0

        def body(c4, carry):
            blocks([(0, c4 * DIL_UNROLL + j) for j in range(DIL_UNROLL)])
            return carry

        lax.fori_loop(0, nblk // DIL_UNROLL, body, 0)


def _dilated_group(qkv, table_g, window, dilation):
    assert (window // 2) // dilation == DIL_SIDE
    batch, _, length, _ = qkv.shape
    assert length % DIL_SIDE == 0
    tq = min(DIL_TQ, length)
    w = min(length, tq + 2 * DIL_SIDE)
    assert length % tq == 0
    gw = B_HEADS_PER_GROUP * HEAD_DIM
    bias = _band_bias(table_g, dilation, tq, w)
    kern = functools.partial(_dil_kernel, length=length, tq=tq, w=w)
    rp = min(dilation, max(1, DIL_UNROLL // (length // tq)))
    assert dilation % rp == 0
    return pl.pallas_call(
        kern,
        grid=(batch, dilation // rp),
        in_specs=[
            pl.BlockSpec((None, rp, length, gw), lambda b, r: (b, r, 0, 0)),
            pl.BlockSpec((None, rp, length, gw), lambda b, r: (b, r, 0, 1)),
            pl.BlockSpec((None, rp, length, gw), lambda b, r: (b, r, 0, 2)),
            pl.BlockSpec((3, B_HEADS_PER_GROUP, tq, w), lambda b, r: (0, 0, 0, 0)),
        ],
        out_specs=[
            pl.BlockSpec((None, rp, length, gw), lambda b, r: (b, r, 0, 0)),
            pl.BlockSpec((None, rp, length, LANES), lambda b, r: (b, r, 0, 0)),
        ],
        out_shape=[
            jax.ShapeDtypeStruct((batch, dilation, length, gw), MXU_DTYPE),
            jax.ShapeDtypeStruct((batch, dilation, length, LANES), F32),
        ],
        compiler_params=_params(("arbitrary", "arbitrary")),
        name=f"attn_dilated_{dilation}",
    )(qkv, qkv, qkv, bias)


def _post_kernel(x_ref, ya_ref, o1_ref, o2_ref, o3_ref, l1_ref, l2_ref, l3_ref, ga_ref, gb_ref,
                 mod_ref, n2_ref, wa_ref, wb_ref, wo_ref, wr_ref, br_ref,
                 x1_ref, h2_ref, rt_ref, o_scr, l_scr):
    tm = x_ref.shape[0]
    o_refs, l_refs = (o1_ref, o2_ref, o3_ref), (l1_ref, l2_ref, l3_ref)
    for gi, (_, dil) in enumerate(B_PAIRS):
        if dil == 1:
            continue
        for r in range(dil):
            rows = pl.ds(r, tm // dil, stride=dil)
            l_scr[gi, rows, :] = l_refs[gi][r]
            for jh in range(B_HEADS_PER_GROUP):
                o_scr[gi, jh, rows, :] = o_refs[gi][r, :, jh * HEAD_DIM:(jh + 1) * HEAD_DIM].astype(F32)

    def lse_rows(gi, rs):
        return l_refs[gi][0, rs, :] if B_PAIRS[gi][1] == 1 else l_scr[gi, rs, :]

    def out_rows(gi, jh, rs):
        if B_PAIRS[gi][1] == 1:
            return o_refs[gi][0, rs, jh * HEAD_DIM:(jh + 1) * HEAD_DIM].astype(F32)
        return o_scr[gi, jh, rs, :]

    groups = [dict(rs=slice(g * tm // POST_SPLIT, (g + 1) * tm // POST_SPLIT)) for g in range(POST_SPLIT)]
    for it in groups:
        rs = it['rs']
        lses = [lse_rows(gi, rs) for gi in range(B_GROUPS)]
        parts = []
        for jh in range(B_HEADS_PER_GROUP):
            lj = [lg[:, jh:jh + 1] for lg in lses]
            mx = jnp.maximum(jnp.maximum(lj[0], lj[1]), lj[2])
            ex = [jnp.exp(v - mx) for v in lj]
            inv = 1.0 / (ex[0] + ex[1] + ex[2])
            acc = (ex[0] * inv) * out_rows(0, jh, rs)
            acc = acc + (ex[1] * inv) * out_rows(1, jh, rs)
            acc = acc + (ex[2] * inv) * out_rows(2, jh, rs)
            parts.append(acc.astype(MXU_DTYPE))
        it['yb'] = jnp.concatenate(parts, axis=1)
    for it in groups:
        it['a'] = jnp.dot(ya_ref[it['rs'], :], wa_ref[...], preferred_element_type=F32)
        it['bb'] = jnp.dot(it['yb'], wb_ref[...], preferred_element_type=F32)
    for it in groups:
        rs = it['rs']
        merged = ga_ref[rs, :].astype(F32) * it['a'] + gb_ref[rs, :].astype(F32) * it['bb']
        it['merged'] = merged.astype(MXU_DTYPE)
    for it in groups:
        it['out'] = jnp.dot(it['merged'], wo_ref[...], preferred_element_type=F32)
    for it in groups:
        rs = it['rs']
        x1 = x_ref[rs, :] + mod_ref[2:3, :] * it['out']
        x1_ref[rs, :] = x1
        h2 = _rms(x1) * n2_ref[...]
        h2 = h2 * (1.0 + mod_ref[4:5, :]) + mod_ref[3:4, :]
        h2_ref[rs, :] = _pack_bf16_pairs(h2)
        it['h_hi'] = h2.astype(MXU_DTYPE)
        it['h_lo'] = (h2 - it['h_hi'].astype(F32)).astype(MXU_DTYPE)
    for it in groups:
        r_hi = jnp.dot(it['h_hi'], wr_ref[...], preferred_element_type=F32)
        r_lo = jnp.dot(it['h_lo'], wr_ref[:, :LANES], preferred_element_type=F32)
        it['r'] = r_hi[:, :LANES] + r_hi[:, LANES:] + r_lo + br_ref[...]
    for it in groups:
        rt_ref[it['rs'], :] = _route(it['r'])


def _route(r):
    lane = lax.broadcasted_iota(jnp.int32, r.shape, 1)
    lanef = lane.astype(F32)
    ninf = -jnp.inf
    gmask = lane < N_GROUPS
    gl = jnp.where(gmask, r, ninf)
    gmax = jnp.max(gl, axis=-1, keepdims=True)
    grp = jnp.min(jnp.where(gl == gmax, lanef, float(LANES)), axis=-1, keepdims=True)
    gsum = jnp.sum(jnp.where(gmask, jnp.exp(jnp.where(gmask, r, gmax) - gmax), 0.0), axis=-1, keepdims=True)
    gate = 1.0 / gsum
    lo = N_GROUPS + grp * EXPERTS_PER_GROUP
    emask = jnp.logical_and(lanef >= lo, lanef < lo + EXPERTS_PER_GROUP)
    el = jnp.where(emask, r, ninf)
    v1 = jnp.max(el, axis=-1, keepdims=True)
    i1 = jnp.min(jnp.where(el == v1, lanef, float(LANES)), axis=-1, keepdims=True)
    el2 = jnp.where(lanef == i1, ninf, el)
    v2 = jnp.max(el2, axis=-1, keepdims=True)
    i2 = jnp.min(jnp.where(el2 == v2, lanef, float(LANES)), axis=-1, keepdims=True)
    tt = jnp.exp(v2 - v1)
    w1 = 1.0 / (1.0 + tt)
    w2 = tt * w1
    return jnp.where(lane == 0, i1 - N_GROUPS,
                     jnp.where(lane == 1, i2 - N_GROUPS,
                               jnp.where(lane == 2, gate * w1,
                                         jnp.where(lane == 3, gate * w2, 0.0))))


def _post(x2, ya, obs, lses, gates, mod3, boff, seq, n2, wa, wb, wo, wr, br):
    t, d = x2.shape
    tm = POST_TM
    assert t % tm == 0 and seq % tm == 0
    spb = seq // tm
    row = lambda i: (i, 0)
    const = lambda i: (0, 0)
    res = lambda i: (i // spb, 0, i % spb, 0)
    o_specs, l_specs = [], []
    for _, dil in B_PAIRS:
        assert tm % dil == 0 and (tm // dil) % 16 == 0
        o_specs.append(pl.BlockSpec((None, dil, tm // dil, B_OUT), res))
        l_specs.append(pl.BlockSpec((None, dil, tm // dil, LANES), res))
    return pl.pallas_call(
        _post_kernel,
        grid=(t // tm,),
        in_specs=[
            pl.BlockSpec((tm, d), row),
            pl.BlockSpec((tm, A_Q), row),
            *o_specs,
            *l_specs,
            pl.BlockSpec((tm, d), lambda i: (i, 0)),
            pl.BlockSpec((tm, d), lambda i: (i, 1)),
            pl.BlockSpec((None, 6, d), lambda i: (boff + i // spb, 0, 0)),
            pl.BlockSpec((1, d), const),
            pl.BlockSpec((A_Q, d), const, pipeline_mode=pl.Buffered(1)),
            pl.BlockSpec((B_OUT, d), const, pipeline_mode=pl.Buffered(1)),
            pl.BlockSpec((d, d), const, pipeline_mode=pl.Buffered(1)),
            pl.BlockSpec((d, 2 * LANES), const, pipeline_mode=pl.Buffered(1)),
            pl.BlockSpec((1, LANES), const),
        ],
        out_specs=[pl.BlockSpec((tm, d), row), pl.BlockSpec((tm, d // 2), row), pl.BlockSpec((tm, LANES), row)],
        out_shape=[jax.ShapeDtypeStruct((t, d), F32), jax.ShapeDtypeStruct((t, d // 2), jnp.uint32),
                   jax.ShapeDtypeStruct((t, LANES), F32)],
        scratch_shapes=[pltpu.VMEM((B_GROUPS, B_HEADS_PER_GROUP, tm, HEAD_DIM), F32),
                        pltpu.VMEM((B_GROUPS, tm, LANES), F32)],
        compiler_params=_params(("arbitrary",)),
        name="post_attn",
    )(x2, ya, obs[0], obs[1], obs[2], lses[0], lses[1], lses[2], gates, gates, mod3, n2.reshape(1, d),
      wa, wb, wo, wr, br)


def _slot_plan(experts):
    a = experts.shape[0]
    c = RANK_CHUNK
    assert a % c == 0 and a % MOE_TM == 0
    oh = (experts[:, None] == jnp.arange(N_EXPERTS, dtype=jnp.int32)[None, :])
    oh3 = oh.astype(jnp.bfloat16).reshape(a // c, c, N_EXPERTS)
    tri = (np.arange(c)[:, None] > np.arange(c)[None, :])
    within = jnp.einsum('ij,cjk->cik', jnp.asarray(tri, jnp.bfloat16), oh3, preferred_element_type=F32)
    tot = jnp.sum(oh3.astype(F32), axis=1)
    off = jnp.cumsum(tot, axis=0) - tot
    counts = jnp.sum(tot, axis=0).astype(jnp.int32)
    padded = (counts + MOE_TM - 1) // MOE_TM * MOE_TM
    pend = jnp.cumsum(padded)
    pstart = pend - padded
    slot = within + off[:, None, :] + pstart.astype(F32)[None, None, :]
    dest = jnp.sum(jnp.where(oh.reshape(a // c, c, N_EXPERTS), slot, 0.0), axis=-1).reshape(a).astype(jnp.int32)
    n_blocks = a // MOE_TM + N_EXPERTS
    idx = jnp.arange(n_blocks, dtype=jnp.int32)
    block_start = idx * MOE_TM
    block_e = jnp.minimum(jnp.sum((pend[None, :] <= block_start[:, None]).astype(jnp.int32), axis=1),
                          N_EXPERTS - 1)
    nb_used = pend[-1] // MOE_TM
    used = idx < nb_used
    last = nb_used - 1
    onehot_e = block_e[:, None] == jnp.arange(N_EXPERTS, dtype=jnp.int32)[None, :]
    seg_end = jnp.sum(jnp.where(onehot_e, (pstart + counts)[None, :], 0), axis=1)
    nv = jnp.where(used, jnp.clip(seg_end - block_start, 0, MOE_TM), 0).astype(jnp.int32)
    def at_expert(table):
        return jnp.sum(jnp.where(onehot_e, table[None, :], 0), axis=1)

    nblk_e = padded // MOE_TM
    pos_in_e = idx - at_expert(pstart // MOE_TM)
    local = pos_in_e % MOE_PAIR
    pair_size = jnp.minimum(MOE_PAIR, at_expert(nblk_e) - (pos_in_e - local))
    pairs_e = (nblk_e + MOE_PAIR - 1) // MOE_PAIR
    pair_id = at_expert(jnp.cumsum(pairs_e) - pairs_e) + pos_in_e // MOE_PAIR
    f_first = pair_id & 1
    base = 2 * (idx - local)
    pos_a = jnp.where(used, base + local, 2 * idx)
    pos_b = jnp.where(used, base + 2 * pair_size - 1 - local, 2 * idx + 1)
    be_last = jnp.sum(jnp.where(idx == last, block_e, 0))
    e_col = jnp.where(used, block_e, be_last)
    x_col = jnp.minimum(idx, last)
    slot_col = jnp.where(used, local, 0)
    o_first = jnp.where(used, idx - local + pair_size - 1, idx)
    zeros = jnp.zeros_like(idx)
    flag_a = jnp.where(used, EXP_FIRST, EXP_ZERO) + EXP_SLOT * slot_col
    flag_b = jnp.where(used, EXP_LAST, 0) + EXP_SLOT * slot_col
    rows_a = jnp.stack([x_col, e_col, jnp.where(used, f_first, zeros), nv, flag_a, o_first], axis=1)
    rows_b = jnp.stack([x_col, e_col, jnp.where(used, 1 - f_first, zeros), nv, flag_b, idx], axis=1)
    steps = jnp.zeros((2 * n_blocks, 6), jnp.int32).at[jnp.concatenate([pos_a, pos_b])].set(
        jnp.concatenate([rows_a, rows_b]).astype(jnp.int32))
    step_idx = jnp.arange(2 * n_blocks, dtype=jnp.int32)
    f_last = jnp.sum(jnp.where(step_idx == 2 * nb_used - 1, steps[:, 2], 0))
    f_col = jnp.where(step_idx < 2 * nb_used, steps[:, 2], f_last)
    plan = dict(xblk=steps[:, 0], expert=steps[:, 1], fhalf=f_col, nv=steps[:, 3], flags=steps[:, 4],
                oblk=steps[:, 5], nv_block=nv)
    return dest, plan, n_blocks


def _dispatch_kernel(nvb_ref, dest_ref, ha_ref, hb_ref, xs_ref, zbuf, sem, zsem, *, steps_a):
    i = pl.program_id(0)
    tm = ha_ref.shape[0]
    n_blocks = nvb_ref.shape[0]

    def zero_block(b):
        return pltpu.make_async_copy(zbuf, xs_ref.at[pl.ds(pl.multiple_of(b * MOE_TM, MOE_TM), MOE_TM)], zsem)

    @pl.when(i == 0)
    def _():
        zbuf[...] = jnp.zeros_like(zbuf)

        def zstart(b, carry):
            pl.when(nvb_ref[b] < MOE_TM)(lambda: zero_block(b).start())
            return carry

        def zwait(b, carry):
            pl.when(nvb_ref[b] < MOE_TM)(lambda: zero_block(b).wait())
            return carry

        lax.fori_loop(0, n_blocks, zstart, 0)
        lax.fori_loop(0, n_blocks, zwait, 0)

    def scatter(h_ref):
        def start(t, carry):
            for k in range(TOP_K):
                d = dest_ref[0, TOP_K * t + k]
                pltpu.make_async_copy(h_ref.at[pl.ds(t, 1)], xs_ref.at[pl.ds(d, 1)], sem).start(priority=k % 2)
            return carry

        lax.fori_loop(0, tm, start, 0, unroll=8)
        for _ in range(TOP_K):
            pltpu.make_async_copy(h_ref, xs_ref.at[pl.ds(0, tm)], sem).wait()

    pl.when(i < steps_a)(lambda: scatter(ha_ref))
    pl.when(i >= steps_a)(lambda: scatter(hb_ref))


def _dispatch(dest, nv_block, h_a, h_b):
    ta, d = h_a.shape
    tb = h_b.shape[0]
    tm = DISP_TM
    assert ta % tm == 0 and tb % tm == 0
    steps_a, steps_b = ta // tm, tb // tm
    n_slots = nv_block.shape[0] * MOE_TM
    grid_spec = pltpu.PrefetchScalarGridSpec(
        num_scalar_prefetch=1,
        grid=(steps_a + steps_b,),
        in_specs=[
            pl.BlockSpec((None, 1, TOP_K * tm), lambda i, nvb: (i, 0, 0), memory_space=pltpu.SMEM),
            pl.BlockSpec((tm, d), lambda i, nvb: (jnp.minimum(i, steps_a - 1), 0)),
            pl.BlockSpec((tm, d), lambda i, nvb: (jnp.maximum(i - steps_a, 0), 0)),
        ],
        out_specs=pl.BlockSpec(memory_space=pl.ANY),
        scratch_shapes=[pltpu.VMEM((MOE_TM, d), h_a.dtype), pltpu.SemaphoreType.DMA(()),
                        pltpu.SemaphoreType.DMA(())],
    )
    return pl.pallas_call(
        functools.partial(_dispatch_kernel, steps_a=steps_a),
        grid_spec=grid_spec,
        out_shape=jax.ShapeDtypeStruct((n_slots, d), h_a.dtype),
        compiler_params=_params(("arbitrary",)),
        name="moe_dispatch",
    )(nv_block, dest.reshape(steps_a + steps_b, 1, TOP_K * tm), h_a, h_b)


def _expert_kernel(xb_ref, e_ref, fh_ref, nv_ref, fl_ref, ob_ref, x_ref, wg_ref, wu_ref, wd_ref, y_ref,
                   wg_s, wu_s, wd_s, acc):
    del xb_ref, e_ref, fh_ref, ob_ref
    s = pl.program_id(0)
    nv = nv_ref[s]
    flags = fl_ref[s]
    first = (flags & EXP_FIRST) > 0
    last = (flags & EXP_LAST) > 0
    slot = (flags // EXP_SLOT) & 3
    tm = x_ref.shape[0]

    def compute(n_rows):
        wg_s[...] = wg_ref[...].astype(wg_s.dtype)
        wu_s[...] = wu_ref[...].astype(wu_s.dtype)
        wd_s[...] = wd_ref[...].astype(wd_s.dtype)
        x = _unpack_bf16_pairs(x_ref[0:n_rows, :]).astype(MXU_DTYPE)
        g = jnp.dot(x, wg_s[...], preferred_element_type=F32)
        u = jnp.dot(x, wu_s[...], preferred_element_type=F32)
        h = (g * _sigmoid(g)) * u
        part = jnp.dot(h.astype(MXU_DTYPE), wd_s[...], preferred_element_type=F32)

        @pl.when(first)
        def _():
            acc[slot, 0:n_rows, :] = part

        @pl.when(last)
        def _():
            y_ref[0:n_rows, :] = _pack_bf16_pairs(acc[slot, 0:n_rows, :] + part)
            if n_rows < tm:
                y_ref[n_rows:tm, :] = jnp.zeros((tm - n_rows, y_ref.shape[1]), y_ref.dtype)

    pl.when(nv > MOE_SUB)(functools.partial(compute, tm))
    pl.when(jnp.logical_and(nv > 0, nv <= MOE_SUB))(functools.partial(compute, MOE_SUB))

    @pl.when((flags & EXP_ZERO) > 0)
    def _():
        y_ref[...] = jnp.zeros_like(y_ref)


def _experts(xs, plan, wg, wu, wd):
    n_slots = xs.shape[0]
    _, d, ff = wg.shape
    tm = MOE_TM
    n_blocks = n_slots // tm
    assert ff % 2 == 0 and xs.shape[1] * 2 == d and 1 <= MOE_PAIR <= 4
    fh = ff // 2
    grid_spec = pltpu.PrefetchScalarGridSpec(
        num_scalar_prefetch=6,
        grid=(2 * n_blocks,),
        in_specs=[
            pl.BlockSpec((tm, d // 2), lambda s, xb, e, fhalf, nv, fl, ob: (xb[s], 0)),
            pl.BlockSpec((None, d, fh), lambda s, xb, e, fhalf, nv, fl, ob: (e[s], 0, fhalf[s])),
            pl.BlockSpec((None, d, fh), lambda s, xb, e, fhalf, nv, fl, ob: (e[s], 0, fhalf[s])),
            pl.BlockSpec((None, fh, d), lambda s, xb, e, fhalf, nv, fl, ob: (e[s], fhalf[s], 0)),
        ],
        out_specs=pl.BlockSpec((tm, d // 2), lambda s, xb, e, fhalf, nv, fl, ob: (ob[s], 0)),
        scratch_shapes=[pltpu.VMEM((d, fh), MXU_DTYPE), pltpu.VMEM((d, fh), MXU_DTYPE),
                        pltpu.VMEM((fh, d), MXU_DTYPE), pltpu.VMEM((MOE_PAIR, tm, d), F32)],
    )
    return pl.pallas_call(
        _expert_kernel,
        grid_spec=grid_spec,
        out_shape=jax.ShapeDtypeStruct((n_slots, d // 2), jnp.uint32),
        compiler_params=_params(("arbitrary",), MOE_VMEM_LIMIT_BYTES),
        name="moe_experts",
    )(plan['xblk'], plan['expert'], plan['fhalf'], plan['nv'], plan['flags'], plan['oblk'], xs, wg, wu, wd)


def _combine_kernel(dest_ref, dest_next_ref, x1_ref, rt_ref, mod_ref, nf_ref, ys_ref, o_ref, ybuf, sem, *, final):
    i = pl.program_id(0)
    tm = x1_ref.shape[0]
    slot = i % 2

    def issue(idx_ref, s):
        def start(t, carry):
            for k in range(TOP_K):
                d = idx_ref[0, TOP_K * t + k]
                pltpu.make_async_copy(ys_ref.at[pl.ds(d, 1)], ybuf.at[s, pl.ds(k * tm + t, 1)],
                                      sem.at[s]).start(priority=k % 2)
            return carry

        lax.fori_loop(0, tm, start, 0, unroll=8)

    @pl.when(i == 0)
    def _():
        issue(dest_ref, 0)

    @pl.when(i + 1 < pl.num_programs(0))
    def _():
        issue(dest_next_ref, 1 - slot)

    for k in range(TOP_K):
        pltpu.make_async_copy(ys_ref.at[pl.ds(0, tm)], ybuf.at[slot, pl.ds(k * tm, tm)], sem.at[slot]).wait()
    rt = rt_ref[...]
    moe = (_unpack_bf16_pairs(ybuf[slot, 0:tm, :]) * rt[:, 2:3]
           + _unpack_bf16_pairs(ybuf[slot, tm:2 * tm, :]) * rt[:, 3:4])
    x2 = x1_ref[...] + mod_ref[5:6, :] * moe
    if final:
        x2 = _rms(x2) * nf_ref[...]
    o_ref[...] = x2


def _combine(dest_blocks, x1, rt, mod3, boff, seq, nf, ys, final):
    t, d = x1.shape
    tm = COMB_TM
    spb = seq // tm
    kern = functools.partial(_combine_kernel, final=final)
    n_steps = t // tm
    return pl.pallas_call(
        kern,
        grid=(n_steps,),
        in_specs=[
            pl.BlockSpec((None, 1, TOP_K * tm), lambda i: (i, 0, 0), memory_space=pltpu.SMEM),
            pl.BlockSpec((None, 1, TOP_K * tm), lambda i: (jnp.minimum(i + 1, n_steps - 1), 0, 0),
                         memory_space=pltpu.SMEM),
            pl.BlockSpec((tm, d), lambda i: (i, 0)),
            pl.BlockSpec((tm, LANES), lambda i: (i, 0)),
            pl.BlockSpec((None, 6, d), lambda i: (boff + i // spb, 0, 0)),
            pl.BlockSpec((1, d), lambda i: (0, 0)),
            pl.BlockSpec(memory_space=pl.ANY),
        ],
        out_specs=pl.BlockSpec((tm, d), lambda i: (i, 0)),
        out_shape=jax.ShapeDtypeStruct((t, d), F32),
        scratch_shapes=[pltpu.VMEM((2, TOP_K * tm, d // 2), jnp.uint32), pltpu.SemaphoreType.DMA((2,))],
        compiler_params=_params(("arbitrary",)),
        name="moe_combine",
    )(dest_blocks, dest_blocks, x1, rt, mod3, nf.reshape(1, d), ys)


def kernel(x_prompt, x_sample, c_prompt, c_sample, w_ada, b_ada, norm1, w_in, q_norm, k_norm, w_branch_a, w_branch_b, w_out, rel_bias, norm2, w_route_group, b_route_group, w_route_expert, b_route_expert, w_gate_exp, w_up_exp, w_down_exp, norm_final):
    depth = w_ada.shape[0]
    d = x_prompt.shape[-1]
    trunks = []
    boff = 0
    for x, c in ((x_prompt, c_prompt), (x_sample, c_sample)):
        b, s, _ = x.shape
        trunks.append(dict(x=x.reshape(b * s, d), batch=b, seq=s, boff=boff))
        boff += b
    n_cond = boff
    rows = -(-n_cond // 8) * 8
    c_all = jnp.concatenate([c_prompt, c_sample, jnp.zeros((rows - n_cond, d), F32)], axis=0)
    scale = HEAD_DIM ** -0.5

    for l in range(depth):
        mod3 = _adaln(c_all, w_ada[l], b_ada[l]).reshape(rows, 6, d)
        w_in_l = _inproj_weight(w_in[l].astype(MXU_DTYPE))
        w_partner = _rope_partner(w_in_l[:, :A_Q + 2 * A_KV])
        wa = w_branch_a[l].astype(MXU_DTYPE)
        wb = w_branch_b[l].astype(MXU_DTYPE)
        wo = w_out[l].astype(MXU_DTYPE)
        wr = jnp.concatenate([w_route_group[l], w_route_expert[l],
                              jnp.zeros((d, LANES - N_GROUPS - N_EXPERTS), F32)], axis=1)
        wr_hi = wr.astype(MXU_DTYPE)
        wr = jnp.concatenate([wr_hi, (wr - wr_hi.astype(F32)).astype(MXU_DTYPE)], axis=1)
        br = jnp.concatenate([b_route_group[l], b_route_expert[l],
                              jnp.zeros((LANES - N_GROUPS - N_EXPERTS,), F32)]).reshape(1, LANES)

        for tr in trunks:
            qa, kva, *qkv_groups, gates = _inproj(tr['x'], mod3, tr['boff'], tr['batch'], tr['seq'], norm1[l],
                                                  w_in_l, w_partner, q_norm[l] * scale, k_norm[l])
            ya = _attn_a(qa, kva, tr['batch'], tr['seq'])
            obs, lses = [], []
            for gi, (window, dilation) in enumerate(B_PAIRS):
                hs = slice(gi * B_HEADS_PER_GROUP, (gi + 1) * B_HEADS_PER_GROUP)
                o, lse = _dilated_group(qkv_groups[gi], rel_bias[:, hs], window, dilation)
                obs.append(o)
                lses.append(lse)
            tr['x1'], tr['h2'], tr['rt'] = _post(tr['x'], ya, obs, lses, gates, mod3, tr['boff'], tr['seq'],
                                                 norm2[l], wa, wb, wo, wr, br)

        experts = jnp.concatenate([tr['rt'][:, :TOP_K] for tr in trunks], axis=0).astype(jnp.int32).reshape(-1)
        dest, plan, n_blocks = _slot_plan(experts)
        a0 = 0
        for tr in trunks:
            n_tok = tr['x'].shape[0]
            tr['dest'] = dest[a0:a0 + TOP_K * n_tok]
            a0 += TOP_K * n_tok
        xs = _dispatch(dest, plan['nv_block'], trunks[0]['h2'], trunks[1]['h2'])
        ys = _experts(xs, plan, w_gate_exp[l], w_up_exp[l], w_down_exp[l])
        for tr in trunks:
            n_tok = tr['x'].shape[0]
            tr['x'] = _combine(tr['dest'].reshape(n_tok // COMB_TM, 1, TOP_K * COMB_TM), tr['x1'], tr['rt'],
                               mod3, tr['boff'], tr['seq'], norm_final, ys, final=(l == depth - 1))

    return tuple(tr['x'].reshape(tr['batch'], tr['seq'], d) for tr in trunks)
```

```python
import functools
import math

import numpy as np
import jax
import jax.numpy as jnp
from jax import lax
from jax.experimental import pallas as pl
from jax.experimental.pallas import tpu as pltpu

F32 = jnp.float32
MXU_DTYPE = jnp.bfloat16

HEAD_DIM = 128
A_HEADS = 8
A_KV_HEADS = 2
A_GROUP = A_HEADS // A_KV_HEADS
ROPE_THETA = 10000.0
GRID_W = 64
B_PAIRS = ((128, 1), (512, 4), (2048, 16))
B_GROUPS = len(B_PAIRS)
B_HEADS_PER_GROUP = 4
B_HEADS = B_GROUPS * B_HEADS_PER_GROUP
REL_BUCKETS = 32
REL_MAX_DIST = 1024
A_Q = A_HEADS * HEAD_DIM
A_KV = A_KV_HEADS * HEAD_DIM
B_W = B_HEADS * HEAD_DIM
B_OUT = B_HEADS_PER_GROUP * HEAD_DIM
N_GROUPS = 8
EXPERTS_PER_GROUP = 8
N_EXPERTS = N_GROUPS * EXPERTS_PER_GROUP
TOP_K = 2
EPS = 1e-6
NEG = -1e30
LANES = 128

VMEM_LIMIT_BYTES = 56 * 1024 * 1024
MOE_VMEM_LIMIT_BYTES = 62 * 1024 * 1024

PROJ_TM = 1024
PROJ_TN = 512
PROJ_SUB = 256
ATT_TQ = 512
ATT_ROW_SPLIT = 4
DIL_TQ = 128
DIL_SIDE = 64
DIL_UNROLL = 4
POST_TM = 256
POST_SPLIT = 2
MOE_TM = 512
MOE_SUB = 128
MOE_PAIR = 4
EXP_FIRST = 1
EXP_LAST = 2
EXP_ZERO = 4
EXP_SLOT = 8
DISP_TM = 1024
COMB_TM = 512
RANK_CHUNK = 512


def _params(sem, vmem_limit=VMEM_LIMIT_BYTES):
    return pltpu.CompilerParams(dimension_semantics=sem, vmem_limit_bytes=vmem_limit)


def _ada_kernel(c_ref, w_ref, b_ref, o_ref):
    c = c_ref[...]
    a = (c * jax.nn.sigmoid(c)).astype(MXU_DTYPE)
    o_ref[...] = jnp.dot(a, w_ref[...].astype(MXU_DTYPE), preferred_element_type=F32) + b_ref[...]


def _adaln(c_all, w, b):
    r, d = c_all.shape
    n = w.shape[1]
    tn = min(n, 1024)
    return pl.pallas_call(
        _ada_kernel,
        grid=(n // tn,),
        in_specs=[pl.BlockSpec((r, d), lambda j: (0, 0)),
                  pl.BlockSpec((d, tn), lambda j: (0, j)),
                  pl.BlockSpec((1, tn), lambda j: (0, j))],
        out_specs=pl.BlockSpec((r, tn), lambda j: (0, j)),
        out_shape=jax.ShapeDtypeStruct((r, n), F32),
        compiler_params=_params(("arbitrary",)),
        name="adaln",
    )(c_all, w, b.reshape(1, n))


def _rms(x):
    return x * lax.rsqrt(jnp.mean(x * x, axis=-1, keepdims=True) + EPS)


def _sigmoid(x):
    return 0.5 * jnp.tanh(0.5 * x) + 0.5


def _pack_bf16_pairs(x):
    n = x.shape[1] // 2
    lo = pltpu.bitcast(x[:, :n].astype(jnp.bfloat16).astype(F32), jnp.uint32)
    hi = pltpu.bitcast(x[:, n:].astype(jnp.bfloat16).astype(F32), jnp.uint32)
    return (hi & jnp.uint32(0xFFFF0000)) | (lo >> 16)


def _unpack_bf16_pairs(u):
    lo = pltpu.bitcast(u << 16, F32)
    hi = pltpu.bitcast(u & jnp.uint32(0xFFFF0000), F32)
    return jnp.concatenate([lo, hi], axis=1)


def _rope_partner(a):
    lead = a.shape[:-1]
    quarter = HEAD_DIM // 4
    a5 = a.reshape(*lead, a.shape[-1] // HEAD_DIM, 2, 2, quarter)
    return a5[..., ::-1, :].reshape(a.shape)


def _norm_rope(x, xp, g2, cos, sin):
    r = lax.rsqrt(jnp.mean(x * x, axis=-1, keepdims=True) + EPS)
    return (x * (cos * g2[0:1, :]) + xp * (sin * g2[1:2, :])) * r


def _inproj_kernel(x_ref, mod_ref, n1_ref, w_ref, wp_ref, cos_ref, sin_ref, qn_ref, kn_ref,
                   qa_ref, kva_ref, qkv1_ref, qkv2_ref, qkv3_ref, g_ref, h_scr, de_scr, *, j_kv, j_b, j_g):
    j = pl.program_id(1)
    tm = x_ref.shape[0]

    @pl.when(j == 0)
    def _():
        h = _rms(x_ref[...]) * n1_ref[...]
        h = h * (1.0 + mod_ref[1:2, :]) + mod_ref[0:1, :]
        h_scr[...] = h.astype(h_scr.dtype)

    subs = PROJ_TN // PROJ_SUB
    sub_heads = PROJ_SUB // HEAD_DIM

    def piece(c, ref=w_ref):
        return jnp.dot(h_scr[...], ref[:, c * PROJ_SUB:(c + 1) * PROJ_SUB], preferred_element_type=F32)

    def head(a, hh):
        return a[:, hh * HEAD_DIM:(hh + 1) * HEAD_DIM]

    @pl.when(j < j_kv)
    def _():
        for c in range(subs):
            acc = piece(c)
            accp = piece(c, wp_ref)
            for hh in range(sub_heads):
                col = (c * sub_heads + hh) * HEAD_DIM
                qa_ref[:, col:col + HEAD_DIM] = _norm_rope(
                    head(acc, hh), head(accp, hh), qn_ref[...], cos_ref[...], sin_ref[...]).astype(qa_ref.dtype)

    @pl.when(j == j_kv)
    def _():
        for c in range(subs):
            acc = piece(c)
            roped = (c + 1) * sub_heads <= A_KV_HEADS
            assert roped or c * sub_heads >= A_KV_HEADS
            accp = piece(c, wp_ref) if roped else None
            for hh in range(sub_heads):
                gh = c * sub_heads + hh
                if roped:
                    val = _norm_rope(head(acc, hh), head(accp, hh), kn_ref[...], cos_ref[...], sin_ref[...])
                else:
                    val = head(acc, hh)
                kva_ref[:, gh * HEAD_DIM:(gh + 1) * HEAD_DIM] = val.astype(kva_ref.dtype)

    for gi, ((_, dil), ref) in enumerate(zip(B_PAIRS, (qkv1_ref, qkv2_ref, qkv3_ref))):
        j0 = j_b + 3 * gi

        @pl.when(jnp.logical_and(j >= j0, j < j0 + 3))
        def _(dil=dil, ref=ref, j0=j0):
            scale = jnp.where(j == j0, HEAD_DIM ** -0.5, 1.0).astype(F32)
            for c in range(subs):
                val = piece(c) * scale
                if dil == 1:
                    ref[0, :, c * PROJ_SUB:(c + 1) * PROJ_SUB] = val.astype(ref.dtype)
                    continue
                for hh in range(sub_heads):
                    de_scr[c * sub_heads + hh] = head(val, hh)
                for r in range(dil):
                    for hh in range(sub_heads):
                        gh = c * sub_heads + hh
                        ref[r, :, gh * HEAD_DIM:(gh + 1) * HEAD_DIM] = de_scr[
                            gh, pl.ds(r, tm // dil, stride=dil), :].astype(ref.dtype)

    @pl.when(j >= j_g)
    def _():
        for c in range(subs):
            g_ref[:, c * PROJ_SUB:(c + 1) * PROJ_SUB] = _sigmoid(piece(c)).astype(g_ref.dtype)


def _rope_tables(s):
    half = HEAD_DIM // 4
    inv = np.power(ROPE_THETA, -np.arange(half, dtype=np.float64) / half)
    t = np.arange(s)
    row = (t // GRID_W).astype(np.float64)
    col = (t % GRID_W).astype(np.float64)
    ang_r = row[:, None] * inv[None, :]
    ang_c = col[:, None] * inv[None, :]
    cos = np.concatenate([np.cos(ang_r), np.cos(ang_r), np.cos(ang_c), np.cos(ang_c)], axis=1)
    sin = np.concatenate([-np.sin(ang_r), np.sin(ang_r), -np.sin(ang_c), np.sin(ang_c)], axis=1)
    return jnp.asarray(cos, F32), jnp.asarray(sin, F32)


def _inproj_weight(w_in_l):
    b0 = A_Q + 2 * A_KV
    parts = [w_in_l[:, :b0]]
    for gi in range(B_GROUPS):
        for part in range(3):
            c0 = b0 + part * B_W + gi * B_OUT
            parts.append(w_in_l[:, c0:c0 + B_OUT])
    parts.append(w_in_l[:, b0 + 3 * B_W:])
    return jnp.concatenate(parts, axis=1)


def _inproj(x2, mod3, boff, batch, seq, n1, w_in, w_partner, qn, kn):
    t, d = x2.shape
    tm, tn = PROJ_TM, PROJ_TN
    assert t % tm == 0 and seq % tm == 0
    assert A_Q % tn == 0 and 2 * A_KV == tn and B_OUT == tn and d % tn == 0
    j_kv = A_Q // tn
    j_b = j_kv + 1
    j_g = j_b + 3 * B_GROUPS
    n_j = j_g + 2 * d // tn
    assert w_in.shape == (d, n_j * tn) and w_partner.shape == (d, (j_kv + 1) * tn)
    cos, sin = _rope_tables(seq)
    spb = seq // tm
    kern = functools.partial(_inproj_kernel, j_kv=j_kv, j_b=j_b, j_g=j_g)
    grp_specs, grp_shapes = [], []
    for gi, (_, dil) in enumerate(B_PAIRS):
        assert tm % dil == 0 and (tm // dil) % 16 == 0
        j0 = j_b + 3 * gi
        grp_specs.append(pl.BlockSpec((None, dil, tm // dil, tn),
                                      lambda i, j, j0=j0: (i // spb, 0, i % spb, jnp.clip(j - j0, 0, 2))))
        grp_shapes.append(jax.ShapeDtypeStruct((batch, dil, seq // dil, 3 * tn), MXU_DTYPE))
    return pl.pallas_call(
        kern,
        grid=(t // tm, n_j),
        in_specs=[
            pl.BlockSpec((tm, d), lambda i, j: (i, 0)),
            pl.BlockSpec((None, 6, d), lambda i, j: (boff + i // spb, 0, 0)),
            pl.BlockSpec((1, d), lambda i, j: (0, 0)),
            pl.BlockSpec((d, tn), lambda i, j: (0, j)),
            pl.BlockSpec((d, tn), lambda i, j: (0, jnp.minimum(j, j_kv))),
            pl.BlockSpec((tm, HEAD_DIM), lambda i, j: (i % spb, 0)),
            pl.BlockSpec((tm, HEAD_DIM), lambda i, j: (i % spb, 0)),
            pl.BlockSpec((2, HEAD_DIM), lambda i, j: (0, 0)),
            pl.BlockSpec((2, HEAD_DIM), lambda i, j: (0, 0)),
        ],
        out_specs=[
            pl.BlockSpec((tm, tn), lambda i, j: (i, jnp.clip(j, 0, j_kv - 1))),
            pl.BlockSpec((tm, tn), lambda i, j: (i, 0)),
            *grp_specs,
            pl.BlockSpec((tm, tn), lambda i, j: (i, jnp.clip(j - j_g, 0, n_j - j_g - 1))),
        ],
        out_shape=[
            jax.ShapeDtypeStruct((t, A_Q), MXU_DTYPE),
            jax.ShapeDtypeStruct((t, 2 * A_KV), MXU_DTYPE),
            *grp_shapes,
            jax.ShapeDtypeStruct((t, 2 * d), MXU_DTYPE),
        ],
        scratch_shapes=[pltpu.VMEM((tm, d), MXU_DTYPE), pltpu.VMEM((tn // HEAD_DIM, tm, HEAD_DIM), F32)],
        compiler_params=_params(("arbitrary", "arbitrary")),
        name="inproj",
    )(x2, mod3, n1.reshape(1, d), w_in, w_partner, cos, sin,
      jnp.stack([qn, _rope_partner(qn)]), jnp.stack([kn, _rope_partner(kn)]))


def _attn_a_kernel(q_ref, k_ref, v_ref, o_ref):
    k = k_ref[...]
    v = v_ref[...]

    tq = q_ref.shape[0]
    units = [(slice(r * tq // ATT_ROW_SPLIT, (r + 1) * tq // ATT_ROW_SPLIT),
              slice(hh * HEAD_DIM, (hh + 1) * HEAD_DIM))
             for hh in range(A_GROUP) for r in range(ATT_ROW_SPLIT)]

    def scores(u):
        rows, cols = units[u]
        return lax.dot_general(q_ref[rows, cols], k, (((1,), (1,)), ((), ())), preferred_element_type=F32)

    s_next = scores(0)
    for u, (rows, cols) in enumerate(units):
        s = s_next
        if u + 1 < len(units):
            s_next = scores(u + 1)
        m = jnp.max(s, axis=-1, keepdims=True)
        p = jnp.exp(s - m)
        l = jnp.sum(p, axis=-1, keepdims=True)
        o = jnp.dot(p.astype(v.dtype), v, preferred_element_type=F32)
        o_ref[rows, cols] = (o / l).astype(o_ref.dtype)


def _attn_a(qa, kva, batch, seq):
    t = qa.shape[0]
    tq = min(ATT_TQ, seq)
    nq = seq // tq
    gw = A_GROUP * HEAD_DIM
    return pl.pallas_call(
        _attn_a_kernel,
        grid=(batch, A_KV_HEADS, nq),
        in_specs=[
            pl.BlockSpec((tq, gw), lambda b, kh, qi: (b * nq + qi, kh)),
            pl.BlockSpec((seq, HEAD_DIM), lambda b, kh, qi: (b, kh)),
            pl.BlockSpec((seq, HEAD_DIM), lambda b, kh, qi: (b, A_KV_HEADS + kh)),
        ],
        out_specs=pl.BlockSpec((tq, gw), lambda b, kh, qi: (b * nq + qi, kh)),
        out_shape=jax.ShapeDtypeStruct((t, A_Q), MXU_DTYPE),
        compiler_params=_params(("arbitrary", "arbitrary", "arbitrary")),
        name="attn_dense",
    )(qa, kva, kva)


def _t5_bucket(rel):
    nb = REL_BUCKETS // 2
    ret = (rel > 0).astype(np.int32) * nb
    n = np.abs(rel)
    max_exact = nb // 2
    large = max_exact + (np.log(np.maximum(n, 1) / max_exact) / np.log(REL_MAX_DIST / max_exact)
                         * (nb - max_exact)).astype(np.int32)
    large = np.minimum(large, nb - 1)
    return (ret + np.where(n < max_exact, n, large)).astype(np.int32)


def _band_bias(table_g, dilation, tq, w):
    h = table_g.shape[1]
    offs = np.arange(-DIL_SIDE, DIL_SIDE + 1)
    diag = jnp.take(table_g, jnp.asarray(_t5_bucket(offs * dilation)), axis=0).astype(F32).T
    diag = jnp.concatenate([diag, jnp.full((h, 1), NEG, F32)], axis=1)
    n = tq + w
    u = np.arange(n)
    delta = np.where(u <= w - 1, u, u - n)
    out = []
    for var in range(3):
        m = delta - DIL_SIDE * var
        idx = np.where(np.abs(m) <= DIL_SIDE, m + DIL_SIDE, 2 * DIL_SIDE + 1)
        v = jnp.take(diag, jnp.asarray(idx), axis=1)
        toep = jnp.tile(v, (1, tq))[:, :tq * (n - 1)].reshape(h, tq, n - 1)[:, :, :w]
        out.append(toep)
    return jnp.stack(out, axis=0)


def _dil_kernel(q_ref, k_ref, v_ref, bias_ref, o_ref, lse_ref, *, length, tq, w):
    rp = q_ref.shape[0]
    nblk = length // tq
    lane = lax.broadcasted_iota(jnp.int32, (tq, LANES), 1)

    def blocks(units):
        work = []
        for ri, c in units:
            if isinstance(c, int):
                q0 = c * tq
                ws = min(max(q0 - DIL_SIDE, 0), length - w)
            else:
                q0 = pl.multiple_of(c * tq, tq)
                ws = pl.multiple_of(jnp.clip(q0 - DIL_SIDE, 0, length - w), DIL_SIDE)
            var = (q0 - ws) // DIL_SIDE
            for hh in range(B_HEADS_PER_GROUP):
                sl = slice(hh * HEAD_DIM, (hh + 1) * HEAD_DIM)
                q = q_ref[ri, pl.ds(q0, tq), sl]
                kw = k_ref[ri, pl.ds(ws, w), sl]
                s = lax.dot_general(q, kw, (((1,), (1,)), ((), ())), preferred_element_type=F32)
                work.append(dict(ri=ri, q0=q0, ws=ws, hh=hh, sl=sl, s=s + bias_ref[var, hh]))
        for it in work:
            it['m'] = jnp.max(it['s'], axis=-1, keepdims=True)
            p = jnp.exp(it['s'] - it['m'])
            it['l'] = jnp.sum(p, axis=-1, keepdims=True)
            it['p'] = p.astype(v_ref.dtype)
        for it in work:
            vw = v_ref[it['ri'], pl.ds(it['ws'], w), it['sl']]
            o = jnp.dot(it['p'], vw, preferred_element_type=F32)
            o_ref[it['ri'], pl.ds(it['q0'], tq), it['sl']] = (o / it['l']).astype(o_ref.dtype)
        for u in range(len(units)):
            lse_blk = jnp.zeros((tq, LANES), F32)
            for it in work[u * B_HEADS_PER_GROUP:(u + 1) * B_HEADS_PER_GROUP]:
                lse_blk = jnp.where(lane == it['hh'], it['m'] + jnp.log(it['l']), lse_blk)
            lse_ref[work[u * B_HEADS_PER_GROUP]['ri'], pl.ds(work[u * B_HEADS_PER_GROUP]['q0'], tq), :] = lse_blk

    if rp * nblk <= DIL_UNROLL:
        blocks([(ri, c) for ri in range(rp) for c in range(nblk)])
    else:
        assert rp == 1 and nblk % DIL_UNROLL == 0

        def body(c4, carry):
            blocks([(0, c4 * DIL_UNROLL + j) for j in range(DIL_UNROLL)])
            return carry

        lax.fori_loop(0, nblk // DIL_UNROLL, body, 0)


def _dilated_group(qkv, table_g, window, dilation):
    assert (window // 2) // dilation == DIL_SIDE
    batch, _, length, _ = qkv.shape
    assert length % DIL_SIDE == 0
    tq = min(DIL_TQ, length)
    w = min(length, tq + 2 * DIL_SIDE)
    assert length % tq == 0
    gw = B_HEADS_PER_GROUP * HEAD_DIM
    bias = _band_bias(table_g, dilation, tq, w)
    kern = functools.partial(_dil_kernel, length=length, tq=tq, w=w)
    rp = min(dilation, max(1, DIL_UNROLL // (length // tq)))
    assert dilation % rp == 0
    return pl.pallas_call(
        kern,
        grid=(batch, dilation // rp),
        in_specs=[
            pl.BlockSpec((None, rp, length, gw), lambda b, r: (b, r, 0, 0)),
            pl.BlockSpec((None, rp, length, gw), lambda b, r: (b, r, 0, 1)),
            pl.BlockSpec((None, rp, length, gw), lambda b, r: (b, r, 0, 2)),
            pl.BlockSpec((3, B_HEADS_PER_GROUP, tq, w), lambda b, r: (0, 0, 0, 0)),
        ],
        out_specs=[
            pl.BlockSpec((None, rp, length, gw), lambda b, r: (b, r, 0, 0)),
            pl.BlockSpec((None, rp, length, LANES), lambda b, r: (b, r, 0, 0)),
        ],
        out_shape=[
            jax.ShapeDtypeStruct((batch, dilation, length, gw), MXU_DTYPE),
            jax.ShapeDtypeStruct((batch, dilation, length, LANES), F32),
        ],
        compiler_params=_params(("arbitrary", "arbitrary")),
        name=f"attn_dilated_{dilation}",
    )(qkv, qkv, qkv, bias)


def _post_kernel(x_ref, ya_ref, o1_ref, o2_ref, o3_ref, l1_ref, l2_ref, l3_ref, ga_ref, gb_ref,
                 mod_ref, n2_ref, wa_ref, wb_ref, wo_ref, wr_ref, br_ref,
                 x1_ref, h2_ref, rt_ref, o_scr, l_scr):
    tm = x_ref.shape[0]
    o_refs, l_refs = (o1_ref, o2_ref, o3_ref), (l1_ref, l2_ref, l3_ref)
    for gi, (_, dil) in enumerate(B_PAIRS):
        if dil == 1:
            continue
        for r in range(dil):
            rows = pl.ds(r, tm // dil, stride=dil)
            l_scr[gi, rows, :] = l_refs[gi][r]
            for jh in range(B_HEADS_PER_GROUP):
                o_scr[gi, jh, rows, :] = o_refs[gi][r, :, jh * HEAD_DIM:(jh + 1) * HEAD_DIM].astype(F32)

    def lse_rows(gi, rs):
        return l_refs[gi][0, rs, :] if B_PAIRS[gi][1] == 1 else l_scr[gi, rs, :]

    def out_rows(gi, jh, rs):
        if B_PAIRS[gi][1] == 1:
            return o_refs[gi][0, rs, jh * HEAD_DIM:(jh + 1) * HEAD_DIM].astype(F32)
        return o_scr[gi, jh, rs, :]

    groups = [dict(rs=slice(g * tm // POST_SPLIT, (g + 1) * tm // POST_SPLIT)) for g in range(POST_SPLIT)]
    for it in groups:
        rs = it['rs']
        lses = [lse_rows(gi, rs) for gi in range(B_GROUPS)]
        parts = []
        for jh in range(B_HEADS_PER_GROUP):
            lj = [lg[:, jh:jh + 1] for lg in lses]
            mx = jnp.maximum(jnp.maximum(lj[0], lj[1]), lj[2])
            ex = [jnp.exp(v - mx) for v in lj]
            inv = 1.0 / (ex[0] + ex[1] + ex[2])
            acc = (ex[0] * inv) * out_rows(0, jh, rs)
            acc = acc + (ex[1] * inv) * out_rows(1, jh, rs)
            acc = acc + (ex[2] * inv) * out_rows(2, jh, rs)
            parts.append(acc.astype(MXU_DTYPE))
        it['yb'] = jnp.concatenate(parts, axis=1)
    for it in groups:
        it['a'] = jnp.dot(ya_ref[it['rs'], :], wa_ref[...], preferred_element_type=F32)
        it['bb'] = jnp.dot(it['yb'], wb_ref[...], preferred_element_type=F32)
    for it in groups:
        rs = it['rs']
        merged = ga_ref[rs, :].astype(F32) * it['a'] + gb_ref[rs, :].astype(F32) * it['bb']
        it['merged'] = merged.astype(MXU_DTYPE)
    for it in groups:
        it['out'] = jnp.dot(it['merged'], wo_ref[...], preferred_element_type=F32)
    for it in groups:
        rs = it['rs']
        x1 = x_ref[rs, :] + mod_ref[2:3, :] * it['out']
        x1_ref[rs, :] = x1
        h2 = _rms(x1) * n2_ref[...]
        h2 = h2 * (1.0 + mod_ref[4:5, :]) + mod_ref[3:4, :]
        h2_ref[rs, :] = _pack_bf16_pairs(h2)
        it['h_hi'] = h2.astype(MXU_DTYPE)
        it['h_lo'] = (h2 - it['h_hi'].astype(F32)).astype(MXU_DTYPE)
    for it in groups:
        r_hi = jnp.dot(it['h_hi'], wr_ref[...], preferred_element_type=F32)
        r_lo = jnp.dot(it['h_lo'], wr_ref[:, :LANES], preferred_element_type=F32)
        it['r'] = r_hi[:, :LANES] + r_hi[:, LANES:] + r_lo + br_ref[...]
    for it in groups:
        rt_ref[it['rs'], :] = _route(it['r'])


def _route(r):
    lane = lax.broadcasted_iota(jnp.int32, r.shape, 1)
    lanef = lane.astype(F32)
    ninf = -jnp.inf
    gmask = lane < N_GROUPS
    gl = jnp.where(gmask, r, ninf)
    gmax = jnp.max(gl, axis=-1, keepdims=True)
    grp = jnp.min(jnp.where(gl == gmax, lanef, float(LANES)), axis=-1, keepdims=True)
    gsum = jnp.sum(jnp.where(gmask, jnp.exp(jnp.where(gmask, r, gmax) - gmax), 0.0), axis=-1, keepdims=True)
    gate = 1.0 / gsum
    lo = N_GROUPS + grp * EXPERTS_PER_GROUP
    emask = jnp.logical_and(lanef >= lo, lanef < lo + EXPERTS_PER_GROUP)
    el = jnp.where(emask, r, ninf)
    v1 = jnp.max(el, axis=-1, keepdims=True)
    i1 = jnp.min(jnp.where(el == v1, lanef, float(LANES)), axis=-1, keepdims=True)
    el2 = jnp.where(lanef == i1, ninf, el)
    v2 = jnp.max(el2, axis=-1, keepdims=True)
    i2 = jnp.min(jnp.where(el2 == v2, lanef, float(LANES)), axis=-1, keepdims=True)
    tt = jnp.exp(v2 - v1)
    w1 = 1.0 / (1.0 + tt)
    w2 = tt * w1
    return jnp.where(lane == 0, i1 - N_GROUPS,
                     jnp.where(lane == 1, i2 - N_GROUPS,
                               jnp.where(lane == 2, gate * w1,
                                         jnp.where(lane == 3, gate * w2, 0.0))))


def _post(x2, ya, obs, lses, gates, mod3, boff, seq, n2, wa, wb, wo, wr, br):
    t, d = x2.shape
    tm = POST_TM
    assert t % tm == 0 and seq % tm == 0
    spb = seq // tm
    row = lambda i: (i, 0)
    const = lambda i: (0, 0)
    res = lambda i: (i // spb, 0, i % spb, 0)
    o_specs, l_specs = [], []
    for _, dil in B_PAIRS:
        assert tm % dil == 0 and (tm // dil) % 16 == 0
        o_specs.append(pl.BlockSpec((None, dil, tm // dil, B_OUT), res))
        l_specs.append(pl.BlockSpec((None, dil, tm // dil, LANES), res))
    return pl.pallas_call(
        _post_kernel,
        grid=(t // tm,),
        in_specs=[
            pl.BlockSpec((tm, d), row),
            pl.BlockSpec((tm, A_Q), row),
            *o_specs,
            *l_specs,
            pl.BlockSpec((tm, d), lambda i: (i, 0)),
            pl.BlockSpec((tm, d), lambda i: (i, 1)),
            pl.BlockSpec((None, 6, d), lambda i: (boff + i // spb, 0, 0)),
            pl.BlockSpec((1, d), const),
            pl.BlockSpec((A_Q, d), const, pipeline_mode=pl.Buffered(1)),
            pl.BlockSpec((B_OUT, d), const, pipeline_mode=pl.Buffered(1)),
            pl.BlockSpec((d, d), const, pipeline_mode=pl.Buffered(1)),
            pl.BlockSpec((d, 2 * LANES), const, pipeline_mode=pl.Buffered(1)),
            pl.BlockSpec((1, LANES), const),
        ],
        out_specs=[pl.BlockSpec((tm, d), row), pl.BlockSpec((tm, d // 2), row), pl.BlockSpec((tm, LANES), row)],
        out_shape=[jax.ShapeDtypeStruct((t, d), F32), jax.ShapeDtypeStruct((t, d // 2), jnp.uint32),
                   jax.ShapeDtypeStruct((t, LANES), F32)],
        scratch_shapes=[pltpu.VMEM((B_GROUPS, B_HEADS_PER_GROUP, tm, HEAD_DIM), F32),
                        pltpu.VMEM((B_GROUPS, tm, LANES), F32)],
        compiler_params=_params(("arbitrary",)),
        name="post_attn",
    )(x2, ya, obs[0], obs[1], obs[2], lses[0], lses[1], lses[2], gates, gates, mod3, n2.reshape(1, d),
      wa, wb, wo, wr, br)


def _slot_plan(experts):
    a = experts.shape[0]
    c = RANK_CHUNK
    assert a % c == 0 and a % MOE_TM == 0
    oh = (experts[:, None] == jnp.arange(N_EXPERTS, dtype=jnp.int32)[None, :])
    oh3 = oh.astype(jnp.bfloat16).reshape(a // c, c, N_EXPERTS)
    tri = (np.arange(c)[:, None] > np.arange(c)[None, :])
    within = jnp.einsum('ij,cjk->cik', jnp.asarray(tri, jnp.bfloat16), oh3, preferred_element_type=F32)
    tot = jnp.sum(oh3.astype(F32), axis=1)
    off = jnp.cumsum(tot, axis=0) - tot
    counts = jnp.sum(tot, axis=0).astype(jnp.int32)
    padded = (counts + MOE_TM - 1) // MOE_TM * MOE_TM
    pend = jnp.cumsum(padded)
    pstart = pend - padded
    slot = within + off[:, None, :] + pstart.astype(F32)[None, None, :]
    dest = jnp.sum(jnp.where(oh.reshape(a // c, c, N_EXPERTS), slot, 0.0), axis=-1).reshape(a).astype(jnp.int32)
    n_blocks = a // MOE_TM + N_EXPERTS
    idx = jnp.arange(n_blocks, dtype=jnp.int32)
    block_start = idx * MOE_TM
    block_e = jnp.minimum(jnp.sum((pend[None, :] <= block_start[:, None]).astype(jnp.int32), axis=1),
                          N_EXPERTS - 1)
    nb_used = pend[-1] // MOE_TM
    used = idx < nb_used
    last = nb_used - 1
    onehot_e = block_e[:, None] == jnp.arange(N_EXPERTS, dtype=jnp.int32)[None, :]
    seg_end = jnp.sum(jnp.where(onehot_e, (pstart + counts)[None, :], 0), axis=1)
    nv = jnp.where(used, jnp.clip(seg_end - block_start, 0, MOE_TM), 0).astype(jnp.int32)
    def at_expert(table):
        return jnp.sum(jnp.where(onehot_e, table[None, :], 0), axis=1)

    nblk_e = padded // MOE_TM
    pos_in_e = idx - at_expert(pstart // MOE_TM)
    local = pos_in_e % MOE_PAIR
    pair_size = jnp.minimum(MOE_PAIR, at_expert(nblk_e) - (pos_in_e - local))
    pairs_e = (nblk_e + MOE_PAIR - 1) // MOE_PAIR
    pair_id = at_expert(jnp.cumsum(pairs_e) - pairs_e) + pos_in_e // MOE_PAIR
    f_first = pair_id & 1
    base = 2 * (idx - local)
    pos_a = jnp.where(used, base + local, 2 * idx)
    pos_b = jnp.where(used, base + 2 * pair_size - 1 - local, 2 * idx + 1)
    be_last = jnp.sum(jnp.where(idx == last, block_e, 0))
    e_col = jnp.where(used, block_e, be_last)
    x_col = jnp.minimum(idx, last)
    slot_col = jnp.where(used, local, 0)
    o_first = jnp.where(used, idx - local + pair_size - 1, idx)
    zeros = jnp.zeros_like(idx)
    flag_a = jnp.where(used, EXP_FIRST, EXP_ZERO) + EXP_SLOT * slot_col
    flag_b = jnp.where(used, EXP_LAST, 0) + EXP_SLOT * slot_col
    rows_a = jnp.stack([x_col, e_col, jnp.where(used, f_first, zeros), nv, flag_a, o_first], axis=1)
    rows_b = jnp.stack([x_col, e_col, jnp.where(used, 1 - f_first, zeros), nv, flag_b, idx], axis=1)
    steps = jnp.zeros((2 * n_blocks, 6), jnp.int32).at[jnp.concatenate([pos_a, pos_b])].set(
        jnp.concatenate([rows_a, rows_b]).astype(jnp.int32))
    step_idx = jnp.arange(2 * n_blocks, dtype=jnp.int32)
    f_last = jnp.sum(jnp.where(step_idx == 2 * nb_used - 1, steps[:, 2], 0))
    f_col = jnp.where(step_idx < 2 * nb_used, steps[:, 2], f_last)
    plan = dict(xblk=steps[:, 0], expert=steps[:, 1], fhalf=f_col, nv=steps[:, 3], flags=steps[:, 4],
                oblk=steps[:, 5], nv_block=nv)
    return dest, plan, n_blocks


def _dispatch_kernel(nvb_ref, dest_ref, ha_ref, hb_ref, xs_ref, zbuf, sem, zsem, *, steps_a):
    i = pl.program_id(0)
    tm = ha_ref.shape[0]
    n_blocks = nvb_ref.shape[0]

    def zero_block(b):
        return pltpu.make_async_copy(zbuf, xs_ref.at[pl.ds(pl.multiple_of(b * MOE_TM, MOE_TM), MOE_TM)], zsem)

    @pl.when(i == 0)
    def _():
        zbuf[...] = jnp.zeros_like(zbuf)

        def zstart(b, carry):
            pl.when(nvb_ref[b] < MOE_TM)(lambda: zero_block(b).start())
            return carry

        def zwait(b, carry):
            pl.when(nvb_ref[b] < MOE_TM)(lambda: zero_block(b).wait())
            return carry

        lax.fori_loop(0, n_blocks, zstart, 0)
        lax.fori_loop(0, n_blocks, zwait, 0)

    def scatter(h_ref):
        def start(t, carry):
            for k in range(TOP_K):
                d = dest_ref[0, TOP_K * t + k]
                pltpu.make_async_copy(h_ref.at[pl.ds(t, 1)], xs_ref.at[pl.ds(d, 1)], sem).start(priority=k % 2)
            return carry

        lax.fori_loop(0, tm, start, 0, unroll=8)
        for _ in range(TOP_K):
            pltpu.make_async_copy(h_ref, xs_ref.at[pl.ds(0, tm)], sem).wait()

    pl.when(i < steps_a)(lambda: scatter(ha_ref))
    pl.when(i >= steps_a)(lambda: scatter(hb_ref))


def _dispatch(dest, nv_block, h_a, h_b):
    ta, d = h_a.shape
    tb = h_b.shape[0]
    tm = DISP_TM
    assert ta % tm == 0 and tb % tm == 0
    steps_a, steps_b = ta // tm, tb // tm
    n_slots = nv_block.shape[0] * MOE_TM
    grid_spec = pltpu.PrefetchScalarGridSpec(
        num_scalar_prefetch=1,
        grid=(steps_a + steps_b,),
        in_specs=[
            pl.BlockSpec((None, 1, TOP_K * tm), lambda i, nvb: (i, 0, 0), memory_space=pltpu.SMEM),
            pl.BlockSpec((tm, d), lambda i, nvb: (jnp.minimum(i, steps_a - 1), 0)),
            pl.BlockSpec((tm, d), lambda i, nvb: (jnp.maximum(i - steps_a, 0), 0)),
        ],
        out_specs=pl.BlockSpec(memory_space=pl.ANY),
        scratch_shapes=[pltpu.VMEM((MOE_TM, d), h_a.dtype), pltpu.SemaphoreType.DMA(()),
                        pltpu.SemaphoreType.DMA(())],
    )
    return pl.pallas_call(
        functools.partial(_dispatch_kernel, steps_a=steps_a),
        grid_spec=grid_spec,
        out_shape=jax.ShapeDtypeStruct((n_slots, d), h_a.dtype),
        compiler_params=_params(("arbitrary",)),
        name="moe_dispatch",
    )(nv_block, dest.reshape(steps_a + steps_b, 1, TOP_K * tm), h_a, h_b)


def _expert_kernel(xb_ref, e_ref, fh_ref, nv_ref, fl_ref, ob_ref, x_ref, wg_ref, wu_ref, wd_ref, y_ref,
                   wg_s, wu_s, wd_s, acc):
    del xb_ref, e_ref, fh_ref, ob_ref
    s = pl.program_id(0)
    nv = nv_ref[s]
    flags = fl_ref[s]
    first = (flags & EXP_FIRST) > 0
    slot = (flags // EXP_SLOT) & 3
    tm = x_ref.shape[0]

    @pl.when(s == 0)
    def _():
        acc[...] = jnp.zeros_like(acc)

    def compute(n_rows):
        wg_s[...] = wg_ref[...].astype(wg_s.dtype)
        wu_s[...] = wu_ref[...].astype(wu_s.dtype)
        wd_s[...] = wd_ref[...].astype(wd_s.dtype)
        x = _unpack_bf16_pairs(x_ref[0:n_rows, :]).astype(MXU_DTYPE)
        g = jnp.dot(x, wg_s[...], preferred_element_type=F32)
        u = jnp.dot(x, wu_s[...], preferred_element_type=F32)
        h = (g * _sigmoid(g)) * u
        part = jnp.dot(h.astype(MXU_DTYPE), wd_s[...], preferred_element_type=F32)

        total = jnp.where(first, part, acc[slot, 0:n_rows, :] + part)
        acc[slot, 0:n_rows, :] = total
        y_ref[0:n_rows, :] = _pack_bf16_pairs(total)
        if n_rows < tm:
            y_ref[n_rows:tm, :] = jnp.zeros((tm - n_rows, y_ref.shape[1]), y_ref.dtype)

    for n_rows in range(MOE_SUB, tm + 1, MOE_SUB):
        pl.when(jnp.logical_and(nv > n_rows - MOE_SUB, nv <= n_rows))(functools.partial(compute, n_rows))

    @pl.when((flags & EXP_ZERO) > 0)
    def _():
        y_ref[...] = jnp.zeros_like(y_ref)


def _experts(xs, plan, wg, wu, wd):
    n_slots = xs.shape[0]
    _, d, ff = wg.shape
    tm = MOE_TM
    n_blocks = n_slots // tm
    assert ff % 2 == 0 and xs.shape[1] * 2 == d and 1 <= MOE_PAIR <= 4
    fh = ff // 2
    grid_spec = pltpu.PrefetchScalarGridSpec(
        num_scalar_prefetch=6,
        grid=(2 * n_blocks,),
        in_specs=[
            pl.BlockSpec((tm, d // 2), lambda s, xb, e, fhalf, nv, fl, ob: (xb[s], 0)),
            pl.BlockSpec((None, d, fh), lambda s, xb, e, fhalf, nv, fl, ob: (e[s], 0, fhalf[s])),
            pl.BlockSpec((None, d, fh), lambda s, xb, e, fhalf, nv, fl, ob: (e[s], 0, fhalf[s])),
            pl.BlockSpec((None, fh, d), lambda s, xb, e, fhalf, nv, fl, ob: (e[s], fhalf[s], 0)),
        ],
        out_specs=pl.BlockSpec((tm, d // 2), lambda s, xb, e, fhalf, nv, fl, ob: (ob[s], 0)),
        scratch_shapes=[pltpu.VMEM((d, fh), MXU_DTYPE), pltpu.VMEM((d, fh), MXU_DTYPE),
                        pltpu.VMEM((fh, d), MXU_DTYPE), pltpu.VMEM((MOE_PAIR, tm, d), F32)],
    )
    return pl.pallas_call(
        _expert_kernel,
        grid_spec=grid_spec,
        out_shape=jax.ShapeDtypeStruct((n_slots, d // 2), jnp.uint32),
        compiler_params=_params(("arbitrary",), MOE_VMEM_LIMIT_BYTES),
        name="moe_experts",
    )(plan['xblk'], plan['expert'], plan['fhalf'], plan['nv'], plan['flags'], plan['oblk'], xs, wg, wu, wd)


def _combine_kernel(dest_ref, dest_next_ref, x1_ref, rt_ref, mod_ref, nf_ref, ys_ref, o_ref, ybuf, sem, *, final):
    i = pl.program_id(0)
    tm = x1_ref.shape[0]
    slot = i % 2

    def issue(idx_ref, s):
        def start(t, carry):
            for k in range(TOP_K):
                d = idx_ref[0, TOP_K * t + k]
                pltpu.make_async_copy(ys_ref.at[pl.ds(d, 1)], ybuf.at[s, pl.ds(k * tm + t, 1)],
                                      sem.at[s]).start(priority=k % 2)
            return carry

        lax.fori_loop(0, tm, start, 0, unroll=8)

    @pl.when(i == 0)
    def _():
        issue(dest_ref, 0)

    @pl.when(i + 1 < pl.num_programs(0))
    def _():
        issue(dest_next_ref, 1 - slot)

    for k in range(TOP_K):
        pltpu.make_async_copy(ys_ref.at[pl.ds(0, tm)], ybuf.at[slot, pl.ds(k * tm, tm)], sem.at[slot]).wait()
    rt = rt_ref[...]
    moe = (_unpack_bf16_pairs(ybuf[slot, 0:tm, :]) * rt[:, 2:3]
           + _unpack_bf16_pairs(ybuf[slot, tm:2 * tm, :]) * rt[:, 3:4])
    x2 = x1_ref[...] + mod_ref[5:6, :] * moe
    if final:
        x2 = _rms(x2) * nf_ref[...]
    o_ref[...] = x2


def _combine(dest_blocks, x1, rt, mod3, boff, seq, nf, ys, final):
    t, d = x1.shape
    tm = COMB_TM
    spb = seq // tm
    kern = functools.partial(_combine_kernel, final=final)
    n_steps = t // tm
    return pl.pallas_call(
        kern,
        grid=(n_steps,),
        in_specs=[
            pl.BlockSpec((None, 1, TOP_K * tm), lambda i: (i, 0, 0), memory_space=pltpu.SMEM),
            pl.BlockSpec((None, 1, TOP_K * tm), lambda i: (jnp.minimum(i + 1, n_steps - 1), 0, 0),
                         memory_space=pltpu.SMEM),
            pl.BlockSpec((tm, d), lambda i: (i, 0)),
            pl.BlockSpec((tm, LANES), lambda i: (i, 0)),
            pl.BlockSpec((None, 6, d), lambda i: (boff + i // spb, 0, 0)),
            pl.BlockSpec((1, d), lambda i: (0, 0)),
            pl.BlockSpec(memory_space=pl.ANY),
        ],
        out_specs=pl.BlockSpec((tm, d), lambda i: (i, 0)),
        out_shape=jax.ShapeDtypeStruct((t, d), F32),
        scratch_shapes=[pltpu.VMEM((2, TOP_K * tm, d // 2), jnp.uint32), pltpu.SemaphoreType.DMA((2,))],
        compiler_params=_params(("arbitrary",)),
        name="moe_combine",
    )(dest_blocks, dest_blocks, x1, rt, mod3, nf.reshape(1, d), ys)


def kernel(x_prompt, x_sample, c_prompt, c_sample, w_ada, b_ada, norm1, w_in, q_norm, k_norm, w_branch_a, w_branch_b, w_out, rel_bias, norm2, w_route_group, b_route_group, w_route_expert, b_route_expert, w_gate_exp, w_up_exp, w_down_exp, norm_final):
    depth = w_ada.shape[0]
    d = x_prompt.shape[-1]
    trunks = []
    boff = 0
    for x, c in ((x_prompt, c_prompt), (x_sample, c_sample)):
        b, s, _ = x.shape
        trunks.append(dict(x=x.reshape(b * s, d), batch=b, seq=s, boff=boff))
        boff += b
    n_cond = boff
    rows = -(-n_cond // 8) * 8
    c_all = jnp.concatenate([c_prompt, c_sample, jnp.zeros((rows - n_cond, d), F32)], axis=0)
    scale = HEAD_DIM ** -0.5

    for l in range(depth):
        mod3 = _adaln(c_all, w_ada[l], b_ada[l]).reshape(rows, 6, d)
        w_in_l = _inproj_weight(w_in[l].astype(MXU_DTYPE))
        w_partner = _rope_partner(w_in_l[:, :A_Q + 2 * A_KV])
        wa = w_branch_a[l].astype(MXU_DTYPE)
        wb = w_branch_b[l].astype(MXU_DTYPE)
        wo = w_out[l].astype(MXU_DTYPE)
        wr = jnp.concatenate([w_route_group[l], w_route_expert[l],
                              jnp.zeros((d, LANES - N_GROUPS - N_EXPERTS), F32)], axis=1)
        wr_hi = wr.astype(MXU_DTYPE)
        wr = jnp.concatenate([wr_hi, (wr - wr_hi.astype(F32)).astype(MXU_DTYPE)], axis=1)
        br = jnp.concatenate([b_route_group[l], b_route_expert[l],
                              jnp.zeros((LANES - N_GROUPS - N_EXPERTS,), F32)]).reshape(1, LANES)

        for tr in trunks:
            qa, kva, *qkv_groups, gates = _inproj(tr['x'], mod3, tr['boff'], tr['batch'], tr['seq'], norm1[l],
                                                  w_in_l, w_partner, q_norm[l] * scale, k_norm[l])
            ya = _attn_a(qa, kva, tr['batch'], tr['seq'])
            obs, lses = [], []
            for gi, (window, dilation) in enumerate(B_PAIRS):
                hs = slice(gi * B_HEADS_PER_GROUP, (gi + 1) * B_HEADS_PER_GROUP)
                o, lse = _dilated_group(qkv_groups[gi], rel_bias[:, hs], window, dilation)
                obs.append(o)
                lses.append(lse)
            tr['x1'], tr['h2'], tr['rt'] = _post(tr['x'], ya, obs, lses, gates, mod3, tr['boff'], tr['seq'],
                                                 norm2[l], wa, wb, wo, wr, br)

        experts = jnp.concatenate([tr['rt'][:, :TOP_K] for tr in trunks], axis=0).astype(jnp.int32).reshape(-1)
        dest, plan, n_blocks = _slot_plan(experts)
        a0 = 0
        for tr in trunks:
            n_tok = tr['x'].shape[0]
            tr['dest'] = dest[a0:a0 + TOP_K * n_tok]
            a0 += TOP_K * n_tok
        xs = _dispatch(dest, plan['nv_block'], trunks[0]['h2'], trunks[1]['h2'])
        ys = _experts(xs, plan, w_gate_exp[l], w_up_exp[l], w_down_exp[l])
        for tr in trunks:
            n_tok = tr['x'].shape[0]
            tr['x'] = _combine(tr['dest'].reshape(n_tok // COMB_TM, 1, TOP_K * COMB_TM), tr['x1'], tr['rt'],
                               mod3, tr['boff'], tr['seq'], norm_final, ys, final=(l == depth - 1))

    return tuple(tr['x'].reshape(tr['batch'], tr['seq'], d) for tr in trunks)
```

```python
import functools
import math

import numpy as np
import jax
import jax.numpy as jnp
from jax import lax
from jax.experimental import pallas as pl
from jax.experimental.pallas import tpu as pltpu

F32 = jnp.float32
MXU_DTYPE = jnp.bfloat16

HEAD_DIM = 128
A_HEADS = 8
A_KV_HEADS = 2
A_GROUP = A_HEADS // A_KV_HEADS
ROPE_THETA = 10000.0
GRID_W = 64
B_PAIRS = ((128, 1), (512, 4), (2048, 16))
B_GROUPS = len(B_PAIRS)
B_HEADS_PER_GROUP = 4
B_HEADS = B_GROUPS * B_HEADS_PER_GROUP
REL_BUCKETS = 32
REL_MAX_DIST = 1024
A_Q = A_HEADS * HEAD_DIM
A_KV = A_KV_HEADS * HEAD_DIM
B_W = B_HEADS * HEAD_DIM
B_OUT = B_HEADS_PER_GROUP * HEAD_DIM
N_GROUPS = 8
EXPERTS_PER_GROUP = 8
N_EXPERTS = N_GROUPS * EXPERTS_PER_GROUP
TOP_K = 2
EPS = 1e-6
NEG = -1e30
LANES = 128

VMEM_LIMIT_BYTES = 56 * 1024 * 1024
MOE_VMEM_LIMIT_BYTES = 62 * 1024 * 1024

PROJ_TM = 1024
PROJ_TN = 512
PROJ_SUB = 256
PROJ_ROW_GROUPS = 4
ATT_TQ = 512
ATT_ROW_SPLIT = 4
DIL_TQ = 128
DIL_SIDE = 64
DIL_UNROLL = 4
POST_TM = 256
POST_SPLIT = 2
MOE_TM = 512
MOE_SUB = 128
MOE_PAIR = 4
EXP_FIRST = 1
EXP_LAST = 2
EXP_ZERO = 4
EXP_SLOT = 8
DISP_TM = 1024
COMB_TM = 512
RANK_CHUNK = 512


def _params(sem, vmem_limit=VMEM_LIMIT_BYTES):
    return pltpu.CompilerParams(dimension_semantics=sem, vmem_limit_bytes=vmem_limit)


def _ada_kernel(c_ref, w_ref, b_ref, o_ref):
    c = c_ref[...]
    a = (c * jax.nn.sigmoid(c)).astype(MXU_DTYPE)
    o_ref[...] = jnp.dot(a, w_ref[...].astype(MXU_DTYPE), preferred_element_type=F32) + b_ref[...]


def _adaln(c_all, w, b):
    r, d = c_all.shape
    n = w.shape[1]
    tn = min(n, 1024)
    return pl.pallas_call(
        _ada_kernel,
        grid=(n // tn,),
        in_specs=[pl.BlockSpec((r, d), lambda j: (0, 0)),
                  pl.BlockSpec((d, tn), lambda j: (0, j)),
                  pl.BlockSpec((1, tn), lambda j: (0, j))],
        out_specs=pl.BlockSpec((r, tn), lambda j: (0, j)),
        out_shape=jax.ShapeDtypeStruct((r, n), F32),
        compiler_params=_params(("arbitrary",)),
        name="adaln",
    )(c_all, w, b.reshape(1, n))


def _rms(x):
    return x * lax.rsqrt(jnp.mean(x * x, axis=-1, keepdims=True) + EPS)


def _sigmoid(x):
    return 0.5 * jnp.tanh(0.5 * x) + 0.5


def _pack_bf16_pairs(x):
    n = x.shape[1] // 2
    lo = pltpu.bitcast(x[:, :n].astype(jnp.bfloat16).astype(F32), jnp.uint32)
    hi = pltpu.bitcast(x[:, n:].astype(jnp.bfloat16).astype(F32), jnp.uint32)
    return (hi & jnp.uint32(0xFFFF0000)) | (lo >> 16)


def _unpack_bf16_pairs(u):
    lo = pltpu.bitcast(u << 16, F32)
    hi = pltpu.bitcast(u & jnp.uint32(0xFFFF0000), F32)
    return jnp.concatenate([lo, hi], axis=1)


def _rope_partner(a):
    lead = a.shape[:-1]
    quarter = HEAD_DIM // 4
    a5 = a.reshape(*lead, a.shape[-1] // HEAD_DIM, 2, 2, quarter)
    return a5[..., ::-1, :].reshape(a.shape)


def _norm_rope(x, xp, g2, cos, sin):
    r = lax.rsqrt(jnp.mean(x * x, axis=-1, keepdims=True) + EPS)
    return (x * (cos * g2[0:1, :]) + xp * (sin * g2[1:2, :])) * r


def _inproj_kernel(x_ref, mod_ref, n1_ref, w_ref, wp_ref, cos_ref, sin_ref, qn_ref, kn_ref,
                   qa_ref, kva_ref, qkv1_ref, qkv2_ref, qkv3_ref, g_ref, h_scr, de_scr, *, j_kv, j_b, j_g):
    j = pl.program_id(1)
    tm = x_ref.shape[0]

    subs = PROJ_TN // PROJ_SUB
    sub_heads = PROJ_SUB // HEAD_DIM

    def piece(c, ref=w_ref, rows=slice(None)):
        return jnp.dot(h_scr[rows, :], ref[:, c * PROJ_SUB:(c + 1) * PROJ_SUB], preferred_element_type=F32)

    def head(a, hh):
        return a[:, hh * HEAD_DIM:(hh + 1) * HEAD_DIM]

    def q_pieces(rows):
        for c in range(subs):
            acc = piece(c, w_ref, rows)
            accp = piece(c, wp_ref, rows)
            for hh in range(sub_heads):
                col = (c * sub_heads + hh) * HEAD_DIM
                qa_ref[rows, col:col + HEAD_DIM] = _norm_rope(
                    head(acc, hh), head(accp, hh), qn_ref[...], cos_ref[rows, :], sin_ref[rows, :]
                ).astype(qa_ref.dtype)

    @pl.when(j == 0)
    def _():
        for g in range(PROJ_ROW_GROUPS):
            rows = slice(g * tm // PROJ_ROW_GROUPS, (g + 1) * tm // PROJ_ROW_GROUPS)
            h = _rms(x_ref[rows, :]) * n1_ref[...]
            h = h * (1.0 + mod_ref[1:2, :]) + mod_ref[0:1, :]
            h_scr[rows, :] = h.astype(h_scr.dtype)
            q_pieces(rows)

    @pl.when(jnp.logical_and(j > 0, j < j_kv))
    def _():
        q_pieces(slice(None))

    @pl.when(j == j_kv)
    def _():
        for c in range(subs):
            acc = piece(c)
            roped = (c + 1) * sub_heads <= A_KV_HEADS
            assert roped or c * sub_heads >= A_KV_HEADS
            accp = piece(c, wp_ref) if roped else None
            for hh in range(sub_heads):
                gh = c * sub_heads + hh
                if roped:
                    val = _norm_rope(head(acc, hh), head(accp, hh), kn_ref[...], cos_ref[...], sin_ref[...])
                else:
                    val = head(acc, hh)
                kva_ref[:, gh * HEAD_DIM:(gh + 1) * HEAD_DIM] = val.astype(kva_ref.dtype)

    for gi, ((_, dil), ref) in enumerate(zip(B_PAIRS, (qkv1_ref, qkv2_ref, qkv3_ref))):
        j0 = j_b + 3 * gi

        @pl.when(jnp.logical_and(j >= j0, j < j0 + 3))
        def _(dil=dil, ref=ref, j0=j0):
            scale = jnp.where(j == j0, HEAD_DIM ** -0.5, 1.0).astype(F32)
            for c in range(subs):
                val = piece(c) * scale
                if dil == 1:
                    ref[0, :, c * PROJ_SUB:(c + 1) * PROJ_SUB] = val.astype(ref.dtype)
                    continue
                for hh in range(sub_heads):
                    de_scr[c * sub_heads + hh] = head(val, hh)
                for r in range(dil):
                    for hh in range(sub_heads):
                        gh = c * sub_heads + hh
                        ref[r, :, gh * HEAD_DIM:(gh + 1) * HEAD_DIM] = de_scr[
                            gh, pl.ds(r, tm // dil, stride=dil), :].astype(ref.dtype)

    @pl.when(j >= j_g)
    def _():
        for c in range(subs):
            g_ref[:, c * PROJ_SUB:(c + 1) * PROJ_SUB] = _sigmoid(piece(c)).astype(g_ref.dtype)


def _rope_tables(s):
    half = HEAD_DIM // 4
    inv = np.power(ROPE_THETA, -np.arange(half, dtype=np.float64) / half)
    t = np.arange(s)
    row = (t // GRID_W).astype(np.float64)
    col = (t % GRID_W).astype(np.float64)
    ang_r = row[:, None] * inv[None, :]
    ang_c = col[:, None] * inv[None, :]
    cos = np.concatenate([np.cos(ang_r), np.cos(ang_r), np.cos(ang_c), np.cos(ang_c)], axis=1)
    sin = np.concatenate([-np.sin(ang_r), np.sin(ang_r), -np.sin(ang_c), np.sin(ang_c)], axis=1)
    return jnp.asarray(cos, F32), jnp.asarray(sin, F32)


def _inproj_weight(w_in_l):
    b0 = A_Q + 2 * A_KV
    parts = [w_in_l[:, :b0]]
    for gi in range(B_GROUPS):
        for part in range(3):
            c0 = b0 + part * B_W + gi * B_OUT
            parts.append(w_in_l[:, c0:c0 + B_OUT])
    parts.append(w_in_l[:, b0 + 3 * B_W:])
    return jnp.concatenate(parts, axis=1)


def _inproj(x2, mod3, boff, batch, seq, n1, w_in, w_partner, qn, kn):
    t, d = x2.shape
    tm, tn = PROJ_TM, PROJ_TN
    assert t % tm == 0 and seq % tm == 0
    assert A_Q % tn == 0 and 2 * A_KV == tn and B_OUT == tn and d % tn == 0
    j_kv = A_Q // tn
    j_b = j_kv + 1
    j_g = j_b + 3 * B_GROUPS
    n_j = j_g + 2 * d // tn
    assert w_in.shape == (d, n_j * tn) and w_partner.shape == (d, (j_kv + 1) * tn)
    cos, sin = _rope_tables(seq)
    spb = seq // tm
    kern = functools.partial(_inproj_kernel, j_kv=j_kv, j_b=j_b, j_g=j_g)
    grp_specs, grp_shapes = [], []
    for gi, (_, dil) in enumerate(B_PAIRS):
        assert tm % dil == 0 and (tm // dil) % 16 == 0
        j0 = j_b + 3 * gi
        grp_specs.append(pl.BlockSpec((None, dil, tm // dil, tn),
                                      lambda i, j, j0=j0: (i // spb, 0, i % spb, jnp.clip(j - j0, 0, 2))))
        grp_shapes.append(jax.ShapeDtypeStruct((batch, dil, seq // dil, 3 * tn), MXU_DTYPE))
    return pl.pallas_call(
        kern,
        grid=(t // tm, n_j),
        in_specs=[
            pl.BlockSpec((tm, d), lambda i, j: (i, 0)),
            pl.BlockSpec((None, 6, d), lambda i, j: (boff + i // spb, 0, 0)),
            pl.BlockSpec((1, d), lambda i, j: (0, 0)),
            pl.BlockSpec((d, tn), lambda i, j: (0, j)),
            pl.BlockSpec((d, tn), lambda i, j: (0, jnp.minimum(j, j_kv))),
            pl.BlockSpec((tm, HEAD_DIM), lambda i, j: (i % spb, 0)),
            pl.BlockSpec((tm, HEAD_DIM), lambda i, j: (i % spb, 0)),
            pl.BlockSpec((2, HEAD_DIM), lambda i, j: (0, 0)),
            pl.BlockSpec((2, HEAD_DIM), lambda i, j: (0, 0)),
        ],
        out_specs=[
            pl.BlockSpec((tm, tn), lambda i, j: (i, jnp.clip(j, 0, j_kv - 1))),
            pl.BlockSpec((tm, tn), lambda i, j: (i, 0)),
            *grp_specs,
            pl.BlockSpec((tm, tn), lambda i, j: (i, jnp.clip(j - j_g, 0, n_j - j_g - 1))),
        ],
        out_shape=[
            jax.ShapeDtypeStruct((t, A_Q), MXU_DTYPE),
            jax.ShapeDtypeStruct((t, 2 * A_KV), MXU_DTYPE),
            *grp_shapes,
            jax.ShapeDtypeStruct((t, 2 * d), MXU_DTYPE),
        ],
        scratch_shapes=[pltpu.VMEM((tm, d), MXU_DTYPE), pltpu.VMEM((tn // HEAD_DIM, tm, HEAD_DIM), F32)],
        compiler_params=_params(("arbitrary", "arbitrary")),
        name="inproj",
    )(x2, mod3, n1.reshape(1, d), w_in, w_partner, cos, sin,
      jnp.stack([qn, _rope_partner(qn)]), jnp.stack([kn, _rope_partner(kn)]))


def _attn_a_kernel(q_ref, k_ref, v_ref, o_ref):
    k = k_ref[...]
    v = v_ref[...]

    tq = q_ref.shape[0]
    units = [(slice(r * tq // ATT_ROW_SPLIT, (r + 1) * tq // ATT_ROW_SPLIT),
              slice(hh * HEAD_DIM, (hh + 1) * HEAD_DIM))
             for hh in range(A_GROUP) for r in range(ATT_ROW_SPLIT)]

    def scores(u):
        rows, cols = units[u]
        return lax.dot_general(q_ref[rows, cols], k, (((1,), (1,)), ((), ())), preferred_element_type=F32)

    s_next = scores(0)
    for u, (rows, cols) in enumerate(units):
        s = s_next
        if u + 1 < len(units):
            s_next = scores(u + 1)
        m = jnp.max(s, axis=-1, keepdims=True)
        p = jnp.exp(s - m)
        l = jnp.sum(p, axis=-1, keepdims=True)
        o = jnp.dot(p.astype(v.dtype), v, preferred_element_type=F32)
        o_ref[rows, cols] = (o / l).astype(o_ref.dtype)


def _attn_a(qa, kva, batch, seq):
    t = qa.shape[0]
    tq = min(ATT_TQ, seq)
    nq = seq // tq
    gw = A_GROUP * HEAD_DIM
    return pl.pallas_call(
        _attn_a_kernel,
        grid=(batch, A_KV_HEADS, nq),
        in_specs=[
            pl.BlockSpec((tq, gw), lambda b, kh, qi: (b * nq + qi, kh)),
            pl.BlockSpec((seq, HEAD_DIM), lambda b, kh, qi: (b, kh)),
            pl.BlockSpec((seq, HEAD_DIM), lambda b, kh, qi: (b, A_KV_HEADS + kh)),
        ],
        out_specs=pl.BlockSpec((tq, gw), lambda b, kh, qi: (b * nq + qi, kh)),
        out_shape=jax.ShapeDtypeStruct((t, A_Q), MXU_DTYPE),
        compiler_params=_params(("arbitrary", "arbitrary", "arbitrary")),
        name="attn_dense",
    )(qa, kva, kva)


def _t5_bucket(rel):
    nb = REL_BUCKETS // 2
    ret = (rel > 0).astype(np.int32) * nb
    n = np.abs(rel)
    max_exact = nb // 2
    large = max_exact + (np.log(np.maximum(n, 1) / max_exact) / np.log(REL_MAX_DIST / max_exact)
                         * (nb - max_exact)).astype(np.int32)
    large = np.minimum(large, nb - 1)
    return (ret + np.where(n < max_exact, n, large)).astype(np.int32)


def _band_bias(table_g, dilation, tq, w):
    h = table_g.shape[1]
    offs = np.arange(-DIL_SIDE, DIL_SIDE + 1)
    diag = jnp.take(table_g, jnp.asarray(_t5_bucket(offs * dilation)), axis=0).astype(F32).T
    diag = jnp.concatenate([diag, jnp.full((h, 1), NEG, F32)], axis=1)
    n = tq + w
    u = np.arange(n)
    delta = np.where(u <= w - 1, u, u - n)
    out = []
    for var in range(3):
        m = delta - DIL_SIDE * var
        idx = np.where(np.abs(m) <= DIL_SIDE, m + DIL_SIDE, 2 * DIL_SIDE + 1)
        v = jnp.take(diag, jnp.asarray(idx), axis=1)
        toep = jnp.tile(v, (1, tq))[:, :tq * (n - 1)].reshape(h, tq, n - 1)[:, :, :w]
        out.append(toep)
    return jnp.stack(out, axis=0)


def _dil_kernel(q_ref, k_ref, v_ref, bias_ref, o_ref, lse_ref, *, length, tq, w):
    rp = q_ref.shape[0]
    nblk = length // tq
    lane = lax.broadcasted_iota(jnp.int32, (tq, LANES), 1)

    def blocks(units):
        work = []
        for ri, c in units:
            if isinstance(c, int):
                q0 = c * tq
                ws = min(max(q0 - DIL_SIDE, 0), length - w)
            else:
                q0 = pl.multiple_of(c * tq, tq)
                ws = pl.multiple_of(jnp.clip(q0 - DIL_SIDE, 0, length - w), DIL_SIDE)
            var = (q0 - ws) // DIL_SIDE
            for hh in range(B_HEADS_PER_GROUP):
                sl = slice(hh * HEAD_DIM, (hh + 1) * HEAD_DIM)
                q = q_ref[ri, pl.ds(q0, tq), sl]
                kw = k_ref[ri, pl.ds(ws, w), sl]
                s = lax.dot_general(q, kw, (((1,), (1,)), ((), ())), preferred_element_type=F32)
                work.append(dict(ri=ri, q0=q0, ws=ws, hh=hh, sl=sl, s=s + bias_ref[var, hh]))
        for it in work:
            it['m'] = jnp.max(it['s'], axis=-1, keepdims=True)
            p = jnp.exp(it['s'] - it['m'])
            it['l'] = jnp.sum(p, axis=-1, keepdims=True)
            it['p'] = p.astype(v_ref.dtype)
        for it in work:
            vw = v_ref[it['ri'], pl.ds(it['ws'], w), it['sl']]
            o = jnp.dot(it['p'], vw, preferred_element_type=F32)
            o_ref[it['ri'], pl.ds(it['q0'], tq), it['sl']] = (o / it['l']).astype(o_ref.dtype)
        for u in range(len(units)):
            lse_blk = jnp.zeros((tq, LANES), F32)
            for it in work[u * B_HEADS_PER_GROUP:(u + 1) * B_HEADS_PER_GROUP]:
                lse_blk = jnp.where(lane == it['hh'], it['m'] + jnp.log(it['l']), lse_blk)
            lse_ref[work[u * B_HEADS_PER_GROUP]['ri'], pl.ds(work[u * B_HEADS_PER_GROUP]['q0'], tq), :] = lse_blk

    if rp * nblk <= DIL_UNROLL:
        blocks([(ri, c) for ri in range(rp) for c in range(nblk)])
    else:
        assert rp == 1 and nblk % DIL_UNROLL == 0

        def body(c4, carry):
            blocks([(0, c4 * DIL_UNROLL + j) for j in range(DIL_UNROLL)])
            return carry

        lax.fori_loop(0, nblk // DIL_UNROLL, body, 0)


def _dilated_group(qkv, table_g, window, dilation):
    assert (window // 2) // dilation == DIL_SIDE
    batch, _, length, _ = qkv.shape
    assert length % DIL_SIDE == 0
    tq = min(DIL_TQ, length)
    w = min(length, tq + 2 * DIL_SIDE)
    assert length % tq == 0
    gw = B_HEADS_PER_GROUP * HEAD_DIM
    bias = _band_bias(table_g, dilation, tq, w)
    kern = functools.partial(_dil_kernel, length=length, tq=tq, w=w)
    rp = min(dilation, max(1, DIL_UNROLL // (length // tq)))
    assert dilation % rp == 0
    return pl.pallas_call(
        kern,
        grid=(batch, dilation // rp),
        in_specs=[
            pl.BlockSpec((None, rp, length, gw), lambda b, r: (b, r, 0, 0)),
            pl.BlockSpec((None, rp, length, gw), lambda b, r: (b, r, 0, 1)),
            pl.BlockSpec((None, rp, length, gw), lambda b, r: (b, r, 0, 2)),
            pl.BlockSpec((3, B_HEADS_PER_GROUP, tq, w), lambda b, r: (0, 0, 0, 0)),
        ],
        out_specs=[
            pl.BlockSpec((None, rp, length, gw), lambda b, r: (b, r, 0, 0)),
            pl.BlockSpec((None, rp, length, LANES), lambda b, r: (b, r, 0, 0)),
        ],
        out_shape=[
            jax.ShapeDtypeStruct((batch, dilation, length, gw), MXU_DTYPE),
            jax.ShapeDtypeStruct((batch, dilation, length, LANES), F32),
        ],
        compiler_params=_params(("arbitrary", "arbitrary")),
        name=f"attn_dilated_{dilation}",
    )(qkv, qkv, qkv, bias)


def _post_kernel(x_ref, ya_ref, o1_ref, o2_ref, o3_ref, l1_ref, l2_ref, l3_ref, ga_ref, gb_ref,
                 mod_ref, n2_ref, wa_ref, wb_ref, wo_ref, wr_ref, br_ref,
                 x1_ref, h2_ref, rt_ref, o_scr, l_scr):
    tm = x_ref.shape[0]
    o_refs, l_refs = (o1_ref, o2_ref, o3_ref), (l1_ref, l2_ref, l3_ref)
    for gi, (_, dil) in enumerate(B_PAIRS):
        if dil == 1:
            continue
        for r in range(dil):
            rows = pl.ds(r, tm // dil, stride=dil)
            l_scr[gi, rows, :] = l_refs[gi][r]
            for jh in range(B_HEADS_PER_GROUP):
                o_scr[gi, jh, rows, :] = o_refs[gi][r, :, jh * HEAD_DIM:(jh + 1) * HEAD_DIM].astype(F32)

    def lse_rows(gi, rs):
        return l_refs[gi][0, rs, :] if B_PAIRS[gi][1] == 1 else l_scr[gi, rs, :]

    def out_rows(gi, jh, rs):
        if B_PAIRS[gi][1] == 1:
            return o_refs[gi][0, rs, jh * HEAD_DIM:(jh + 1) * HEAD_DIM].astype(F32)
        return o_scr[gi, jh, rs, :]

    groups = [dict(rs=slice(g * tm // POST_SPLIT, (g + 1) * tm // POST_SPLIT)) for g in range(POST_SPLIT)]
    for it in groups:
        rs = it['rs']
        lses = [lse_rows(gi, rs) for gi in range(B_GROUPS)]
        parts = []
        for jh in range(B_HEADS_PER_GROUP):
            lj = [lg[:, jh:jh + 1] for lg in lses]
            mx = jnp.maximum(jnp.maximum(lj[0], lj[1]), lj[2])
            ex = [jnp.exp(v - mx) for v in lj]
            inv = 1.0 / (ex[0] + ex[1] + ex[2])
            acc = (ex[0] * inv) * out_rows(0, jh, rs)
            acc = acc + (ex[1] * inv) * out_rows(1, jh, rs)
            acc = acc + (ex[2] * inv) * out_rows(2, jh, rs)
            parts.append(acc.astype(MXU_DTYPE))
        it['yb'] = jnp.concatenate(parts, axis=1)
    for it in groups:
        it['a'] = jnp.dot(ya_ref[it['rs'], :], wa_ref[...], preferred_element_type=F32)
        it['bb'] = jnp.dot(it['yb'], wb_ref[...], preferred_element_type=F32)
    for it in groups:
        rs = it['rs']
        merged = ga_ref[rs, :].astype(F32) * it['a'] + gb_ref[rs, :].astype(F32) * it['bb']
        it['merged'] = merged.astype(MXU_DTYPE)
    for it in groups:
        it['out'] = jnp.dot(it['merged'], wo_ref[...], preferred_element_type=F32)
    for it in groups:
        rs = it['rs']
        x1 = x_ref[rs, :] + mod_ref[2:3, :] * it['out']
        x1_ref[rs, :] = x1
        h2 = _rms(x1) * n2_ref[...]
        h2 = h2 * (1.0 + mod_ref[4:5, :]) + mod_ref[3:4, :]
        h2_ref[rs, :] = _pack_bf16_pairs(h2)
        it['h_hi'] = h2.astype(MXU_DTYPE)
        it['h_lo'] = (h2 - it['h_hi'].astype(F32)).astype(MXU_DTYPE)
    for it in groups:
        r_hi = jnp.dot(it['h_hi'], wr_ref[...], preferred_element_type=F32)
        r_lo = jnp.dot(it['h_lo'], wr_ref[:, :LANES], preferred_element_type=F32)
        it['r'] = r_hi[:, :LANES] + r_hi[:, LANES:] + r_lo + br_ref[...]
    for it in groups:
        rt_ref[it['rs'], :] = _route(it['r'])


def _route(r):
    lane = lax.broadcasted_iota(jnp.int32, r.shape, 1)
    lanef = lane.astype(F32)
    ninf = -jnp.inf
    gmask = lane < N_GROUPS
    gl = jnp.where(gmask, r, ninf)
    gmax = jnp.max(gl, axis=-1, keepdims=True)
    grp = jnp.min(jnp.where(gl == gmax, lanef, float(LANES)), axis=-1, keepdims=True)
    gsum = jnp.sum(jnp.where(gmask, jnp.exp(jnp.where(gmask, r, gmax) - gmax), 0.0), axis=-1, keepdims=True)
    gate = 1.0 / gsum
    lo = N_GROUPS + grp * EXPERTS_PER_GROUP
    emask = jnp.logical_and(lanef >= lo, lanef < lo + EXPERTS_PER_GROUP)
    el = jnp.where(emask, r, ninf)
    v1 = jnp.max(el, axis=-1, keepdims=True)
    i1 = jnp.min(jnp.where(el == v1, lanef, float(LANES)), axis=-1, keepdims=True)
    el2 = jnp.where(lanef == i1, ninf, el)
    v2 = jnp.max(el2, axis=-1, keepdims=True)
    i2 = jnp.min(jnp.where(el2 == v2, lanef, float(LANES)), axis=-1, keepdims=True)
    tt = jnp.exp(v2 - v1)
    w1 = 1.0 / (1.0 + tt)
    w2 = tt * w1
    return jnp.where(lane == 0, i1 - N_GROUPS,
                     jnp.where(lane == 1, i2 - N_GROUPS,
                               jnp.where(lane == 2, gate * w1,
                                         jnp.where(lane == 3, gate * w2, 0.0))))


def _post(x2, ya, obs, lses, gates, mod3, boff, seq, n2, wa, wb, wo, wr, br):
    t, d = x2.shape
    tm = POST_TM
    assert t % tm == 0 and seq % tm == 0
    spb = seq // tm
    row = lambda i: (i, 0)
    const = lambda i: (0, 0)
    res = lambda i: (i // spb, 0, i % spb, 0)
    o_specs, l_specs = [], []
    for _, dil in B_PAIRS:
        assert tm % dil == 0 and (tm // dil) % 16 == 0
        o_specs.append(pl.BlockSpec((None, dil, tm // dil, B_OUT), res))
        l_specs.append(pl.BlockSpec((None, dil, tm // dil, LANES), res))
    return pl.pallas_call(
        _post_kernel,
        grid=(t // tm,),
        in_specs=[
            pl.BlockSpec((tm, d), row),
            pl.BlockSpec((tm, A_Q), row),
            *o_specs,
            *l_specs,
            pl.BlockSpec((tm, d), lambda i: (i, 0)),
            pl.BlockSpec((tm, d), lambda i: (i, 1)),
            pl.BlockSpec((None, 6, d), lambda i: (boff + i // spb, 0, 0)),
            pl.BlockSpec((1, d), const),
            pl.BlockSpec((A_Q, d), const, pipeline_mode=pl.Buffered(1)),
            pl.BlockSpec((B_OUT, d), const, pipeline_mode=pl.Buffered(1)),
            pl.BlockSpec((d, d), const, pipeline_mode=pl.Buffered(1)),
            pl.BlockSpec((d, 2 * LANES), const, pipeline_mode=pl.Buffered(1)),
            pl.BlockSpec((1, LANES), const),
        ],
        out_specs=[pl.BlockSpec((tm, d), row), pl.BlockSpec((tm, d // 2), row), pl.BlockSpec((tm, LANES), row)],
        out_shape=[jax.ShapeDtypeStruct((t, d), F32), jax.ShapeDtypeStruct((t, d // 2), jnp.uint32),
                   jax.ShapeDtypeStruct((t, LANES), F32)],
        scratch_shapes=[pltpu.VMEM((B_GROUPS, B_HEADS_PER_GROUP, tm, HEAD_DIM), F32),
                        pltpu.VMEM((B_GROUPS, tm, LANES), F32)],
        compiler_params=_params(("arbitrary",)),
        name="post_attn",
    )(x2, ya, obs[0], obs[1], obs[2], lses[0], lses[1], lses[2], gates, gates, mod3, n2.reshape(1, d),
      wa, wb, wo, wr, br)


def _slot_plan(experts):
    a = experts.shape[0]
    c = RANK_CHUNK
    assert a % c == 0 and a % MOE_TM == 0
    oh = (experts[:, None] == jnp.arange(N_EXPERTS, dtype=jnp.int32)[None, :])
    oh3 = oh.astype(jnp.bfloat16).reshape(a // c, c, N_EXPERTS)
    tri = (np.arange(c)[:, None] > np.arange(c)[None, :])
    within = jnp.einsum('ij,cjk->cik', jnp.asarray(tri, jnp.bfloat16), oh3, preferred_element_type=F32)
    tot = jnp.sum(oh3.astype(F32), axis=1)
    off = jnp.cumsum(tot, axis=0) - tot
    counts = jnp.sum(tot, axis=0).astype(jnp.int32)
    padded = (counts + MOE_TM - 1) // MOE_TM * MOE_TM
    pend = jnp.cumsum(padded)
    pstart = pend - padded
    slot = within + off[:, None, :] + pstart.astype(F32)[None, None, :]
    dest = jnp.sum(jnp.where(oh.reshape(a // c, c, N_EXPERTS), slot, 0.0), axis=-1).reshape(a).astype(jnp.int32)
    n_blocks = a // MOE_TM + N_EXPERTS
    idx = jnp.arange(n_blocks, dtype=jnp.int32)
    block_start = idx * MOE_TM
    block_e = jnp.minimum(jnp.sum((pend[None, :] <= block_start[:, None]).astype(jnp.int32), axis=1),
                          N_EXPERTS - 1)
    nb_used = pend[-1] // MOE_TM
    used = idx < nb_used
    last = nb_used - 1
    onehot_e = block_e[:, None] == jnp.arange(N_EXPERTS, dtype=jnp.int32)[None, :]
    seg_end = jnp.sum(jnp.where(onehot_e, (pstart + counts)[None, :], 0), axis=1)
    nv = jnp.where(used, jnp.clip(seg_end - block_start, 0, MOE_TM), 0).astype(jnp.int32)
    def at_expert(table):
        return jnp.sum(jnp.where(onehot_e, table[None, :], 0), axis=1)

    nblk_e = padded // MOE_TM
    pos_in_e = idx - at_expert(pstart // MOE_TM)
    local = pos_in_e % MOE_PAIR
    pair_size = jnp.minimum(MOE_PAIR, at_expert(nblk_e) - (pos_in_e - local))
    pairs_e = (nblk_e + MOE_PAIR - 1) // MOE_PAIR
    pair_id = at_expert(jnp.cumsum(pairs_e) - pairs_e) + pos_in_e // MOE_PAIR
    f_first = pair_id & 1
    base = 2 * (idx - local)
    pos_a = jnp.where(used, base + local, 2 * idx)
    pos_b = jnp.where(used, base + 2 * pair_size - 1 - local, 2 * idx + 1)
    be_last = jnp.sum(jnp.where(idx == last, block_e, 0))
    e_col = jnp.where(used, block_e, be_last)
    x_col = jnp.minimum(idx, last)
    slot_col = jnp.where(used, local, 0)
    o_first = jnp.where(used, idx - local + pair_size - 1, idx)
    zeros = jnp.zeros_like(idx)
    flag_a = jnp.where(used, EXP_FIRST, EXP_ZERO) + EXP_SLOT * slot_col
    flag_b = jnp.where(used, EXP_LAST, 0) + EXP_SLOT * slot_col
    rows_a = jnp.stack([x_col, e_col, jnp.where(used, f_first, zeros), nv, flag_a, o_first], axis=1)
    rows_b = jnp.stack([x_col, e_col, jnp.where(used, 1 - f_first, zeros), nv, flag_b, idx], axis=1)
    steps = jnp.zeros((2 * n_blocks, 6), jnp.int32).at[jnp.concatenate([pos_a, pos_b])].set(
        jnp.concatenate([rows_a, rows_b]).astype(jnp.int32))
    step_idx = jnp.arange(2 * n_blocks, dtype=jnp.int32)
    f_last = jnp.sum(jnp.where(step_idx == 2 * nb_used - 1, steps[:, 2], 0))
    f_col = jnp.where(step_idx < 2 * nb_used, steps[:, 2], f_last)
    plan = dict(xblk=steps[:, 0], expert=steps[:, 1], fhalf=f_col, nv=steps[:, 3], flags=steps[:, 4],
                oblk=steps[:, 5], nv_block=nv)
    return dest, plan, n_blocks


def _dispatch_kernel(nvb_ref, dest_ref, ha_ref, hb_ref, xs_ref, zbuf, sem, zsem, *, steps_a):
    i = pl.program_id(0)
    tm = ha_ref.shape[0]
    n_blocks = nvb_ref.shape[0]

    def zero_block(b):
        return pltpu.make_async_copy(zbuf, xs_ref.at[pl.ds(pl.multiple_of(b * MOE_TM, MOE_TM), MOE_TM)], zsem)

    @pl.when(i == 0)
    def _():
        zbuf[...] = jnp.zeros_like(zbuf)

        def zstart(b, carry):
            pl.when(nvb_ref[b] < MOE_TM)(lambda: zero_block(b).start())
            return carry

        def zwait(b, carry):
            pl.when(nvb_ref[b] < MOE_TM)(lambda: zero_block(b).wait())
            return carry

        lax.fori_loop(0, n_blocks, zstart, 0)
        lax.fori_loop(0, n_blocks, zwait, 0)

    def scatter(h_ref):
        def start(t, carry):
            for k in range(TOP_K):
                d = dest_ref[0, TOP_K * t + k]
                pltpu.make_async_copy(h_ref.at[pl.ds(t, 1)], xs_ref.at[pl.ds(d, 1)], sem).start(priority=k % 2)
            return carry

        lax.fori_loop(0, tm, start, 0, unroll=8)
        for _ in range(TOP_K):
            pltpu.make_async_copy(h_ref, xs_ref.at[pl.ds(0, tm)], sem).wait()

    pl.when(i < steps_a)(lambda: scatter(ha_ref))
    pl.when(i >= steps_a)(lambda: scatter(hb_ref))


def _dispatch(dest, nv_block, h_a, h_b):
    ta, d = h_a.shape
    tb = h_b.shape[0]
    tm = DISP_TM
    assert ta % tm == 0 and tb % tm == 0
    steps_a, steps_b = ta // tm, tb // tm
    n_slots = nv_block.shape[0] * MOE_TM
    grid_spec = pltpu.PrefetchScalarGridSpec(
        num_scalar_prefetch=1,
        grid=(steps_a + steps_b,),
        in_specs=[
            pl.BlockSpec((None, 1, TOP_K * tm), lambda i, nvb: (i, 0, 0), memory_space=pltpu.SMEM),
            pl.BlockSpec((tm, d), lambda i, nvb: (jnp.minimum(i, steps_a - 1), 0)),
            pl.BlockSpec((tm, d), lambda i, nvb: (jnp.maximum(i - steps_a, 0), 0)),
        ],
        out_specs=pl.BlockSpec(memory_space=pl.ANY),
        scratch_shapes=[pltpu.VMEM((MOE_TM, d), h_a.dtype), pltpu.SemaphoreType.DMA(()),
                        pltpu.SemaphoreType.DMA(())],
    )
    return pl.pallas_call(
        functools.partial(_dispatch_kernel, steps_a=steps_a),
        grid_spec=grid_spec,
        out_shape=jax.ShapeDtypeStruct((n_slots, d), h_a.dtype),
        compiler_params=_params(("arbitrary",)),
        name="moe_dispatch",
    )(nv_block, dest.reshape(steps_a + steps_b, 1, TOP_K * tm), h_a, h_b)


def _expert_kernel(xb_ref, e_ref, fh_ref, nv_ref, fl_ref, ob_ref, x_ref, wg_ref, wu_ref, wd_ref, y_ref,
                   wg_s, wu_s, wd_s, acc):
    del xb_ref, e_ref, fh_ref, ob_ref
    s = pl.program_id(0)
    nv = nv_ref[s]
    flags = fl_ref[s]
    first = (flags & EXP_FIRST) > 0
    slot = (flags // EXP_SLOT) & 3
    tm = x_ref.shape[0]

    @pl.when(s == 0)
    def _():
        acc[...] = jnp.zeros_like(acc)

    def compute(n_rows):
        wg_s[...] = wg_ref[...].astype(wg_s.dtype)
        wu_s[...] = wu_ref[...].astype(wu_s.dtype)
        wd_s[...] = wd_ref[...].astype(wd_s.dtype)
        x = _unpack_bf16_pairs(x_ref[0:n_rows, :]).astype(MXU_DTYPE)
        g = jnp.dot(x, wg_s[...], preferred_element_type=F32)
        u = jnp.dot(x, wu_s[...], preferred_element_type=F32)
        h = (g * _sigmoid(g)) * u
        part = jnp.dot(h.astype(MXU_DTYPE), wd_s[...], preferred_element_type=F32)

        total = jnp.where(first, part, acc[slot, 0:n_rows, :] + part)
        acc[slot, 0:n_rows, :] = total
        y_ref[0:n_rows, :] = _pack_bf16_pairs(total)
        if n_rows < tm:
            y_ref[n_rows:tm, :] = jnp.zeros((tm - n_rows, y_ref.shape[1]), y_ref.dtype)

    for n_rows in range(MOE_SUB, tm + 1, MOE_SUB):
        pl.when(jnp.logical_and(nv > n_rows - MOE_SUB, nv <= n_rows))(functools.partial(compute, n_rows))

    @pl.when((flags & EXP_ZERO) > 0)
    def _():
        y_ref[...] = jnp.zeros_like(y_ref)


def _experts(xs, plan, wg, wu, wd):
    n_slots = xs.shape[0]
    _, d, ff = wg.shape
    tm = MOE_TM
    n_blocks = n_slots // tm
    assert ff % 2 == 0 and xs.shape[1] * 2 == d and 1 <= MOE_PAIR <= 4
    fh = ff // 2
    grid_spec = pltpu.PrefetchScalarGridSpec(
        num_scalar_prefetch=6,
        grid=(2 * n_blocks,),
        in_specs=[
            pl.BlockSpec((tm, d // 2), lambda s, xb, e, fhalf, nv, fl, ob: (xb[s], 0)),
            pl.BlockSpec((None, d, fh), lambda s, xb, e, fhalf, nv, fl, ob: (e[s], 0, fhalf[s])),
            pl.BlockSpec((None, d, fh), lambda s, xb, e, fhalf, nv, fl, ob: (e[s], 0, fhalf[s])),
            pl.BlockSpec((None, fh, d), lambda s, xb, e, fhalf, nv, fl, ob: (e[s], fhalf[s], 0)),
        ],
        out_specs=pl.BlockSpec((tm, d // 2), lambda s, xb, e, fhalf, nv, fl, ob: (ob[s], 0)),
        scratch_shapes=[pltpu.VMEM((d, fh), MXU_DTYPE), pltpu.VMEM((d, fh), MXU_DTYPE),
                        pltpu.VMEM((fh, d), MXU_DTYPE), pltpu.VMEM((MOE_PAIR, tm, d), F32)],
    )
    return pl.pallas_call(
        _expert_kernel,
        grid_spec=grid_spec,
        out_shape=jax.ShapeDtypeStruct((n_slots, d // 2), jnp.uint32),
        compiler_params=_params(("arbitrary",), MOE_VMEM_LIMIT_BYTES),
        name="moe_experts",
    )(plan['xblk'], plan['expert'], plan['fhalf'], plan['nv'], plan['flags'], plan['oblk'], xs, wg, wu, wd)


def _combine_kernel(dest_ref, dest_next_ref, x1_ref, rt_ref, mod_ref, nf_ref, ys_ref, o_ref, ybuf, sem, *, final):
    i = pl.program_id(0)
    tm = x1_ref.shape[0]
    slot = i % 2

    def issue(idx_ref, s):
        def start(t, carry):
            for k in range(TOP_K):
                d = idx_ref[0, TOP_K * t + k]
                pltpu.make_async_copy(ys_ref.at[pl.ds(d, 1)], ybuf.at[s, pl.ds(k * tm + t, 1)],
                                      sem.at[s]).start(priority=k % 2)
            return carry

        lax.fori_loop(0, tm, start, 0, unroll=8)

    @pl.when(i == 0)
    def _():
        issue(dest_ref, 0)

    @pl.when(i + 1 < pl.num_programs(0))
    def _():
        issue(dest_next_ref, 1 - slot)

    for k in range(TOP_K):
        pltpu.make_async_copy(ys_ref.at[pl.ds(0, tm)], ybuf.at[slot, pl.ds(k * tm, tm)], sem.at[slot]).wait()
    rt = rt_ref[...]
    moe = (_unpack_bf16_pairs(ybuf[slot, 0:tm, :]) * rt[:, 2:3]
           + _unpack_bf16_pairs(ybuf[slot, tm:2 * tm, :]) * rt[:, 3:4])
    x2 = x1_ref[...] + mod_ref[5:6, :] * moe
    if final:
        x2 = _rms(x2) * nf_ref[...]
    o_ref[...] = x2


def _combine(dest_blocks, x1, rt, mod3, boff, seq, nf, ys, final):
    t, d = x1.shape
    tm = COMB_TM
    spb = seq // tm
    kern = functools.partial(_combine_kernel, final=final)
    n_steps = t // tm
    return pl.pallas_call(
        kern,
        grid=(n_steps,),
        in_specs=[
            pl.BlockSpec((None, 1, TOP_K * tm), lambda i: (i, 0, 0), memory_space=pltpu.SMEM),
            pl.BlockSpec((None, 1, TOP_K * tm), lambda i: (jnp.minimum(i + 1, n_steps - 1), 0, 0),
                         memory_space=pltpu.SMEM),
            pl.BlockSpec((tm, d), lambda i: (i, 0)),
            pl.BlockSpec((tm, LANES), lambda i: (i, 0)),
            pl.BlockSpec((None, 6, d), lambda i: (boff + i // spb, 0, 0)),
            pl.BlockSpec((1, d), lambda i: (0, 0)),
            pl.BlockSpec(memory_space=pl.ANY),
        ],
        out_specs=pl.BlockSpec((tm, d), lambda i: (i, 0)),
        out_shape=jax.ShapeDtypeStruct((t, d), F32),
        scratch_shapes=[pltpu.VMEM((2, TOP_K * tm, d // 2), jnp.uint32), pltpu.SemaphoreType.DMA((2,))],
        compiler_params=_params(("arbitrary",)),
        name="moe_combine",
    )(dest_blocks, dest_blocks, x1, rt, mod3, nf.reshape(1, d), ys)


def kernel(x_prompt, x_sample, c_prompt, c_sample, w_ada, b_ada, norm1, w_in, q_norm, k_norm, w_branch_a, w_branch_b, w_out, rel_bias, norm2, w_route_group, b_route_group, w_route_expert, b_route_expert, w_gate_exp, w_up_exp, w_down_exp, norm_final):
    depth = w_ada.shape[0]
    d = x_prompt.shape[-1]
    trunks = []
    boff = 0
    for x, c in ((x_prompt, c_prompt), (x_sample, c_sample)):
        b, s, _ = x.shape
        trunks.append(dict(x=x.reshape(b * s, d), batch=b, seq=s, boff=boff))
        boff += b
    n_cond = boff
    rows = -(-n_cond // 8) * 8
    c_all = jnp.concatenate([c_prompt, c_sample, jnp.zeros((rows - n_cond, d), F32)], axis=0)
    scale = HEAD_DIM ** -0.5

    for l in range(depth):
        mod3 = _adaln(c_all, w_ada[l], b_ada[l]).reshape(rows, 6, d)
        w_in_l = _inproj_weight(w_in[l].astype(MXU_DTYPE))
        w_partner = _rope_partner(w_in_l[:, :A_Q + 2 * A_KV])
        wa = w_branch_a[l].astype(MXU_DTYPE)
        wb = w_branch_b[l].astype(MXU_DTYPE)
        wo = w_out[l].astype(MXU_DTYPE)
        wr = jnp.concatenate([w_route_group[l], w_route_expert[l],
                              jnp.zeros((d, LANES - N_GROUPS - N_EXPERTS), F32)], axis=1)
        wr_hi = wr.astype(MXU_DTYPE)
        wr = jnp.concatenate([wr_hi, (wr - wr_hi.astype(F32)).astype(MXU_DTYPE)], axis=1)
        br = jnp.concatenate([b_route_group[l], b_route_expert[l],
                              jnp.zeros((LANES - N_GROUPS - N_EXPERTS,), F32)]).reshape(1, LANES)

        for tr in trunks:
            qa, kva, *qkv_groups, gates = _inproj(tr['x'], mod3, tr['boff'], tr['batch'], tr['seq'], norm1[l],
                                                  w_in_l, w_partner, q_norm[l] * scale, k_norm[l])
            ya = _attn_a(qa, kva, tr['batch'], tr['seq'])
            obs, lses = [], []
            for gi, (window, dilation) in enumerate(B_PAIRS):
                hs = slice(gi * B_HEADS_PER_GROUP, (gi + 1) * B_HEADS_PER_GROUP)
                o, lse = _dilated_group(qkv_groups[gi], rel_bias[:, hs], window, dilation)
                obs.append(o)
                lses.append(lse)
            tr['x1'], tr['h2'], tr['rt'] = _post(tr['x'], ya, obs, lses, gates, mod3, tr['boff'], tr['seq'],
                                                 norm2[l], wa, wb, wo, wr, br)

        experts = jnp.concatenate([tr['rt'][:, :TOP_K] for tr in trunks], axis=0).astype(jnp.int32).reshape(-1)
        dest, plan, n_blocks = _slot_plan(experts)
        a0 = 0
        for tr in trunks:
            n_tok = tr['x'].shape[0]
            tr['dest'] = dest[a0:a0 + TOP_K * n_tok]
            a0 += TOP_K * n_tok
        xs = _dispatch(dest, plan['nv_block'], trunks[0]['h2'], trunks[1]['h2'])
        ys = _experts(xs, plan, w_gate_exp[l], w_up_exp[l], w_down_exp[l])
        for tr in trunks:
            n_tok = tr['x'].shape[0]
            tr['x'] = _combine(tr['dest'].reshape(n_tok // COMB_TM, 1, TOP_K * COMB_TM), tr['x1'], tr['rt'],
                               mod3, tr['boff'], tr['seq'], norm_final, ys, final=(l == depth - 1))

    return tuple(tr['x'].reshape(tr['batch'], tr['seq'], d) for tr in trunks)
```
